```python
import math
import jax, jax.numpy as jnp
from jax import lax
import numpy as np

D_MODEL = 2048
BATCH = 1
SEQ = 16384
DEPTH = 2

N_META = 16
GRID_W = 64
Q_BLOCK = 128
MIX_WIDTH = D_MODEL
BRANCH_W = MIX_WIDTH // 4
EPS = 1e-6
ROPE_THETA = 500000.0
AXIAL_THETA = 10000.0

A_V_DIM = 128
A_HEADS = BRANCH_W // A_V_DIM
A_QK_DIM = A_V_DIM // 2
A_ROT = A_QK_DIM // 4

B_HEAD_DIM = 128
B_HEADS = BRANCH_W // B_HEAD_DIM
B_KV_HEADS = B_HEADS // 2

C_V_DIM = 128
C_HEADS = BRANCH_W // C_V_DIM
C_NOPE = 128
C_ROPE = 64
C_Q_LORA = 3 * D_MODEL // 16
C_KV_LORA = D_MODEL // 8

POOL_WINDOWS = (2, 4, 8, 16)
D_GROUPS = len(POOL_WINDOWS)
D_GROUP_DIM = BRANCH_W // D_GROUPS

IN_SIZES = (
    A_HEADS * 2 * A_QK_DIM,
    A_HEADS * 2 * A_QK_DIM,
    A_HEADS * A_V_DIM,
    BRANCH_W,
    B_HEADS * B_HEAD_DIM,
    B_KV_HEADS * B_HEAD_DIM,
    B_KV_HEADS * B_HEAD_DIM,
    BRANCH_W,
    C_Q_LORA,
    C_KV_LORA,
    C_ROPE,
    BRANCH_W,
    BRANCH_W,
    BRANCH_W,
)
IN_COLS = sum(IN_SIZES)

kernel_name = 'hybrid_parallel_heads_encoder'


def rms_norm(x, w):
    x32 = x.astype(jnp.float32)
    y = x32 * lax.rsqrt(jnp.mean(x32 * x32, axis=-1, keepdims=True) + EPS)
    return (y * w.astype(jnp.float32)).astype(x.dtype)


def rope_table(pos, dim, theta):
    inv = theta ** (-jnp.arange(0, dim, 2, dtype=jnp.float32) / dim)
    ang = pos.astype(jnp.float32)[:, None] * inv[None, :]
    return jnp.cos(ang), jnp.sin(ang)


def apply_rope(x, cos, sin):
    half = x.shape[-1] // 2
    x1, x2 = x[..., :half], x[..., half:]
    c = cos.astype(x.dtype)
    s = sin.astype(x.dtype)
    return jnp.concatenate([x1 * c - x2 * s, x1 * s + x2 * c], axis=-1)


def softmax32(s, scale):
    return jax.nn.softmax(s.astype(jnp.float32) * scale, axis=-1)


def over_query_blocks(attend, qs):
    meta = tuple(q[:, :N_META] for q in qs)

    def to_blocks(q):
        b, l = q.shape[:2]
        n = (l - N_META) // Q_BLOCK
        return jnp.moveaxis(q[:, N_META:].reshape(b, n, Q_BLOCK, *q.shape[2:]), 1, 0)

    blocks = tuple(to_blocks(q) for q in qs)
    out = lax.map(lambda qb: attend(*qb), blocks)
    out = jnp.moveaxis(out, 0, 1)
    out = out.reshape(out.shape[0], -1, *out.shape[3:])
    return jnp.concatenate([attend(*meta), out], axis=1)


def diff_attention(q_raw, k_raw, v_raw, q_norm_w, k_norm_w, lam_params, subln_w, lambda_init, cos, sin):
    b, l = q_raw.shape[:2]
    q = rms_norm(q_raw.reshape(b, l, A_HEADS, 2, A_QK_DIM), q_norm_w)
    k = rms_norm(k_raw.reshape(b, l, A_HEADS, 2, A_QK_DIM), k_norm_w)
    c, s = cos[:, None, None, :], sin[:, None, None, :]
    q = jnp.concatenate([apply_rope(q[..., :A_ROT], c, s), q[..., A_ROT:]], axis=-1)
    k = jnp.concatenate([apply_rope(k[..., :A_ROT], c, s), k[..., A_ROT:]], axis=-1)
    v = v_raw.reshape(b, l, A_HEADS, A_V_DIM)
    lp = lam_params.astype(jnp.float32)
    lam = jnp.exp(jnp.sum(lp[0] * lp[1])) - jnp.exp(jnp.sum(lp[2] * lp[3])) + lambda_init
    scale = A_QK_DIM ** -0.5

    def attend(qb):
        p = softmax32(jnp.einsum('bqhmd,bkhmd->bhmqk', qb, k), scale)
        w = (p[:, :, 0] - lam * p[:, :, 1]).astype(v.dtype)
        return jnp.einsum('bhqk,bkhd->bqhd', w, v)

    o = over_query_blocks(attend, (q,))
    o = rms_norm(o, subln_w) * (1.0 - lambda_init)
    return o.reshape(b, l, BRANCH_W)


def axial_gqa(q_raw, k_raw, v_raw, q_norm_w, k_norm_w, cos_r, sin_r, cos_c, sin_c):
    b, l = q_raw.shape[:2]
    q = rms_norm(q_raw.reshape(b, l, B_HEADS, B_HEAD_DIM), q_norm_w)
    k = rms_norm(k_raw.reshape(b, l, B_KV_HEADS, B_HEAD_DIM), k_norm_w)
    v = v_raw.reshape(b, l, B_KV_HEADS, B_HEAD_DIM)
    half = B_HEAD_DIM // 2

    def axial(t):
        return jnp.concatenate([
            apply_rope(t[..., :half], cos_r[:, None, :], sin_r[:, None, :]),
            apply_rope(t[..., half:], cos_c[:, None, :], sin_c[:, None, :])], axis=-1)

    q, k = axial(q), axial(k)
    rep = B_HEADS // B_KV_HEADS
    scale = B_HEAD_DIM ** -0.5

    def attend(qb):
        bq, nq = qb.shape[:2]
        qg = qb.reshape(bq, nq, B_KV_HEADS, rep, B_HEAD_DIM)
        p = softmax32(jnp.einsum('bqgrd,bkgd->bgrqk', qg, k), scale).astype(v.dtype)
        return jnp.einsum('bgrqk,bkgd->bqgrd', p, v).reshape(bq, nq, BRANCH_W)

    return over_query_blocks(attend, (q,))


def latent_attention(cq_raw, ckv_raw, kr_raw, q_lat_w, kv_lat_w, w_uq, w_ukv, q_norm_w, k_norm_w, cos, sin):
    b, l = cq_raw.shape[:2]
    c_q = rms_norm(cq_raw, q_lat_w)
    c_kv = rms_norm(ckv_raw, kv_lat_w)
    q = jnp.einsum('blr,rn->bln', c_q, w_uq).reshape(b, l, C_HEADS, C_NOPE + C_ROPE)
    kv = jnp.einsum('blr,rn->bln', c_kv, w_ukv).reshape(b, l, C_HEADS, C_NOPE + C_V_DIM)
    q_nope = rms_norm(q[..., :C_NOPE], q_norm_w[:C_NOPE])
    q_rope = apply_rope(rms_norm(q[..., C_NOPE:], q_norm_w[C_NOPE:]), cos[:, None, :], sin[:, None, :])
    k_nope = rms_norm(kv[..., :C_NOPE], k_norm_w[:C_NOPE])
    v = kv[..., C_NOPE:]
    k_rope = apply_rope(rms_norm(kr_raw, k_norm_w[C_NOPE:]), cos, sin)
    scale = (C_NOPE + C_ROPE) ** -0.5

    def attend(qn, qr):
        s = (jnp.einsum('bqhd,bkhd->bhqk', qn, k_nope).astype(jnp.float32)
             + jnp.einsum('bqhd,bkd->bhqk', qr, k_rope).astype(jnp.float32))
        p = softmax32(s, scale).astype(v.dtype)
        return jnp.einsum('bhqk,bkhd->bqhd', p, v).reshape(qn.shape[0], qn.shape[1], BRANCH_W)

    return over_query_blocks(attend, (q_nope, q_rope))


def multiscale_pool(u, w_group, scale):
    b, l, _ = u.shape
    u32 = u.astype(jnp.float32)
    cs = jnp.concatenate([jnp.zeros((b, 1, BRANCH_W), jnp.float32), lax.cumsum(u32, axis=1)], axis=1)
    t = jnp.arange(l)
    outs = []
    for g, w in enumerate(POOL_WINDOWS):
        lo = jnp.clip(t - w // 2, 0, l)
        hi = jnp.clip(t + w - w // 2, 0, l)
        sl = slice(g * D_GROUP_DIM, (g + 1) * D_GROUP_DIM)
        csg = cs[..., sl]
        cnt = (hi - lo).astype(jnp.float32)[None, :, None]
        mean = (jnp.take(csg, hi, axis=1) - jnp.take(csg, lo, axis=1)) / cnt
        outs.append(mean - u32[..., sl])
    pooled = jnp.stack(outs, axis=2).astype(u.dtype)
    mixed = jnp.einsum('blgc,gcd->blgd', pooled, w_group).reshape(b, l, BRANCH_W)
    return mixed * scale


def hybrid_layer(x, norm_w, w_in, w_out, a_q_norm, a_k_norm, a_lambda, a_subln, b_q_norm, b_k_norm,
                 c_q_lat_norm, c_kv_lat_norm, c_w_uq, c_w_ukv, c_q_norm, c_k_norm, d_w_group, d_scale,
                 lambda_init, tabs):
    cos_a, sin_a, cos_r, sin_r, cos_c, sin_c, cos_m, sin_m = tabs
    h = rms_norm(x, norm_w)
    proj = jnp.einsum('bld,dn->bln', h, w_in)
    split_points = [int(v) for v in np.cumsum(IN_SIZES)[:-1]]
    (aq, ak, av, ag, bq, bk, bv, bg, cq, ckv, ckr, cg, du, dg) = jnp.split(proj, split_points, axis=-1)
    ya = diff_attention(aq, ak, av, a_q_norm, a_k_norm, a_lambda, a_subln, lambda_init, cos_a, sin_a)
    yb = axial_gqa(bq, bk, bv, b_q_norm, b_k_norm, cos_r, sin_r, cos_c, sin_c)
    yc = latent_attention(cq, ckv, ckr, c_q_lat_norm, c_kv_lat_norm, c_w_uq, c_w_ukv, c_q_norm, c_k_norm, cos_m, sin_m)
    yd = multiscale_pool(du, d_w_group, d_scale)
    y = jnp.concatenate([ya * jax.nn.silu(ag), yb * jax.nn.silu(bg),
                         yc * jax.nn.silu(cg), yd * jax.nn.silu(dg)], axis=-1)
    return x + jnp.einsum('bln,nd->bld', y, w_out)


def setup_inputs(seed: int = 0) -> dict:
    key = jax.random.key(seed)
    ks = jax.random.split(key, 24)
    f32 = jnp.float32

    def nrm(k, shape, scale):
        return jax.random.normal(k, shape, f32) * scale

    def gain(k, shape):
        return 1.0 + 0.02 * jax.random.normal(k, shape, f32)

    return {
        'x': nrm(ks[0], (BATCH, SEQ, D_MODEL), 1.0),
        'meta_tokens': nrm(ks[1], (N_META, D_MODEL), 1.0),
        'norm_w': gain(ks[2], (DEPTH, D_MODEL)),
        'w_in': nrm(ks[3], (DEPTH, D_MODEL, IN_COLS), D_MODEL ** -0.5),
        'w_out': nrm(ks[4], (DEPTH, MIX_WIDTH, D_MODEL), MIX_WIDTH ** -0.5),
        'a_q_norm': gain(ks[5], (DEPTH, A_QK_DIM)),
        'a_k_norm': gain(ks[6], (DEPTH, A_QK_DIM)),
        'a_lambda': nrm(ks[7], (DEPTH, 4, A_QK_DIM), 0.1),
        'a_subln': gain(ks[8], (DEPTH, A_V_DIM)),
        'b_q_norm': gain(ks[9], (DEPTH, B_HEAD_DIM)),
        'b_k_norm': gain(ks[10], (DEPTH, B_HEAD_DIM)),
        'c_q_lat_norm': gain(ks[11], (DEPTH, C_Q_LORA)),
        'c_kv_lat_norm': gain(ks[12], (DEPTH, C_KV_LORA)),
        'c_w_uq': nrm(ks[13], (DEPTH, C_Q_LORA, C_HEADS * (C_NOPE + C_ROPE)), C_Q_LORA ** -0.5),
        'c_w_ukv': nrm(ks[14], (DEPTH, C_KV_LORA, C_HEADS * (C_NOPE + C_V_DIM)), C_KV_LORA ** -0.5),
        'c_q_norm': gain(ks[15], (DEPTH, C_NOPE + C_ROPE)),
        'c_k_norm': gain(ks[16], (DEPTH, C_NOPE + C_ROPE)),
        'd_w_group': nrm(ks[17], (DEPTH, D_GROUPS, D_GROUP_DIM, D_GROUP_DIM), D_GROUP_DIM ** -0.5),
        'd_scale': gain(ks[18], (DEPTH, BRANCH_W)),
    }


def reference(x, meta_tokens, norm_w, w_in, w_out, a_q_norm, a_k_norm, a_lambda, a_subln, b_q_norm, b_k_norm,
              c_q_lat_norm, c_kv_lat_norm, c_w_uq, c_w_ukv, c_q_norm, c_k_norm, d_w_group, d_scale):
    b, n_tok, d = x.shape
    rows = n_tok // GRID_W
    l = N_META + n_tok
    h = jnp.concatenate([jnp.broadcast_to(meta_tokens[None].astype(x.dtype), (b, N_META, d)), x], axis=1)

    pos = jnp.arange(l, dtype=jnp.int32)
    cos_a, sin_a = rope_table(pos, A_ROT, ROPE_THETA)
    cos_m, sin_m = rope_table(pos, C_ROPE, ROPE_THETA)
    row_pos = jnp.concatenate([jnp.full((N_META,), -1, jnp.int32),
                               jnp.repeat(jnp.arange(rows, dtype=jnp.int32), GRID_W)])
    col_pos = jnp.concatenate([jnp.arange(N_META, dtype=jnp.int32),
                               jnp.tile(jnp.arange(GRID_W, dtype=jnp.int32), rows)])
    cos_r, sin_r = rope_table(row_pos, B_HEAD_DIM // 2, AXIAL_THETA)
    cos_c, sin_c = rope_table(col_pos, B_HEAD_DIM // 2, AXIAL_THETA)
    tabs = (cos_a, sin_a, cos_r, sin_r, cos_c, sin_c, cos_m, sin_m)

    for i in range(DEPTH):
        lambda_init = 0.8 - 0.6 * math.exp(-0.3 * i)
        h = hybrid_layer(h, norm_w[i], w_in[i], w_out[i], a_q_norm[i], a_k_norm[i], a_lambda[i], a_subln[i],
                         b_q_norm[i], b_k_norm[i], c_q_lat_norm[i], c_kv_lat_norm[i], c_w_uq[i], c_w_ukv[i],
                         c_q_norm[i], c_k_norm[i], d_w_group[i], d_scale[i], lambda_init, tabs)
    return h[:, N_META:]
```

```python
import functools
import math

import jax
import jax.numpy as jnp
from jax import lax
from jax.experimental import pallas as pl
from jax.experimental.pallas import tpu as pltpu

F32 = jnp.float32
BF16 = jnp.bfloat16

D_MODEL = 2048
N_META = 16
GRID_W = 64
EPS = 1e-6
ROPE_THETA = 500000.0
AXIAL_THETA = 10000.0
BRANCH_W = 512
LANES = 128
SEQ_PAD = 128
ROW_TILE = 384
KV_CHUNK = 512
A_QK = 64
A_ROT = 16
B_HD = 128
C_NOPE = 128
C_ROPE = 64
C_Q_LORA = 384
C_KV_LORA = 256
POOL_WINDOWS = (2, 4, 8, 16)
POOL_HALO = 8
LOG2E = 1.4426950408889634
NEG_BIG = -1e30
VMEM_LIMIT = 56 * 1024 * 1024

P_AQ, P_AK, P_AV = 0, 512, 1024
P_BQ, P_BK, P_BV = 1536, 2048, 2304
P_CQ, P_CKV, P_CKR = 2560, 2944, 3200
P_GATE, P_DU, P_END = 3328, 5376, 5888


def _seg_ones(seg):
    r = lax.broadcasted_iota(jnp.int32, (LANES, LANES), 0) // seg
    c = lax.broadcasted_iota(jnp.int32, (LANES, LANES), 1) // seg
    return jnp.where(r == c, 1.0, 0.0).astype(BF16)


def _seg_mean_sq(x, ones_mat, seg):
    xx = x * x
    hi = xx.astype(BF16)
    lo = (xx - hi.astype(F32)).astype(BF16)
    s = (jnp.dot(hi, ones_mat, preferred_element_type=F32)
         + jnp.dot(lo, ones_mat, preferred_element_type=F32))
    return s * (1.0 / seg)


def _rope(x, c, s, half):
    lane = lax.broadcasted_iota(jnp.int32, x.shape, 1)
    fwd = pltpu.roll(x, LANES - half, axis=1)
    bwd = pltpu.roll(x, half, axis=1)
    partner = jnp.where((lane % (2 * half)) < half, fwd, bwd)
    return x * c + partner * s


def _proj_kernel(h_ref, nw_ref, win_ref, wuq_ref, wukv_ref,
                 aqn_ref, akn_ref, bqn_ref, bkn_ref, cql_ref, ckvl_ref,
                 cqnn_ref, cqnr_ref, cknn_ref, cknr_ref,
                 ca_ref, sa_ref, cb_ref, sb_ref, cc_ref, sc_ref,
                 qa_ref, ka_ref, va_ref, qb_ref, kb_ref, vb_ref,
                 qc_ref, kc_ref, vc_ref, sg_ref, du_ref):
    h = h_ref[...]
    ms = jnp.mean(h * h, axis=-1, keepdims=True)
    hb = (h * lax.rsqrt(ms + EPS) * nw_ref[...]).astype(BF16)

    def mm(lo, hi):
        return jnp.dot(hb, win_ref[:, lo:hi], preferred_element_type=F32)

    ones64 = _seg_ones(64)
    ones128 = _seg_ones(128)
    ca, sa = ca_ref[...], sa_ref[...]
    cb, sb = cb_ref[...], sb_ref[...]
    cc, sc = cc_ref[...], sc_ref[...]

    pa = mm(P_AQ, P_AV)
    a_scale = (A_QK ** -0.5) * LOG2E
    for blk in range(8):
        x = pa[:, blk * LANES:(blk + 1) * LANES]
        w = aqn_ref[...] if blk < 4 else akn_ref[...]
        x = x * lax.rsqrt(_seg_mean_sq(x, ones64, 64) + EPS) * w
        x = _rope(x, ca, sa, A_ROT // 2)
        if blk < 4:
            qa_ref[:, blk * LANES:(blk + 1) * LANES] = (x * a_scale).astype(BF16)
        else:
            ka_ref[:, (blk - 4) * LANES:(blk - 3) * LANES] = x.astype(BF16)
    va_ref[...] = mm(P_AV, P_BQ).astype(BF16)

    pb = mm(P_BQ, P_BV)
    b_scale = (B_HD ** -0.5) * LOG2E
    for blk in range(6):
        x = pb[:, blk * LANES:(blk + 1) * LANES]
        w = bqn_ref[...] if blk < 4 else bkn_ref[...]
        x = x * lax.rsqrt(_seg_mean_sq(x, ones128, 128) + EPS) * w
        x = _rope(x, cb, sb, B_HD // 4)
        if blk < 4:
            qb_ref[:, blk * LANES:(blk + 1) * LANES] = (x * b_scale).astype(BF16)
        else:
            kb_ref[:, (blk - 4) * LANES:(blk - 3) * LANES] = x.astype(BF16)
    vb_ref[...] = mm(P_BV, P_CQ).astype(BF16)

    pc = mm(P_CQ, P_GATE)
    cq = pc[:, 0:C_Q_LORA]
    cq = cq * lax.rsqrt(jnp.mean(cq * cq, axis=-1, keepdims=True) + EPS) * cql_ref[...]
    qfull = jnp.dot(cq.astype(BF16), wuq_ref[...], preferred_element_type=F32)
    ckv = pc[:, C_Q_LORA:C_Q_LORA + C_KV_LORA]
    ckv = ckv * lax.rsqrt(jnp.mean(ckv * ckv, axis=-1, keepdims=True) + EPS) * ckvl_ref[...]
    kvfull = jnp.dot(ckv.astype(BF16), wukv_ref[...], preferred_element_type=F32)
    kr = pc[:, C_Q_LORA + C_KV_LORA:]
    kr = kr * lax.rsqrt(_seg_mean_sq(kr, ones64, 64) + EPS) * cknr_ref[...]
    kr = _rope(kr, cc, sc, C_ROPE // 2).astype(BF16)
    c_scale = ((C_NOPE + C_ROPE) ** -0.5) * LOG2E
    for hd in range(4):
        base = hd * 2 * LANES
        qn = qfull[:, base:base + LANES]
        qn = qn * lax.rsqrt(_seg_mean_sq(qn, ones128, 128) + EPS) * cqnn_ref[...]
        qr = qfull[:, base + LANES:base + 2 * LANES]
        qr = qr * lax.rsqrt(_seg_mean_sq(qr, ones64, 64) + EPS) * cqnr_ref[...]
        qr = _rope(qr, cc, sc, C_ROPE // 2)
        qc_ref[:, base:base + LANES] = (qn * c_scale).astype(BF16)
        qc_ref[:, base + LANES:base + 2 * LANES] = (qr * c_scale).astype(BF16)
        kn = kvfull[:, base:base + LANES]
        kn = kn * lax.rsqrt(_seg_mean_sq(kn, ones128, 128) + EPS) * cknn_ref[...]
        kc_ref[:, base:base + LANES] = kn.astype(BF16)
        kc_ref[:, base + LANES:base + 2 * LANES] = kr
        vc_ref[:, hd * LANES:(hd + 1) * LANES] = kvfull[:, base + LANES:base + 2 * LANES].astype(BF16)

    for blk in range(4):
        g = mm(P_GATE + blk * BRANCH_W, P_GATE + (blk + 1) * BRANCH_W)
        sg_ref[:, blk * BRANCH_W:(blk + 1) * BRANCH_W] = jax.nn.silu(g).astype(BF16)
    du_ref[...] = mm(P_DU, P_END)


def _attn_kernel(q_ref, k_ref, v_ref, o_ref, m_sc, l_sc, acc_sc, *,
                 mode, tq, ck, n_main, tail_valid):
    if mode == "A":
        q = q_ref[...]
        lane = lax.broadcasted_iota(jnp.int32, q.shape, 1)
        zero = jnp.zeros_like(q)
        qs = [jnp.where(lane < A_QK, q, zero), jnp.where(lane >= A_QK, q, zero)]
    elif mode == "B":
        qs = [q_ref[:, 0:LANES], q_ref[:, LANES:2 * LANES]]
    else:
        qs = [q_ref[...]]
    n_grp = len(qs)

    m_sc[...] = jnp.full(m_sc.shape, NEG_BIG, F32)
    l_sc[...] = jnp.zeros(l_sc.shape, F32)
    acc_sc[...] = jnp.zeros(acc_sc.shape, F32)

    def step(k, v, width, valid):
        for g in range(n_grp):
            s = lax.dot_general(qs[g], k, (((1,), (1,)), ((), ())),
                                preferred_element_type=F32)
            if valid is not None:
                col = lax.broadcasted_iota(jnp.int32, s.shape, 1)
                s = jnp.where(col < valid, s, NEG_BIG)
            m_prev = m_sc[g]
            m_new = jnp.maximum(m_prev, jnp.max(s, axis=1, keepdims=True))
            alpha = jnp.exp2(m_prev - m_new)
            p = jnp.exp2(s - pltpu.repeat(m_new, width // LANES, axis=1))
            l_part = p[:, 0:LANES]
            for t in range(1, width // LANES):
                l_part = l_part + p[:, t * LANES:(t + 1) * LANES]
            l_sc[g] = alpha * l_sc[g] + l_part
            acc_sc[g] = alpha * acc_sc[g] + jnp.dot(p.astype(BF16), v,
                                                    preferred_element_type=F32)
            m_sc[g] = m_new

    def body(j, carry):
        start = pl.multiple_of(j * ck, ck)
        step(k_ref[pl.ds(start, ck), :], v_ref[pl.ds(start, ck), :], ck, None)
        return carry

    lax.fori_loop(0, n_main, body, 0)
    t0 = n_main * ck
    step(k_ref[t0:t0 + LANES, :], v_ref[t0:t0 + LANES, :], LANES, tail_valid)

    for g in range(n_grp):
        l = jnp.sum(l_sc[g], axis=1, keepdims=True)
        o_ref[g] = acc_sc[g] / l


def _final_kernel(h_ref, oa_ref, ob_ref, oc_ref, du_ref, dup_ref, dun_ref, sg_ref,
                  wout_ref, lam_ref, subln_ref, wg_ref, dsc_ref, out_ref, ext_sc, *,
                  lambda_init, seq_len, tl):
    i = pl.program_id(0)
    ones128 = _seg_ones(128)

    lp = lam_ref[...]
    lam = (jnp.exp(jnp.sum(lp[0:1, :] * lp[1:2, :], axis=-1, keepdims=True))
           - jnp.exp(jnp.sum(lp[2:3, :] * lp[3:4, :], axis=-1, keepdims=True)) + lambda_init)

    ys = []
    for hd in range(4):
        d = oa_ref[hd, 0] - lam * oa_ref[hd, 1]
        d = d * lax.rsqrt(_seg_mean_sq(d, ones128, 128) + EPS) * subln_ref[...]
        d = d * (1.0 - lambda_init)
        ys.append(d * sg_ref[:, hd * LANES:(hd + 1) * LANES].astype(F32))
    for g in range(2):
        for r in range(2):
            c = 4 + 2 * g + r
            ys.append(ob_ref[g, r] * sg_ref[:, c * LANES:(c + 1) * LANES].astype(F32))
    for hd in range(4):
        c = 8 + hd
        ys.append(oc_ref[hd, 0] * sg_ref[:, c * LANES:(c + 1) * LANES].astype(F32))

    row0 = i * tl
    def masked(x, first_row):
        r = first_row + lax.broadcasted_iota(jnp.int32, x.shape, 0)
        return jnp.where((r >= 0) & (r < seq_len), x, 0.0)
    ext_sc[0:POOL_HALO, :] = masked(dup_ref[...], row0 - POOL_HALO)
    ext_sc[POOL_HALO:POOL_HALO + tl, :] = masked(du_ref[...], row0)
    ext_sc[POOL_HALO + tl:2 * POOL_HALO + tl, :] = masked(dun_ref[...], row0 + tl)
    t = row0 + lax.broadcasted_iota(jnp.int32, (tl, LANES), 0)
    for g, w in enumerate(POOL_WINDOWS):
        cols = slice(g * LANES, (g + 1) * LANES)
        acc = ext_sc[POOL_HALO - w // 2:POOL_HALO - w // 2 + tl, cols]
        for j in range(-w // 2 + 1, w // 2):
            acc = acc + ext_sc[POOL_HALO + j:POOL_HALO + j + tl, cols]
        cnt = jnp.minimum(t + w // 2, seq_len) - jnp.maximum(t - w // 2, 0)
        cnt = jnp.maximum(cnt, 1).astype(F32)
        pooled = acc / cnt - ext_sc[POOL_HALO:POOL_HALO + tl, cols]
        mixed = jnp.dot(pooled.astype(BF16), wg_ref[g], preferred_element_type=F32)
        c = 12 + g
        ys.append(mixed * dsc_ref[:, cols] * sg_ref[:, c * LANES:(c + 1) * LANES].astype(F32))

    y = jnp.concatenate([v.astype(BF16) for v in ys], axis=1)
    out_ref[...] = h_ref[...] + jnp.dot(y, wout_ref[...], preferred_element_type=F32)


def _rope_table(pos, dim, theta):
    inv = theta ** (-jnp.arange(0, dim, 2, dtype=F32) / dim)
    ang = pos.astype(F32)[:, None] * inv[None, :]
    return jnp.cos(ang), jnp.sin(ang)


def _rope_lane_tables(n_tok, lp):
    l = N_META + n_tok
    rows = n_tok // GRID_W
    pos = jnp.arange(l, dtype=jnp.int32)
    cos_a, sin_a = _rope_table(pos, A_ROT, ROPE_THETA)
    cos_m, sin_m = _rope_table(pos, C_ROPE, ROPE_THETA)
    row_pos = jnp.concatenate([jnp.full((N_META,), -1, jnp.int32),
                               jnp.repeat(jnp.arange(rows, dtype=jnp.int32), GRID_W)])
    col_pos = jnp.concatenate([jnp.arange(N_META, dtype=jnp.int32),
                               jnp.tile(jnp.arange(GRID_W, dtype=jnp.int32), rows)])
    cos_r, sin_r = _rope_table(row_pos, B_HD // 2, AXIAL_THETA)
    cos_c, sin_c = _rope_table(col_pos, B_HD // 2, AXIAL_THETA)
    ones = lambda n: jnp.ones((l, n), F32)
    zeros = lambda n: jnp.zeros((l, n), F32)
    ca64 = jnp.concatenate([cos_a, cos_a, ones(A_QK - A_ROT)], axis=1)
    sa64 = jnp.concatenate([-sin_a, sin_a, zeros(A_QK - A_ROT)], axis=1)
    ca = jnp.concatenate([ca64, ca64], axis=1)
    sa = jnp.concatenate([sa64, sa64], axis=1)
    cb = jnp.concatenate([cos_r, cos_r, cos_c, cos_c], axis=1)
    sb = jnp.concatenate([-sin_r, sin_r, -sin_c, sin_c], axis=1)
    cc = jnp.concatenate([cos_m, cos_m, ones(LANES - C_ROPE)], axis=1)
    sc = jnp.concatenate([-sin_m, sin_m, zeros(LANES - C_ROPE)], axis=1)
    pad = lambda t: jnp.pad(t, ((0, lp - l), (0, 0)))
    return tuple(pad(t) for t in (ca, sa, cb, sb, cc, sc))


def _row_spec(tl, width):
    return pl.BlockSpec((tl, width), lambda i: (i, 0))


def _const_spec(shape):
    nd = len(shape)
    return pl.BlockSpec(shape, lambda i: (0,) * nd, pipeline_mode=pl.Buffered(1))


def _proj_call(h, lw, tables, lp, tl):
    n_tiles = lp // tl
    small = [lw["aqn"], lw["akn"], lw["bqn"], lw["bkn"], lw["cql"], lw["ckvl"],
             lw["cqnn"], lw["cqnr"], lw["cknn"], lw["cknr"]]
    in_specs = ([_row_spec(tl, D_MODEL), _const_spec((1, D_MODEL)),
                 _const_spec((D_MODEL, P_END)), _const_spec(lw["wuq"].shape),
                 _const_spec(lw["wukv"].shape)]
                + [_const_spec(a.shape) for a in small]
                + [_row_spec(tl, LANES)] * 6)
    out_widths = [512, 512, 512, 512, 256, 256, 1024, 1024, 512, 2048, 512]
    out_dtypes = [BF16] * 10 + [F32]
    out_shape = [jax.ShapeDtypeStruct((lp, w), dt) for w, dt in zip(out_widths, out_dtypes)]
    out_specs = [_row_spec(tl, w) for w in out_widths]
    return pl.pallas_call(
        _proj_kernel,
        grid=(n_tiles,),
        in_specs=in_specs,
        out_specs=out_specs,
        out_shape=out_shape,
        compiler_params=pltpu.CompilerParams(
            dimension_semantics=("arbitrary",), vmem_limit_bytes=VMEM_LIMIT),
        name="proj",
    )(h, lw["nw"], lw["win"], lw["wuq"], lw["wukv"], *small, *tables)


def _attn_call(q, k, v, mode, lp, n_tok, tq, ck):
    n_heads = {"A": 4, "B": 2, "C": 4}[mode]
    n_grp = {"A": 2, "B": 2, "C": 1}[mode]
    qw = {"A": LANES, "B": 2 * LANES, "C": 2 * LANES}[mode]
    kw = {"A": LANES, "B": LANES, "C": 2 * LANES}[mode]
    n_main = n_tok // ck
    tail_valid = N_META + n_tok - n_main * ck
    kern = functools.partial(_attn_kernel, mode=mode, tq=tq, ck=ck, n_main=n_main,
                             tail_valid=tail_valid)
    return pl.pallas_call(
        kern,
        grid=(n_heads, lp // tq),
        in_specs=[pl.BlockSpec((tq, qw), lambda hd, i: (i, hd)),
                  pl.BlockSpec((lp, kw), lambda hd, i: (0, hd)),
                  pl.BlockSpec((lp, LANES), lambda hd, i: (0, hd))],
        out_specs=pl.BlockSpec((None, n_grp, tq, LANES), lambda hd, i: (hd, 0, i, 0)),
        out_shape=jax.ShapeDtypeStruct((n_heads, n_grp, lp, LANES), F32),
        scratch_shapes=[pltpu.VMEM((n_grp, tq, LANES), F32)] * 3,
        compiler_params=pltpu.CompilerParams(
            dimension_semantics=("arbitrary", "arbitrary"), vmem_limit_bytes=VMEM_LIMIT),
        name="attn_" + mode,
    )(q, k, v)


def _final_call(h, oa, ob, oc, du, sg, lw, lambda_init, lp, n_tok, tl):
    n_tiles = lp // tl
    hb = tl // POOL_HALO
    n_hblk = lp // POOL_HALO
    kern = functools.partial(_final_kernel, lambda_init=lambda_init,
                             seq_len=N_META + n_tok, tl=tl)
    in_specs = [
        _row_spec(tl, D_MODEL),
        pl.BlockSpec((4, 2, tl, LANES), lambda i: (0, 0, i, 0)),
        pl.BlockSpec((2, 2, tl, LANES), lambda i: (0, 0, i, 0)),
        pl.BlockSpec((4, 1, tl, LANES), lambda i: (0, 0, i, 0)),
        _row_spec(tl, BRANCH_W),
        pl.BlockSpec((POOL_HALO, BRANCH_W), lambda i: (jnp.maximum(i * hb - 1, 0), 0)),
        pl.BlockSpec((POOL_HALO, BRANCH_W), lambda i: (jnp.minimum((i + 1) * hb, n_hblk - 1), 0)),
        _row_spec(tl, D_MODEL),
        _const_spec((D_MODEL, D_MODEL)),
        _const_spec((4, A_QK)),
        _const_spec((1, LANES)),
        _const_spec((4, LANES, LANES)),
        _const_spec((1, BRANCH_W)),
    ]
    return pl.pallas_call(
        kern,
        grid=(n_tiles,),
        in_specs=in_specs,
        out_specs=_row_spec(tl, D_MODEL),
        out_shape=jax.ShapeDtypeStruct((lp, D_MODEL), F32),
        scratch_shapes=[pltpu.VMEM((tl + 2 * POOL_HALO, BRANCH_W), F32)],
        compiler_params=pltpu.CompilerParams(
            dimension_semantics=("arbitrary",), vmem_limit_bytes=VMEM_LIMIT),
        name="final",
    )(h, oa, ob, oc, du, du, du, sg, lw["wout"], lw["lam"], lw["subln"], lw["wg"], lw["dsc"])


def _layer_weights(i, norm_w, w_in, w_out, a_q_norm, a_k_norm, a_lambda, a_subln, b_q_norm,
                   b_k_norm, c_q_lat_norm, c_kv_lat_norm, c_w_uq, c_w_ukv, c_q_norm, c_k_norm,
                   d_w_group, d_scale):
    w = w_in[i]
    col = lambda lo, n: w[:, lo:lo + n]
    win = jnp.concatenate([
        col(0, 512), col(512, 512), col(1024, 512),
        col(2048, 512), col(2560, 256), col(2816, 256),
        col(3584, 384), col(3968, 256), col(4224, 64), jnp.zeros((D_MODEL, 64), w.dtype),
        col(1536, 512), col(3072, 512), col(4288, 512), col(5312, 512),
        col(4800, 512),
    ], axis=1).astype(BF16)
    uq = c_w_uq[i].reshape(C_Q_LORA, 4, C_NOPE + C_ROPE)
    uq = jnp.pad(uq, ((0, 0), (0, 0), (0, 2 * LANES - C_NOPE - C_ROPE)))
    wuq = uq.reshape(C_Q_LORA, 4 * 2 * LANES).astype(BF16)
    row = lambda v: v.reshape(1, -1).astype(F32)
    pad_row = lambda v: jnp.pad(v.astype(F32), (0, LANES - v.shape[0])).reshape(1, LANES)
    return dict(
        nw=row(norm_w[i]), win=win, wuq=wuq, wukv=c_w_ukv[i].astype(BF16),
        aqn=row(jnp.tile(a_q_norm[i], 2)), akn=row(jnp.tile(a_k_norm[i], 2)),
        bqn=row(b_q_norm[i]), bkn=row(b_k_norm[i]),
        cql=row(c_q_lat_norm[i]), ckvl=row(c_kv_lat_norm[i]),
        cqnn=row(c_q_norm[i][:C_NOPE]), cqnr=pad_row(c_q_norm[i][C_NOPE:]),
        cknn=row(c_k_norm[i][:C_NOPE]), cknr=pad_row(c_k_norm[i][C_NOPE:]),
        wout=w_out[i].astype(BF16), lam=a_lambda[i].astype(F32), subln=row(a_subln[i]),
        wg=d_w_group[i].astype(BF16), dsc=row(d_scale[i]),
    )


def kernel(x, meta_tokens, norm_w, w_in, w_out, a_q_norm, a_k_norm, a_lambda, a_subln, b_q_norm,
           b_k_norm, c_q_lat_norm, c_kv_lat_norm, c_w_uq, c_w_ukv, c_q_norm, c_k_norm,
           d_w_group, d_scale):
    b, n_tok, d = x.shape
    assert b == 1 and d == D_MODEL
    lp = n_tok + SEQ_PAD
    tl = ROW_TILE
    ck = KV_CHUNK
    assert lp % tl == 0 and n_tok % ck == 0 and n_tok % GRID_W == 0
    depth = norm_w.shape[0]

    h = jnp.concatenate([meta_tokens.astype(x.dtype), x[0],
                         jnp.zeros((lp - N_META - n_tok, d), x.dtype)], axis=0)
    tables = _rope_lane_tables(n_tok, lp)

    for i in range(depth):
        lambda_init = 0.8 - 0.6 * math.exp(-0.3 * i)
        lw = _layer_weights(i, norm_w, w_in, w_out, a_q_norm, a_k_norm, a_lambda, a_subln,
                            b_q_norm, b_k_norm, c_q_lat_norm, c_kv_lat_norm, c_w_uq, c_w_ukv,
                            c_q_norm, c_k_norm, d_w_group, d_scale)
        qa, ka, va, qb, kb, vb, qc, kc, vc, sg, du = _proj_call(h, lw, tables, lp, tl)
        oa = _attn_call(qa, ka, va, "A", lp, n_tok, tl, ck)
        ob = _attn_call(qb, kb, vb, "B", lp, n_tok, tl, ck)
        oc = _attn_call(qc, kc, vc, "C", lp, n_tok, tl, ck)
        h = _final_call(h, oa, ob, oc, du, sg, lw, lambda_init, lp, n_tok, tl)
    return h[N_META:N_META + n_tok][None]
```

```python
import functools
import math

import jax
import jax.numpy as jnp
from jax import lax
from jax.experimental import pallas as pl
from jax.experimental.pallas import tpu as pltpu

F32 = jnp.float32
BF16 = jnp.bfloat16

D_MODEL = 2048
N_META = 16
GRID_W = 64
EPS = 1e-6
ROPE_THETA = 500000.0
AXIAL_THETA = 10000.0
BRANCH_W = 512
LANES = 128
SEQ_PAD = 128
ROW_TILE = 384
KV_CHUNK = 1024
A_QK = 64
A_ROT = 16
B_HD = 128
C_NOPE = 128
C_ROPE = 64
C_Q_LORA = 384
C_KV_LORA = 256
POOL_WINDOWS = (2, 4, 8, 16)
POOL_HALO = 8
LOG2E = 1.4426950408889634
NEG_BIG = -1e30
VMEM_LIMIT = 56 * 1024 * 1024

P_AQ, P_AK, P_AV = 0, 512, 1024
P_BQ, P_BK, P_BV = 1536, 2048, 2304
P_CQ, P_CKV, P_CKR = 2560, 2944, 3200
P_GATE, P_DU, P_END = 3328, 5376, 5888


def _seg_ones(seg):
    r = lax.broadcasted_iota(jnp.int32, (LANES, LANES), 0) // seg
    c = lax.broadcasted_iota(jnp.int32, (LANES, LANES), 1) // seg
    return jnp.where(r == c, 1.0, 0.0).astype(BF16)


def _seg_mean_sq(x, ones_mat, seg):
    xx = x * x
    hi = xx.astype(BF16)
    lo = (xx - hi.astype(F32)).astype(BF16)
    s = (jnp.dot(hi, ones_mat, preferred_element_type=F32)
         + jnp.dot(lo, ones_mat, preferred_element_type=F32))
    return s * (1.0 / seg)


def _rope(x, c, s, half):
    lane = lax.broadcasted_iota(jnp.int32, x.shape, 1)
    fwd = pltpu.roll(x, LANES - half, axis=1)
    bwd = pltpu.roll(x, half, axis=1)
    partner = jnp.where((lane % (2 * half)) < half, fwd, bwd)
    return x * c + partner * s


def _proj_kernel(h_ref, nw_ref, win_ref, wuq_ref, wukv_ref,
                 aqn_ref, akn_ref, bqn_ref, bkn_ref, cql_ref, ckvl_ref,
                 cqnn_ref, cqnr_ref, cknn_ref, cknr_ref,
                 ca_ref, sa_ref, cb_ref, sb_ref, cc_ref, sc_ref,
                 qa_ref, ka_ref, va_ref, qb_ref, kb_ref, vb_ref,
                 qc_ref, kc_ref, vc_ref, sg_ref, du_ref):
    h = h_ref[...]
    ms = jnp.mean(h * h, axis=-1, keepdims=True)
    hb = (h * lax.rsqrt(ms + EPS) * nw_ref[...]).astype(BF16)

    def mm(lo, hi):
        return jnp.dot(hb, win_ref[:, lo:hi], preferred_element_type=F32)

    ones64 = _seg_ones(64)
    ones128 = _seg_ones(128)
    ca, sa = ca_ref[...], sa_ref[...]
    cb, sb = cb_ref[...], sb_ref[...]
    cc, sc = cc_ref[...], sc_ref[...]

    pa = mm(P_AQ, P_AV)
    a_scale = (A_QK ** -0.5) * LOG2E
    for blk in range(8):
        x = pa[:, blk * LANES:(blk + 1) * LANES]
        w = aqn_ref[...] if blk < 4 else akn_ref[...]
        x = x * lax.rsqrt(_seg_mean_sq(x, ones64, 64) + EPS) * w
        x = _rope(x, ca, sa, A_ROT // 2)
        if blk < 4:
            qa_ref[:, blk * LANES:(blk + 1) * LANES] = (x * a_scale).astype(BF16)
        else:
            ka_ref[:, (blk - 4) * LANES:(blk - 3) * LANES] = x.astype(BF16)
    va_ref[...] = mm(P_AV, P_BQ).astype(BF16)

    pb = mm(P_BQ, P_BV)
    b_scale = (B_HD ** -0.5) * LOG2E
    for blk in range(6):
        x = pb[:, blk * LANES:(blk + 1) * LANES]
        w = bqn_ref[...] if blk < 4 else bkn_ref[...]
        x = x * lax.rsqrt(_seg_mean_sq(x, ones128, 128) + EPS) * w
        x = _rope(x, cb, sb, B_HD // 4)
        if blk < 4:
            qb_ref[:, blk * LANES:(blk + 1) * LANES] = (x * b_scale).astype(BF16)
        else:
            kb_ref[:, (blk - 4) * LANES:(blk - 3) * LANES] = x.astype(BF16)
    vb_ref[...] = mm(P_BV, P_CQ).astype(BF16)

    pc = mm(P_CQ, P_GATE)
    cq = pc[:, 0:C_Q_LORA]
    cq = cq * lax.rsqrt(jnp.mean(cq * cq, axis=-1, keepdims=True) + EPS) * cql_ref[...]
    qfull = jnp.dot(cq.astype(BF16), wuq_ref[...], preferred_element_type=F32)
    ckv = pc[:, C_Q_LORA:C_Q_LORA + C_KV_LORA]
    ckv = ckv * lax.rsqrt(jnp.mean(ckv * ckv, axis=-1, keepdims=True) + EPS) * ckvl_ref[...]
    kvfull = jnp.dot(ckv.astype(BF16), wukv_ref[...], preferred_element_type=F32)
    kr = pc[:, C_Q_LORA + C_KV_LORA:]
    kr = kr * lax.rsqrt(_seg_mean_sq(kr, ones64, 64) + EPS) * cknr_ref[...]
    kr = _rope(kr, cc, sc, C_ROPE // 2).astype(BF16)
    c_scale = ((C_NOPE + C_ROPE) ** -0.5) * LOG2E
    for hd in range(4):
        base = hd * 2 * LANES
        qn = qfull[:, base:base + LANES]
        qn = qn * lax.rsqrt(_seg_mean_sq(qn, ones128, 128) + EPS) * cqnn_ref[...]
        qr = qfull[:, base + LANES:base + 2 * LANES]
        qr = qr * lax.rsqrt(_seg_mean_sq(qr, ones64, 64) + EPS) * cqnr_ref[...]
        qr = _rope(qr, cc, sc, C_ROPE // 2)
        qc_ref[:, base:base + LANES] = (qn * c_scale).astype(BF16)
        qc_ref[:, base + LANES:base + 2 * LANES] = (qr * c_scale).astype(BF16)
        kn = kvfull[:, base:base + LANES]
        kn = kn * lax.rsqrt(_seg_mean_sq(kn, ones128, 128) + EPS) * cknn_ref[...]
        kc_ref[:, base:base + LANES] = kn.astype(BF16)
        kc_ref[:, base + LANES:base + 2 * LANES] = kr
        vc_ref[:, hd * LANES:(hd + 1) * LANES] = kvfull[:, base + LANES:base + 2 * LANES].astype(BF16)

    for blk in range(4):
        g = mm(P_GATE + blk * BRANCH_W, P_GATE + (blk + 1) * BRANCH_W)
        sg_ref[:, blk * BRANCH_W:(blk + 1) * BRANCH_W] = jax.nn.silu(g).astype(BF16)
    du_ref[...] = mm(P_DU, P_END)


def _attn_kernel(q_ref, k_ref, v_ref, o_ref, vx_sc, s_sc, mc_sc, m_sc, acc_sc, *,
                 mode, tq, ck, n_main, tail_valid):
    lp = v_ref.shape[0]

    @pl.when(pl.program_id(1) == 0)
    def _():
        def fill(c, carry):
            r = pl.multiple_of(c * tq, tq)
            vx_sc[pl.ds(r, tq), 0:LANES] = v_ref[pl.ds(r, tq), :]
            vx_sc[pl.ds(r, tq), LANES:2 * LANES] = jnp.ones((tq, LANES), BF16)
            return carry
        lax.fori_loop(0, lp // tq, fill, 0)

    if mode == "A":
        q = q_ref[...]
        lane = lax.broadcasted_iota(jnp.int32, q.shape, 1)
        zero = jnp.zeros_like(q)
        qs = jnp.concatenate([jnp.where(lane < A_QK, q, zero),
                              jnp.where(lane >= A_QK, q, zero)], axis=0)
    elif mode == "B":
        qs = jnp.concatenate([q_ref[:, 0:LANES], q_ref[:, LANES:2 * LANES]], axis=0)
    else:
        qs = q_ref[...]
    rows = qs.shape[0]
    n_grp = rows // tq

    m_sc[...] = jnp.full(m_sc.shape, NEG_BIG, F32)
    acc_sc[...] = jnp.zeros(acc_sc.shape, F32)

    def scores(k):
        return lax.dot_general(qs, k, (((1,), (1,)), ((), ())), preferred_element_type=F32)

    def row_max(s):
        return jnp.broadcast_to(jnp.max(s, axis=1, keepdims=True), (rows, LANES))

    def qk_stage(j, slot):
        s = scores(k_ref[pl.ds(pl.multiple_of(j * ck, ck), ck), :])
        s_sc[slot] = s
        mc_sc[slot] = row_max(s)

    def softmax_pv(s, m_cur, vx):
        width = s.shape[1]
        m_prev = m_sc[...]
        m_new = jnp.maximum(m_prev, m_cur)
        alpha = jnp.exp2(m_prev - m_new)
        p = jnp.exp2(s - pltpu.repeat(m_new, width // LANES, axis=1)).astype(BF16)
        pv = jnp.dot(p, vx, preferred_element_type=F32)
        acc_sc[...] = pltpu.repeat(alpha, 2, axis=1) * acc_sc[...] + pv
        m_sc[...] = m_new

    def pv_stage(j, slot):
        vx = vx_sc[pl.ds(pl.multiple_of(j * ck, ck), ck), :]
        softmax_pv(s_sc[slot], mc_sc[slot], vx)

    qk_stage(0, 0)

    def body(i, carry):
        j = 2 * i
        qk_stage(j + 1, 1)
        pv_stage(j, 0)
        qk_stage(j + 2, 0)
        pv_stage(j + 1, 1)
        return carry

    lax.fori_loop(0, n_main // 2 - 1, body, 0)
    qk_stage(n_main - 1, 1)
    pv_stage(n_main - 2, 0)
    t0 = n_main * ck
    s_tail = scores(k_ref[t0:t0 + LANES, :])
    col = lax.broadcasted_iota(jnp.int32, s_tail.shape, 1)
    s_tail = jnp.where(col < tail_valid, s_tail, NEG_BIG)
    pv_stage(n_main - 1, 1)
    softmax_pv(s_tail, row_max(s_tail), vx_sc[t0:t0 + LANES, :])

    o = acc_sc[:, 0:LANES] / acc_sc[:, LANES:2 * LANES]
    for g in range(n_grp):
        o_ref[g] = o[g * tq:(g + 1) * tq]


def _final_kernel(h_ref, oa_ref, ob_ref, oc_ref, du_ref, dup_ref, dun_ref, sg_ref,
                  wout_ref, lam_ref, subln_ref, wg_ref, dsc_ref, out_ref, ext_sc, *,
                  lambda_init, seq_len, tl):
    i = pl.program_id(0)
    ones128 = _seg_ones(128)

    lp = lam_ref[...]
    lam = (jnp.exp(jnp.sum(lp[0:1, :] * lp[1:2, :], axis=-1, keepdims=True))
           - jnp.exp(jnp.sum(lp[2:3, :] * lp[3:4, :], axis=-1, keepdims=True)) + lambda_init)

    ys = []
    for hd in range(4):
        d = oa_ref[hd, 0] - lam * oa_ref[hd, 1]
        d = d * lax.rsqrt(_seg_mean_sq(d, ones128, 128) + EPS) * subln_ref[...]
        d = d * (1.0 - lambda_init)
        ys.append(d * sg_ref[:, hd * LANES:(hd + 1) * LANES].astype(F32))
    for g in range(2):
        for r in range(2):
            c = 4 + 2 * g + r
            ys.append(ob_ref[g, r] * sg_ref[:, c * LANES:(c + 1) * LANES].astype(F32))
    for hd in range(4):
        c = 8 + hd
        ys.append(oc_ref[hd, 0] * sg_ref[:, c * LANES:(c + 1) * LANES].astype(F32))

    row0 = i * tl
    def masked(x, first_row):
        r = first_row + lax.broadcasted_iota(jnp.int32, x.shape, 0)
        return jnp.where((r >= 0) & (r < seq_len), x, 0.0)
    ext_sc[0:POOL_HALO, :] = masked(dup_ref[...], row0 - POOL_HALO)
    ext_sc[POOL_HALO:POOL_HALO + tl, :] = masked(du_ref[...], row0)
    ext_sc[POOL_HALO + tl:2 * POOL_HALO + tl, :] = masked(dun_ref[...], row0 + tl)
    t = row0 + lax.broadcasted_iota(jnp.int32, (tl, LANES), 0)
    for g, w in enumerate(POOL_WINDOWS):
        cols = slice(g * LANES, (g + 1) * LANES)
        acc = ext_sc[POOL_HALO - w // 2:POOL_HALO - w // 2 + tl, cols]
        for j in range(-w // 2 + 1, w // 2):
            acc = acc + ext_sc[POOL_HALO + j:POOL_HALO + j + tl, cols]
        cnt = jnp.minimum(t + w // 2, seq_len) - jnp.maximum(t - w // 2, 0)
        cnt = jnp.maximum(cnt, 1).astype(F32)
        pooled = acc / cnt - ext_sc[POOL_HALO:POOL_HALO + tl, cols]
        mixed = jnp.dot(pooled.astype(BF16), wg_ref[g], preferred_element_type=F32)
        c = 12 + g
        ys.append(mixed * dsc_ref[:, cols] * sg_ref[:, c * LANES:(c + 1) * LANES].astype(F32))

    y = jnp.concatenate([v.astype(BF16) for v in ys], axis=1)
    out_ref[...] = h_ref[...] + jnp.dot(y, wout_ref[...], preferred_element_type=F32)


def _rope_table(pos, dim, theta):
    inv = theta ** (-jnp.arange(0, dim, 2, dtype=F32) / dim)
    ang = pos.astype(F32)[:, None] * inv[None, :]
    return jnp.cos(ang), jnp.sin(ang)


def _rope_lane_tables(n_tok, lp):
    l = N_META + n_tok
    rows = n_tok // GRID_W
    pos = jnp.arange(l, dtype=jnp.int32)
    cos_a, sin_a = _rope_table(pos, A_ROT, ROPE_THETA)
    cos_m, sin_m = _rope_table(pos, C_ROPE, ROPE_THETA)
    row_pos = jnp.concatenate([jnp.full((N_META,), -1, jnp.int32),
                               jnp.repeat(jnp.arange(rows, dtype=jnp.int32), GRID_W)])
    col_pos = jnp.concatenate([jnp.arange(N_META, dtype=jnp.int32),
                               jnp.tile(jnp.arange(GRID_W, dtype=jnp.int32), rows)])
    cos_r, sin_r = _rope_table(row_pos, B_HD // 2, AXIAL_THETA)
    cos_c, sin_c = _rope_table(col_pos, B_HD // 2, AXIAL_THETA)
    ones = lambda n: jnp.ones((l, n), F32)
    zeros = lambda n: jnp.zeros((l, n), F32)
    ca64 = jnp.concatenate([cos_a, cos_a, ones(A_QK - A_ROT)], axis=1)
    sa64 = jnp.concatenate([-sin_a, sin_a, zeros(A_QK - A_ROT)], axis=1)
    ca = jnp.concatenate([ca64, ca64], axis=1)
    sa = jnp.concatenate([sa64, sa64], axis=1)
    cb = jnp.concatenate([cos_r, cos_r, cos_c, cos_c], axis=1)
    sb = jnp.concatenate([-sin_r, sin_r, -sin_c, sin_c], axis=1)
    cc = jnp.concatenate([cos_m, cos_m, ones(LANES - C_ROPE)], axis=1)
    sc = jnp.concatenate([-sin_m, sin_m, zeros(LANES - C_ROPE)], axis=1)
    pad = lambda t: jnp.pad(t, ((0, lp - l), (0, 0)))
    return tuple(pad(t) for t in (ca, sa, cb, sb, cc, sc))


def _row_spec(tl, width):
    return pl.BlockSpec((tl, width), lambda i: (i, 0))


def _const_spec(shape):
    nd = len(shape)
    return pl.BlockSpec(shape, lambda i: (0,) * nd, pipeline_mode=pl.Buffered(1))


def _proj_call(h, lw, tables, lp, tl):
    n_tiles = lp // tl
    small = [lw["aqn"], lw["akn"], lw["bqn"], lw["bkn"], lw["cql"], lw["ckvl"],
             lw["cqnn"], lw["cqnr"], lw["cknn"], lw["cknr"]]
    in_specs = ([_row_spec(tl, D_MODEL), _const_spec((1, D_MODEL)),
                 _const_spec((D_MODEL, P_END)), _const_spec(lw["wuq"].shape),
                 _const_spec(lw["wukv"].shape)]
                + [_const_spec(a.shape) for a in small]
                + [_row_spec(tl, LANES)] * 6)
    out_widths = [512, 512, 512, 512, 256, 256, 1024, 1024, 512, 2048, 512]
    out_dtypes = [BF16] * 10 + [F32]
    out_shape = [jax.ShapeDtypeStruct((lp, w), dt) for w, dt in zip(out_widths, out_dtypes)]
    out_specs = [_row_spec(tl, w) for w in out_widths]
    return pl.pallas_call(
        _proj_kernel,
        grid=(n_tiles,),
        in_specs=in_specs,
        out_specs=out_specs,
        out_shape=out_shape,
        compiler_params=pltpu.CompilerParams(
            dimension_semantics=("arbitrary",), vmem_limit_bytes=VMEM_LIMIT),
        name="proj",
    )(h, lw["nw"], lw["win"], lw["wuq"], lw["wukv"], *small, *tables)


def _attn_call(q, k, v, mode, lp, n_tok, tq, ck):
    n_heads = {"A": 4, "B": 2, "C": 4}[mode]
    n_grp = {"A": 2, "B": 2, "C": 1}[mode]
    qw = {"A": LANES, "B": 2 * LANES, "C": 2 * LANES}[mode]
    kw = {"A": LANES, "B": LANES, "C": 2 * LANES}[mode]
    n_main = n_tok // ck
    tail_valid = N_META + n_tok - n_main * ck
    kern = functools.partial(_attn_kernel, mode=mode, tq=tq, ck=ck, n_main=n_main,
                             tail_valid=tail_valid)
    return pl.pallas_call(
        kern,
        grid=(n_heads, lp // tq),
        in_specs=[pl.BlockSpec((tq, qw), lambda hd, i: (i, hd)),
                  pl.BlockSpec((lp, kw), lambda hd, i: (0, hd)),
                  pl.BlockSpec((lp, LANES), lambda hd, i: (0, hd))],
        out_specs=pl.BlockSpec((None, n_grp, tq, LANES), lambda hd, i: (hd, 0, i, 0)),
        out_shape=jax.ShapeDtypeStruct((n_heads, n_grp, lp, LANES), F32),
        scratch_shapes=[pltpu.VMEM((lp, 2 * LANES), BF16),
                        pltpu.VMEM((2, n_grp * tq, ck), F32),
                        pltpu.VMEM((2, n_grp * tq, LANES), F32),
                        pltpu.VMEM((n_grp * tq, LANES), F32),
                        pltpu.VMEM((n_grp * tq, 2 * LANES), F32)],
        compiler_params=pltpu.CompilerParams(
            dimension_semantics=("arbitrary", "arbitrary"), vmem_limit_bytes=VMEM_LIMIT),
        name="attn_" + mode,
    )(q, k, v)


def _final_call(h, oa, ob, oc, du, sg, lw, lambda_init, lp, n_tok, tl):
    n_tiles = lp // tl
    hb = tl // POOL_HALO
    n_hblk = lp // POOL_HALO
    kern = functools.partial(_final_kernel, lambda_init=lambda_init,
                             seq_len=N_META + n_tok, tl=tl)
    in_specs = [
        _row_spec(tl, D_MODEL),
        pl.BlockSpec((4, 2, tl, LANES), lambda i: (0, 0, i, 0)),
        pl.BlockSpec((2, 2, tl, LANES), lambda i: (0, 0, i, 0)),
        pl.BlockSpec((4, 1, tl, LANES), lambda i: (0, 0, i, 0)),
        _row_spec(tl, BRANCH_W),
        pl.BlockSpec((POOL_HALO, BRANCH_W), lambda i: (jnp.maximum(i * hb - 1, 0), 0)),
        pl.BlockSpec((POOL_HALO, BRANCH_W), lambda i: (jnp.minimum((i + 1) * hb, n_hblk - 1), 0)),
        _row_spec(tl, D_MODEL),
        _const_spec((D_MODEL, D_MODEL)),
        _const_spec((4, A_QK)),
        _const_spec((1, LANES)),
        _const_spec((4, LANES, LANES)),
        _const_spec((1, BRANCH_W)),
    ]
    return pl.pallas_call(
        kern,
        grid=(n_tiles,),
        in_specs=in_specs,
        out_specs=_row_spec(tl, D_MODEL),
        out_shape=jax.ShapeDtypeStruct((lp, D_MODEL), F32),
        scratch_shapes=[pltpu.VMEM((tl + 2 * POOL_HALO, BRANCH_W), F32)],
        compiler_params=pltpu.CompilerParams(
            dimension_semantics=("arbitrary",), vmem_limit_bytes=VMEM_LIMIT),
        name="final",
    )(h, oa, ob, oc, du, du, du, sg, lw["wout"], lw["lam"], lw["subln"], lw["wg"], lw["dsc"])


def _layer_weights(i, norm_w, w_in, w_out, a_q_norm, a_k_norm, a_lambda, a_subln, b_q_norm,
                   b_k_norm, c_q_lat_norm, c_kv_lat_norm, c_w_uq, c_w_ukv, c_q_norm, c_k_norm,
                   d_w_group, d_scale):
    w = w_in[i]
    col = lambda lo, n: w[:, lo:lo + n]
    win = jnp.concatenate([
        col(0, 512), col(512, 512), col(1024, 512),
        col(2048, 512), col(2560, 256), col(2816, 256),
        col(3584, 384), col(3968, 256), col(4224, 64), jnp.zeros((D_MODEL, 64), w.dtype),
        col(1536, 512), col(3072, 512), col(4288, 512), col(5312, 512),
        col(4800, 512),
    ], axis=1).astype(BF16)
    uq = c_w_uq[i].reshape(C_Q_LORA, 4, C_NOPE + C_ROPE)
    uq = jnp.pad(uq, ((0, 0), (0, 0), (0, 2 * LANES - C_NOPE - C_ROPE)))
    wuq = uq.reshape(C_Q_LORA, 4 * 2 * LANES).astype(BF16)
    row = lambda v: v.reshape(1, -1).astype(F32)
    pad_row = lambda v: jnp.pad(v.astype(F32), (0, LANES - v.shape[0])).reshape(1, LANES)
    return dict(
        nw=row(norm_w[i]), win=win, wuq=wuq, wukv=c_w_ukv[i].astype(BF16),
        aqn=row(jnp.tile(a_q_norm[i], 2)), akn=row(jnp.tile(a_k_norm[i], 2)),
        bqn=row(b_q_norm[i]), bkn=row(b_k_norm[i]),
        cql=row(c_q_lat_norm[i]), ckvl=row(c_kv_lat_norm[i]),
        cqnn=row(c_q_norm[i][:C_NOPE]), cqnr=pad_row(c_q_norm[i][C_NOPE:]),
        cknn=row(c_k_norm[i][:C_NOPE]), cknr=pad_row(c_k_norm[i][C_NOPE:]),
        wout=w_out[i].astype(BF16), lam=a_lambda[i].astype(F32), subln=row(a_subln[i]),
        wg=d_w_group[i].astype(BF16), dsc=row(d_scale[i]),
    )


def kernel(x, meta_tokens, norm_w, w_in, w_out, a_q_norm, a_k_norm, a_lambda, a_subln, b_q_norm,
           b_k_norm, c_q_lat_norm, c_kv_lat_norm, c_w_uq, c_w_ukv, c_q_norm, c_k_norm,
           d_w_group, d_scale):
    b, n_tok, d = x.shape
    assert b == 1 and d == D_MODEL
    lp = n_tok + SEQ_PAD
    tl = ROW_TILE
    ck = KV_CHUNK
    assert lp % tl == 0 and n_tok % (2 * ck) == 0 and n_tok % GRID_W == 0
    depth = norm_w.shape[0]

    h = jnp.concatenate([meta_tokens.astype(x.dtype), x[0],
                         jnp.zeros((lp - N_META - n_tok, d), x.dtype)], axis=0)
    tables = _rope_lane_tables(n_tok, lp)

    for i in range(depth):
        lambda_init = 0.8 - 0.6 * math.exp(-0.3 * i)
        lw = _layer_weights(i, norm_w, w_in, w_out, a_q_norm, a_k_norm, a_lambda, a_subln,
                            b_q_norm, b_k_norm, c_q_lat_norm, c_kv_lat_norm, c_w_uq, c_w_ukv,
                            c_q_norm, c_k_norm, d_w_group, d_scale)
        qa, ka, va, qb, kb, vb, qc, kc, vc, sg, du = _proj_call(h, lw, tables, lp, tl)
        oa = _attn_call(qa, ka, va, "A", lp, n_tok, tl, ck)
        ob = _attn_call(qb, kb, vb, "B", lp, n_tok, tl, ck)
        oc = _attn_call(qc, kc, vc, "C", lp, n_tok, tl, ck)
        h = _final_call(h, oa, ob, oc, du, sg, lw, lambda_init, lp, n_tok, tl)
    return h[N_META:N_META + n_tok][None]
```

```python
import functools
import math

import jax
import jax.numpy as jnp
from jax import lax
from jax.experimental import pallas as pl
from jax.experimental.pallas import tpu as pltpu

F32 = jnp.float32
BF16 = jnp.bfloat16

D_MODEL = 2048
N_META = 16
GRID_W = 64
EPS = 1e-6
ROPE_THETA = 500000.0
AXIAL_THETA = 10000.0
BRANCH_W = 512
LANES = 128
SEQ_PAD = 512
ROW_TILE = 384
ATTN_Q_TILE = 768
KV_CHUNK = 1024
A_QK = 64
A_ROT = 16
B_HD = 128
C_NOPE = 128
C_ROPE = 64
C_Q_LORA = 384
C_KV_LORA = 256
POOL_WINDOWS = (2, 4, 8, 16)
POOL_HALO = 8
LOG2E = 1.4426950408889634
NEG_BIG = -1e30
VMEM_LIMIT = 56 * 1024 * 1024

P_AQ, P_AK, P_AV = 0, 512, 1024
P_BQ, P_BK, P_BV = 1536, 2048, 2304
P_CQ, P_CKV, P_CKR = 2560, 2944, 3200
P_GATE, P_DU, P_END = 3328, 5376, 5888


def _seg_ones(seg):
    r = lax.broadcasted_iota(jnp.int32, (LANES, LANES), 0) // seg
    c = lax.broadcasted_iota(jnp.int32, (LANES, LANES), 1) // seg
    return jnp.where(r == c, 1.0, 0.0).astype(BF16)


def _seg_mean_sq(x, ones_mat, seg):
    xx = x * x
    hi = xx.astype(BF16)
    lo = (xx - hi.astype(F32)).astype(BF16)
    s = (jnp.dot(hi, ones_mat, preferred_element_type=F32)
         + jnp.dot(lo, ones_mat, preferred_element_type=F32))
    return s * (1.0 / seg)


def _rope(x, c, s, half):
    lane = lax.broadcasted_iota(jnp.int32, x.shape, 1)
    fwd = pltpu.roll(x, LANES - half, axis=1)
    bwd = pltpu.roll(x, half, axis=1)
    partner = jnp.where((lane % (2 * half)) < half, fwd, bwd)
    return x * c + partner * s


def _proj_kernel(h_ref, nw_ref, win_ref, wuq_ref, wukv_ref,
                 aqn_ref, akn_ref, bqn_ref, bkn_ref, cql_ref, ckvl_ref,
                 cqnn_ref, cqnr_ref, cknn_ref, cknr_ref,
                 ca_ref, sa_ref, cb_ref, sb_ref, cc_ref, sc_ref,
                 qa_ref, ka_ref, va_ref, qb_ref, kb_ref, vb_ref,
                 qc_ref, kc_ref, vc_ref, sg_ref, du_ref):
    h = h_ref[...]
    ms = jnp.mean(h * h, axis=-1, keepdims=True)
    hb = (h * lax.rsqrt(ms + EPS) * nw_ref[...]).astype(BF16)

    def mm(lo, hi):
        return jnp.dot(hb, win_ref[:, lo:hi], preferred_element_type=F32)

    ones64 = _seg_ones(64)
    ones128 = _seg_ones(128)
    ca, sa = ca_ref[...], sa_ref[...]
    cb, sb = cb_ref[...], sb_ref[...]
    cc, sc = cc_ref[...], sc_ref[...]

    pa = mm(P_AQ, P_AV)
    a_scale = (A_QK ** -0.5) * LOG2E
    for blk in range(8):
        x = pa[:, blk * LANES:(blk + 1) * LANES]
        w = aqn_ref[...] if blk < 4 else akn_ref[...]
        x = x * lax.rsqrt(_seg_mean_sq(x, ones64, 64) + EPS) * w
        x = _rope(x, ca, sa, A_ROT // 2)
        if blk < 4:
            qa_ref[:, blk * LANES:(blk + 1) * LANES] = (x * a_scale).astype(BF16)
        else:
            ka_ref[:, (blk - 4) * LANES:(blk - 3) * LANES] = x.astype(BF16)
    va_ref[...] = mm(P_AV, P_BQ).astype(BF16)

    pb = mm(P_BQ, P_BV)
    b_scale = (B_HD ** -0.5) * LOG2E
    for blk in range(6):
        x = pb[:, blk * LANES:(blk + 1) * LANES]
        w = bqn_ref[...] if blk < 4 else bkn_ref[...]
        x = x * lax.rsqrt(_seg_mean_sq(x, ones128, 128) + EPS) * w
        x = _rope(x, cb, sb, B_HD // 4)
        if blk < 4:
            qb_ref[:, blk * LANES:(blk + 1) * LANES] = (x * b_scale).astype(BF16)
        else:
            kb_ref[:, (blk - 4) * LANES:(blk - 3) * LANES] = x.astype(BF16)
    vb_ref[...] = mm(P_BV, P_CQ).astype(BF16)

    pc = mm(P_CQ, P_GATE)
    cq = pc[:, 0:C_Q_LORA]
    cq = cq * lax.rsqrt(jnp.mean(cq * cq, axis=-1, keepdims=True) + EPS) * cql_ref[...]
    qfull = jnp.dot(cq.astype(BF16), wuq_ref[...], preferred_element_type=F32)
    ckv = pc[:, C_Q_LORA:C_Q_LORA + C_KV_LORA]
    ckv = ckv * lax.rsqrt(jnp.mean(ckv * ckv, axis=-1, keepdims=True) + EPS) * ckvl_ref[...]
    kvfull = jnp.dot(ckv.astype(BF16), wukv_ref[...], preferred_element_type=F32)
    kr = pc[:, C_Q_LORA + C_KV_LORA:]
    kr = kr * lax.rsqrt(_seg_mean_sq(kr, ones64, 64) + EPS) * cknr_ref[...]
    kr = _rope(kr, cc, sc, C_ROPE // 2).astype(BF16)
    c_scale = ((C_NOPE + C_ROPE) ** -0.5) * LOG2E
    for hd in range(4):
        base = hd * 2 * LANES
        qn = qfull[:, base:base + LANES]
        qn = qn * lax.rsqrt(_seg_mean_sq(qn, ones128, 128) + EPS) * cqnn_ref[...]
        qr = qfull[:, base + LANES:base + 2 * LANES]
        qr = qr * lax.rsqrt(_seg_mean_sq(qr, ones64, 64) + EPS) * cqnr_ref[...]
        qr = _rope(qr, cc, sc, C_ROPE // 2)
        qc_ref[:, base:base + LANES] = (qn * c_scale).astype(BF16)
        qc_ref[:, base + LANES:base + 2 * LANES] = (qr * c_scale).astype(BF16)
        kn = kvfull[:, base:base + LANES]
        kn = kn * lax.rsqrt(_seg_mean_sq(kn, ones128, 128) + EPS) * cknn_ref[...]
        kc_ref[:, base:base + LANES] = kn.astype(BF16)
        kc_ref[:, base + LANES:base + 2 * LANES] = kr
        vc_ref[:, hd * LANES:(hd + 1) * LANES] = kvfull[:, base + LANES:base + 2 * LANES].astype(BF16)

    for blk in range(4):
        g = mm(P_GATE + blk * BRANCH_W, P_GATE + (blk + 1) * BRANCH_W)
        sg_ref[:, blk * BRANCH_W:(blk + 1) * BRANCH_W] = jax.nn.silu(g).astype(BF16)
    du_ref[...] = mm(P_DU, P_END)


def _lane_repeat(x, n):
    return jnp.concatenate([x] * n, axis=1)


def _attn_kernel(q_ref, k_ref, v_ref, o_ref, vx_sc, s_sc, mc_sc, m_sc, acc_sc, *,
                 mode, tq, ck, n_main, tail_valid):
    lp = v_ref.shape[0]

    @pl.when(pl.program_id(1) == 0)
    def _():
        def fill(c, carry):
            r = pl.multiple_of(c * tq, tq)
            vx_sc[pl.ds(r, tq), 0:LANES] = v_ref[pl.ds(r, tq), :]
            vx_sc[pl.ds(r, tq), LANES:2 * LANES] = jnp.ones((tq, LANES), BF16)
            return carry
        lax.fori_loop(0, lp // tq, fill, 0)

    if mode == "A":
        q = q_ref[...]
        lane = lax.broadcasted_iota(jnp.int32, q.shape, 1)
        zero = jnp.zeros_like(q)
        qs = jnp.concatenate([jnp.where(lane < A_QK, q, zero),
                              jnp.where(lane >= A_QK, q, zero)], axis=0)
    elif mode == "B":
        qs = jnp.concatenate([q_ref[:, 0:LANES], q_ref[:, LANES:2 * LANES]], axis=0)
    else:
        qs = q_ref[...]
    rows = qs.shape[0]
    n_grp = rows // tq

    m_sc[...] = jnp.full(m_sc.shape, NEG_BIG, F32)
    acc_sc[...] = jnp.zeros(acc_sc.shape, F32)

    def scores(k):
        return lax.dot_general(qs, k, (((1,), (1,)), ((), ())), preferred_element_type=F32)

    def row_max(s):
        return jnp.broadcast_to(jnp.max(s, axis=1, keepdims=True), (rows, LANES))

    def qk_stage(j, slot):
        s = scores(k_ref[pl.ds(pl.multiple_of(j * ck, ck), ck), :])
        s_sc[slot] = s
        mc_sc[slot] = row_max(s)

    def softmax_pv(s, m_cur, vx):
        width = s.shape[1]
        m_prev = m_sc[...]
        m_new = jnp.maximum(m_prev, m_cur)
        alpha = jnp.exp2(m_prev - m_new)
        p = jnp.exp2(s - _lane_repeat(m_new, width // LANES)).astype(BF16)
        pv = jnp.dot(p, vx, preferred_element_type=F32)
        acc_sc[...] = _lane_repeat(alpha, 2) * acc_sc[...] + pv
        m_sc[...] = m_new

    def pv_stage(j, slot):
        vx = vx_sc[pl.ds(pl.multiple_of(j * ck, ck), ck), :]
        softmax_pv(s_sc[slot], mc_sc[slot], vx)

    qk_stage(0, 0)

    def body(i, carry):
        j = 2 * i
        qk_stage(j + 1, 1)
        pv_stage(j, 0)
        qk_stage(j + 2, 0)
        pv_stage(j + 1, 1)
        return carry

    lax.fori_loop(0, n_main // 2 - 1, body, 0)
    qk_stage(n_main - 1, 1)
    pv_stage(n_main - 2, 0)
    t0 = n_main * ck
    s_tail = scores(k_ref[t0:t0 + LANES, :])
    col = lax.broadcasted_iota(jnp.int32, s_tail.shape, 1)
    s_tail = jnp.where(col < tail_valid, s_tail, NEG_BIG)
    pv_stage(n_main - 1, 1)
    softmax_pv(s_tail, row_max(s_tail), vx_sc[t0:t0 + LANES, :])

    o = acc_sc[:, 0:LANES] / acc_sc[:, LANES:2 * LANES]
    for g in range(n_grp):
        o_ref[g] = o[g * tq:(g + 1) * tq]


def _final_kernel(h_ref, oa_ref, ob_ref, oc_ref, du_ref, dup_ref, dun_ref, sg_ref,
                  wout_ref, lam_ref, subln_ref, wg_ref, dsc_ref, out_ref, ext_sc, *,
                  lambda_init, seq_len, tl):
    i = pl.program_id(0)
    ones128 = _seg_ones(128)

    lp = lam_ref[...]
    lam = (jnp.exp(jnp.sum(lp[0:1, :] * lp[1:2, :], axis=-1, keepdims=True))
           - jnp.exp(jnp.sum(lp[2:3, :] * lp[3:4, :], axis=-1, keepdims=True)) + lambda_init)

    ys = []
    for hd in range(4):
        d = oa_ref[hd, 0] - lam * oa_ref[hd, 1]
        d = d * lax.rsqrt(_seg_mean_sq(d, ones128, 128) + EPS) * subln_ref[...]
        d = d * (1.0 - lambda_init)
        ys.append(d * sg_ref[:, hd * LANES:(hd + 1) * LANES].astype(F32))
    for g in range(2):
        for r in range(2):
            c = 4 + 2 * g + r
            ys.append(ob_ref[g, r] * sg_ref[:, c * LANES:(c + 1) * LANES].astype(F32))
    for hd in range(4):
        c = 8 + hd
        ys.append(oc_ref[hd, 0] * sg_ref[:, c * LANES:(c + 1) * LANES].astype(F32))

    row0 = i * tl
    def masked(x, first_row):
        r = first_row + lax.broadcasted_iota(jnp.int32, x.shape, 0)
        return jnp.where((r >= 0) & (r < seq_len), x, 0.0)
    ext_sc[0:POOL_HALO, :] = masked(dup_ref[...], row0 - POOL_HALO)
    ext_sc[POOL_HALO:POOL_HALO + tl, :] = masked(du_ref[...], row0)
    ext_sc[POOL_HALO + tl:2 * POOL_HALO + tl, :] = masked(dun_ref[...], row0 + tl)
    t = row0 + lax.broadcasted_iota(jnp.int32, (tl, LANES), 0)
    for g, w in enumerate(POOL_WINDOWS):
        cols = slice(g * LANES, (g + 1) * LANES)
        acc = ext_sc[POOL_HALO - w // 2:POOL_HALO - w // 2 + tl, cols]
        for j in range(-w // 2 + 1, w // 2):
            acc = acc + ext_sc[POOL_HALO + j:POOL_HALO + j + tl, cols]
        cnt = jnp.minimum(t + w // 2, seq_len) - jnp.maximum(t - w // 2, 0)
        cnt = jnp.maximum(cnt, 1).astype(F32)
        pooled = acc / cnt - ext_sc[POOL_HALO:POOL_HALO + tl, cols]
        mixed = jnp.dot(pooled.astype(BF16), wg_ref[g], preferred_element_type=F32)
        c = 12 + g
        ys.append(mixed * dsc_ref[:, cols] * sg_ref[:, c * LANES:(c + 1) * LANES].astype(F32))

    y = jnp.concatenate([v.astype(BF16) for v in ys], axis=1)
    out_ref[...] = h_ref[...] + jnp.dot(y, wout_ref[...], preferred_element_type=F32)


def _rope_table(pos, dim, theta):
    inv = theta ** (-jnp.arange(0, dim, 2, dtype=F32) / dim)
    ang = pos.astype(F32)[:, None] * inv[None, :]
    return jnp.cos(ang), jnp.sin(ang)


def _rope_lane_tables(n_tok, lp):
    l = N_META + n_tok
    rows = n_tok // GRID_W
    pos = jnp.arange(l, dtype=jnp.int32)
    cos_a, sin_a = _rope_table(pos, A_ROT, ROPE_THETA)
    cos_m, sin_m = _rope_table(pos, C_ROPE, ROPE_THETA)
    row_pos = jnp.concatenate([jnp.full((N_META,), -1, jnp.int32),
                               jnp.repeat(jnp.arange(rows, dtype=jnp.int32), GRID_W)])
    col_pos = jnp.concatenate([jnp.arange(N_META, dtype=jnp.int32),
                               jnp.tile(jnp.arange(GRID_W, dtype=jnp.int32), rows)])
    cos_r, sin_r = _rope_table(row_pos, B_HD // 2, AXIAL_THETA)
    cos_c, sin_c = _rope_table(col_pos, B_HD // 2, AXIAL_THETA)
    ones = lambda n: jnp.ones((l, n), F32)
    zeros = lambda n: jnp.zeros((l, n), F32)
    ca64 = jnp.concatenate([cos_a, cos_a, ones(A_QK - A_ROT)], axis=1)
    sa64 = jnp.concatenate([-sin_a, sin_a, zeros(A_QK - A_ROT)], axis=1)
    ca = jnp.concatenate([ca64, ca64], axis=1)
    sa = jnp.concatenate([sa64, sa64], axis=1)
    cb = jnp.concatenate([cos_r, cos_r, cos_c, cos_c], axis=1)
    sb = jnp.concatenate([-sin_r, sin_r, -sin_c, sin_c], axis=1)
    cc = jnp.concatenate([cos_m, cos_m, ones(LANES - C_ROPE)], axis=1)
    sc = jnp.concatenate([-sin_m, sin_m, zeros(LANES - C_ROPE)], axis=1)
    pad = lambda t: jnp.pad(t, ((0, lp - l), (0, 0)))
    return tuple(pad(t) for t in (ca, sa, cb, sb, cc, sc))


def _row_spec(tl, width):
    return pl.BlockSpec((tl, width), lambda i: (i, 0))


def _const_spec(shape):
    nd = len(shape)
    return pl.BlockSpec(shape, lambda i: (0,) * nd, pipeline_mode=pl.Buffered(1))


def _proj_call(h, lw, tables, lp, tl):
    n_tiles = lp // tl
    small = [lw["aqn"], lw["akn"], lw["bqn"], lw["bkn"], lw["cql"], lw["ckvl"],
             lw["cqnn"], lw["cqnr"], lw["cknn"], lw["cknr"]]
    in_specs = ([_row_spec(tl, D_MODEL), _const_spec((1, D_MODEL)),
                 _const_spec((D_MODEL, P_END)), _const_spec(lw["wuq"].shape),
                 _const_spec(lw["wukv"].shape)]
                + [_const_spec(a.shape) for a in small]
                + [_row_spec(tl, LANES)] * 6)
    out_widths = [512, 512, 512, 512, 256, 256, 1024, 1024, 512, 2048, 512]
    out_dtypes = [BF16] * 10 + [F32]
    out_shape = [jax.ShapeDtypeStruct((lp, w), dt) for w, dt in zip(out_widths, out_dtypes)]
    out_specs = [_row_spec(tl, w) for w in out_widths]
    return pl.pallas_call(
        _proj_kernel,
        grid=(n_tiles,),
        in_specs=in_specs,
        out_specs=out_specs,
        out_shape=out_shape,
        compiler_params=pltpu.CompilerParams(
            dimension_semantics=("arbitrary",), vmem_limit_bytes=VMEM_LIMIT),
        name="proj",
    )(h, lw["nw"], lw["win"], lw["wuq"], lw["wukv"], *small, *tables)


def _attn_call(q, k, v, mode, lp, n_tok, tq, ck):
    n_heads = {"A": 4, "B": 2, "C": 4}[mode]
    n_grp = {"A": 2, "B": 2, "C": 1}[mode]
    qw = {"A": LANES, "B": 2 * LANES, "C": 2 * LANES}[mode]
    kw = {"A": LANES, "B": LANES, "C": 2 * LANES}[mode]
    n_main = n_tok // ck
    tail_valid = N_META + n_tok - n_main * ck
    kern = functools.partial(_attn_kernel, mode=mode, tq=tq, ck=ck, n_main=n_main,
                             tail_valid=tail_valid)
    return pl.pallas_call(
        kern,
        grid=(n_heads, lp // tq),
        in_specs=[pl.BlockSpec((tq, qw), lambda hd, i: (i, hd)),
                  pl.BlockSpec((lp, kw), lambda hd, i: (0, hd)),
                  pl.BlockSpec((lp, LANES), lambda hd, i: (0, hd))],
        out_specs=pl.BlockSpec((None, n_grp, tq, LANES), lambda hd, i: (hd, 0, i, 0)),
        out_shape=jax.ShapeDtypeStruct((n_heads, n_grp, lp, LANES), F32),
        scratch_shapes=[pltpu.VMEM((lp, 2 * LANES), BF16),
                        pltpu.VMEM((2, n_grp * tq, ck), F32),
                        pltpu.VMEM((2, n_grp * tq, LANES), F32),
                        pltpu.VMEM((n_grp * tq, LANES), F32),
                        pltpu.VMEM((n_grp * tq, 2 * LANES), F32)],
        compiler_params=pltpu.CompilerParams(
            dimension_semantics=("arbitrary", "arbitrary"), vmem_limit_bytes=VMEM_LIMIT),
        name="attn_" + mode,
    )(q, k, v)


def _final_call(h, oa, ob, oc, du, sg, lw, lambda_init, lp, n_tok, tl):
    n_tiles = lp // tl
    hb = tl // POOL_HALO
    n_hblk = lp // POOL_HALO
    kern = functools.partial(_final_kernel, lambda_init=lambda_init,
                             seq_len=N_META + n_tok, tl=tl)
    in_specs = [
        _row_spec(tl, D_MODEL),
        pl.BlockSpec((4, 2, tl, LANES), lambda i: (0, 0, i, 0)),
        pl.BlockSpec((2, 2, tl, LANES), lambda i: (0, 0, i, 0)),
        pl.BlockSpec((4, 1, tl, LANES), lambda i: (0, 0, i, 0)),
        _row_spec(tl, BRANCH_W),
        pl.BlockSpec((POOL_HALO, BRANCH_W), lambda i: (jnp.maximum(i * hb - 1, 0), 0)),
        pl.BlockSpec((POOL_HALO, BRANCH_W), lambda i: (jnp.minimum((i + 1) * hb, n_hblk - 1), 0)),
        _row_spec(tl, D_MODEL),
        _const_spec((D_MODEL, D_MODEL)),
        _const_spec((4, A_QK)),
        _const_spec((1, LANES)),
        _const_spec((4, LANES, LANES)),
        _const_spec((1, BRANCH_W)),
    ]
    return pl.pallas_call(
        kern,
        grid=(n_tiles,),
        in_specs=in_specs,
        out_specs=_row_spec(tl, D_MODEL),
        out_shape=jax.ShapeDtypeStruct((lp, D_MODEL), F32),
        scratch_shapes=[pltpu.VMEM((tl + 2 * POOL_HALO, BRANCH_W), F32)],
        compiler_params=pltpu.CompilerParams(
            dimension_semantics=("arbitrary",), vmem_limit_bytes=VMEM_LIMIT),
        name="final",
    )(h, oa, ob, oc, du, du, du, sg, lw["wout"], lw["lam"], lw["subln"], lw["wg"], lw["dsc"])


def _layer_weights(i, norm_w, w_in, w_out, a_q_norm, a_k_norm, a_lambda, a_subln, b_q_norm,
                   b_k_norm, c_q_lat_norm, c_kv_lat_norm, c_w_uq, c_w_ukv, c_q_norm, c_k_norm,
                   d_w_group, d_scale):
    w = w_in[i]
    col = lambda lo, n: w[:, lo:lo + n]
    win = jnp.concatenate([
        col(0, 512), col(512, 512), col(1024, 512),
        col(2048, 512), col(2560, 256), col(2816, 256),
        col(3584, 384), col(3968, 256), col(4224, 64), jnp.zeros((D_MODEL, 64), w.dtype),
        col(1536, 512), col(3072, 512), col(4288, 512), col(5312, 512),
        col(4800, 512),
    ], axis=1).astype(BF16)
    uq = c_w_uq[i].reshape(C_Q_LORA, 4, C_NOPE + C_ROPE)
    uq = jnp.pad(uq, ((0, 0), (0, 0), (0, 2 * LANES - C_NOPE - C_ROPE)))
    wuq = uq.reshape(C_Q_LORA, 4 * 2 * LANES).astype(BF16)
    row = lambda v: v.reshape(1, -1).astype(F32)
    pad_row = lambda v: jnp.pad(v.astype(F32), (0, LANES - v.shape[0])).reshape(1, LANES)
    return dict(
        nw=row(norm_w[i]), win=win, wuq=wuq, wukv=c_w_ukv[i].astype(BF16),
        aqn=row(jnp.tile(a_q_norm[i], 2)), akn=row(jnp.tile(a_k_norm[i], 2)),
        bqn=row(b_q_norm[i]), bkn=row(b_k_norm[i]),
        cql=row(c_q_lat_norm[i]), ckvl=row(c_kv_lat_norm[i]),
        cqnn=row(c_q_norm[i][:C_NOPE]), cqnr=pad_row(c_q_norm[i][C_NOPE:]),
        cknn=row(c_k_norm[i][:C_NOPE]), cknr=pad_row(c_k_norm[i][C_NOPE:]),
        wout=w_out[i].astype(BF16), lam=a_lambda[i].astype(F32), subln=row(a_subln[i]),
        wg=d_w_group[i].astype(BF16), dsc=row(d_scale[i]),
    )


def kernel(x, meta_tokens, norm_w, w_in, w_out, a_q_norm, a_k_norm, a_lambda, a_subln, b_q_norm,
           b_k_norm, c_q_lat_norm, c_kv_lat_norm, c_w_uq, c_w_ukv, c_q_norm, c_k_norm,
           d_w_group, d_scale):
    b, n_tok, d = x.shape
    assert b == 1 and d == D_MODEL
    lp = n_tok + SEQ_PAD
    tl = ROW_TILE
    tq = ATTN_Q_TILE
    ck = KV_CHUNK
    assert lp % tl == 0 and lp % tq == 0 and n_tok % (2 * ck) == 0 and n_tok % GRID_W == 0
    depth = norm_w.shape[0]

    h = jnp.concatenate([meta_tokens.astype(x.dtype), x[0],
                         jnp.zeros((lp - N_META - n_tok, d), x.dtype)], axis=0)
    tables = _rope_lane_tables(n_tok, lp)

    for i in range(depth):
        lambda_init = 0.8 - 0.6 * math.exp(-0.3 * i)
        lw = _layer_weights(i, norm_w, w_in, w_out, a_q_norm, a_k_norm, a_lambda, a_subln,
                            b_q_norm, b_k_norm, c_q_lat_norm, c_kv_lat_norm, c_w_uq, c_w_ukv,
                            c_q_norm, c_k_norm, d_w_group, d_scale)
        qa, ka, va, qb, kb, vb, qc, kc, vc, sg, du = _proj_call(h, lw, tables, lp, tl)
        oa = _attn_call(qa, ka, va, "A", lp, n_tok, tq, ck)
        ob = _attn_call(qb, kb, vb, "B", lp, n_tok, tq, ck)
        oc = _attn_call(qc, kc, vc, "C", lp, n_tok, tq, ck)
        h = _final_call(h, oa, ob, oc, du, sg, lw, lambda_init, lp, n_tok, tl)
    return h[N_META:N_META + n_tok][None]
```

```python
import functools
import math

import jax
import jax.numpy as jnp
from jax import lax
from jax.experimental import pallas as pl
from jax.experimental.pallas import tpu as pltpu

F32 = jnp.float32
BF16 = jnp.bfloat16

D_MODEL = 2048
N_META = 16
GRID_W = 64
EPS = 1e-6
ROPE_THETA = 500000.0
AXIAL_THETA = 10000.0
BRANCH_W = 512
LANES = 128
SEQ_PAD = 512
ROW_TILE = 384
ATTN_ROWS = 768
KV_CHUNK = 1024
A_QK = 64
A_ROT = 16
B_HD = 128
C_NOPE = 128
C_ROPE = 64
C_Q_LORA = 384
C_KV_LORA = 256
POOL_WINDOWS = (2, 4, 8, 16)
POOL_HALO = 8
LOG2E = 1.4426950408889634
NEG_BIG = -1e30
VMEM_LIMIT = 56 * 1024 * 1024

P_AQ, P_AK, P_AV = 0, 512, 1024
P_BQ, P_BK, P_BV = 1536, 2048, 2304
P_CQ, P_CKV, P_CKR = 2560, 2944, 3200
P_GATE, P_DU, P_END = 3328, 5376, 5888


def _mean_sq(x, n):
    return jnp.sum(x * x, axis=-1, keepdims=True) * (1.0 / n)


def _half_mean_sq(x):
    xx = x * x
    lower = lax.broadcasted_iota(jnp.int32, x.shape, 1) < (LANES // 2)
    lo = jnp.sum(jnp.where(lower, xx, 0.0), axis=-1, keepdims=True)
    hi = jnp.sum(jnp.where(lower, 0.0, xx), axis=-1, keepdims=True)
    return jnp.where(lower, lo, hi) * (2.0 / LANES)


def _rope(x, c, s, half):
    lane = lax.broadcasted_iota(jnp.int32, x.shape, 1)
    fwd = pltpu.roll(x, LANES - half, axis=1)
    bwd = pltpu.roll(x, half, axis=1)
    partner = jnp.where((lane % (2 * half)) < half, fwd, bwd)
    return x * c + partner * s


def _proj_kernel(h_ref, nw_ref, win_ref, wuq_ref, wukv_ref,
                 aqn_ref, akn_ref, bqn_ref, bkn_ref, cql_ref, ckvl_ref,
                 cqnn_ref, cqnr_ref, cknn_ref, cknr_ref,
                 ca_ref, sa_ref, cb_ref, sb_ref, cc_ref, sc_ref,
                 qa_ref, ka_ref, va_ref, qb_ref, kb_ref, vb_ref,
                 qc_ref, kc_ref, vc_ref, sg_ref, du_ref):
    h = h_ref[...]
    ms = jnp.mean(h * h, axis=-1, keepdims=True)
    hb = (h * lax.rsqrt(ms + EPS) * nw_ref[...]).astype(BF16)

    def mm(lo, hi):
        return jnp.dot(hb, win_ref[:, lo:hi], preferred_element_type=F32)

    ca, sa = ca_ref[...], sa_ref[...]
    cb, sb = cb_ref[...], sb_ref[...]
    cc, sc = cc_ref[...], sc_ref[...]

    pa = mm(P_AQ, P_AV)
    a_scale = (A_QK ** -0.5) * LOG2E
    for blk in range(8):
        x = pa[:, blk * LANES:(blk + 1) * LANES]
        w = aqn_ref[...] if blk < 4 else akn_ref[...]
        x = x * lax.rsqrt(_half_mean_sq(x) + EPS) * w
        x = _rope(x, ca, sa, A_ROT // 2)
        if blk < 4:
            qa_ref[:, blk * LANES:(blk + 1) * LANES] = (x * a_scale).astype(BF16)
        else:
            ka_ref[:, (blk - 4) * LANES:(blk - 3) * LANES] = x.astype(BF16)
    va_ref[...] = mm(P_AV, P_BQ).astype(BF16)

    pb = mm(P_BQ, P_BV)
    b_scale = (B_HD ** -0.5) * LOG2E
    for blk in range(6):
        x = pb[:, blk * LANES:(blk + 1) * LANES]
        w = bqn_ref[...] if blk < 4 else bkn_ref[...]
        x = x * lax.rsqrt(_mean_sq(x, B_HD) + EPS) * w
        x = _rope(x, cb, sb, B_HD // 4)
        if blk < 4:
            qb_ref[:, blk * LANES:(blk + 1) * LANES] = (x * b_scale).astype(BF16)
        else:
            kb_ref[:, (blk - 4) * LANES:(blk - 3) * LANES] = x.astype(BF16)
    vb_ref[...] = mm(P_BV, P_CQ).astype(BF16)

    pc = mm(P_CQ, P_GATE)
    cq = pc[:, 0:C_Q_LORA]
    cq = cq * lax.rsqrt(jnp.mean(cq * cq, axis=-1, keepdims=True) + EPS) * cql_ref[...]
    qfull = jnp.dot(cq.astype(BF16), wuq_ref[...], preferred_element_type=F32)
    ckv = pc[:, C_Q_LORA:C_Q_LORA + C_KV_LORA]
    ckv = ckv * lax.rsqrt(jnp.mean(ckv * ckv, axis=-1, keepdims=True) + EPS) * ckvl_ref[...]
    kvfull = jnp.dot(ckv.astype(BF16), wukv_ref[...], preferred_element_type=F32)
    kr = pc[:, C_Q_LORA + C_KV_LORA:]
    kr = kr * lax.rsqrt(_mean_sq(kr, C_ROPE) + EPS) * cknr_ref[...]
    kr = _rope(kr, cc, sc, C_ROPE // 2).astype(BF16)
    c_scale = ((C_NOPE + C_ROPE) ** -0.5) * LOG2E
    for hd in range(4):
        base = hd * 2 * LANES
        qn = qfull[:, base:base + LANES]
        qn = qn * lax.rsqrt(_mean_sq(qn, C_NOPE) + EPS) * cqnn_ref[...]
        qr = qfull[:, base + LANES:base + 2 * LANES]
        qr = qr * lax.rsqrt(_mean_sq(qr, C_ROPE) + EPS) * cqnr_ref[...]
        qr = _rope(qr, cc, sc, C_ROPE // 2)
        qc_ref[:, base:base + LANES] = (qn * c_scale).astype(BF16)
        qc_ref[:, base + LANES:base + 2 * LANES] = (qr * c_scale).astype(BF16)
        kn = kvfull[:, base:base + LANES]
        kn = kn * lax.rsqrt(_mean_sq(kn, C_NOPE) + EPS) * cknn_ref[...]
        kc_ref[:, base:base + LANES] = kn.astype(BF16)
        kc_ref[:, base + LANES:base + 2 * LANES] = kr
        vc_ref[:, hd * LANES:(hd + 1) * LANES] = kvfull[:, base + LANES:base + 2 * LANES].astype(BF16)

    for blk in range(4):
        g = mm(P_GATE + blk * BRANCH_W, P_GATE + (blk + 1) * BRANCH_W)
        sg_ref[:, blk * BRANCH_W:(blk + 1) * BRANCH_W] = jax.nn.silu(g).astype(BF16)
    du_ref[...] = mm(P_DU, P_END)


def _lane_repeat(x, n):
    return jnp.concatenate([x] * n, axis=1)


def _attn_kernel(q_ref, k_ref, v_ref, o_ref, vx_sc, s_sc, mc_sc, m_sc, acc_sc, *,
                 mode, tq, ck, n_main, tail_valid):
    lp = v_ref.shape[0]

    @pl.when(pl.program_id(1) == 0)
    def _():
        def fill(c, carry):
            r = pl.multiple_of(c * tq, tq)
            vx_sc[pl.ds(r, tq), 0:LANES] = v_ref[pl.ds(r, tq), :]
            vx_sc[pl.ds(r, tq), LANES:2 * LANES] = jnp.ones((tq, LANES), BF16)
            return carry
        lax.fori_loop(0, lp // tq, fill, 0)

    if mode == "A":
        q = q_ref[...]
        lane = lax.broadcasted_iota(jnp.int32, q.shape, 1)
        zero = jnp.zeros_like(q)
        qs = jnp.concatenate([jnp.where(lane < A_QK, q, zero),
                              jnp.where(lane >= A_QK, q, zero)], axis=0)
    elif mode == "B":
        qs = jnp.concatenate([q_ref[:, 0:LANES], q_ref[:, LANES:2 * LANES]], axis=0)
    else:
        qs = q_ref[...]
    rows = qs.shape[0]
    n_grp = rows // tq

    m_sc[...] = jnp.full(m_sc.shape, NEG_BIG, F32)
    acc_sc[...] = jnp.zeros(acc_sc.shape, F32)

    def scores(k):
        return lax.dot_general(qs, k, (((1,), (1,)), ((), ())), preferred_element_type=F32)

    def row_max(s):
        return jnp.broadcast_to(jnp.max(s, axis=1, keepdims=True), (rows, LANES))

    def qk_stage(j, slot):
        s = scores(k_ref[pl.ds(pl.multiple_of(j * ck, ck), ck), :])
        s_sc[slot] = s
        mc_sc[slot] = row_max(s)

    def softmax_pv(s, m_cur, vx):
        width = s.shape[1]
        m_prev = m_sc[...]
        m_new = jnp.maximum(m_prev, m_cur)
        alpha = jnp.exp2(m_prev - m_new)
        p = jnp.exp2(s - _lane_repeat(m_new, width // LANES)).astype(BF16)
        pv = jnp.dot(p, vx, preferred_element_type=F32)
        acc_sc[...] = _lane_repeat(alpha, 2) * acc_sc[...] + pv
        m_sc[...] = m_new

    def pv_stage(j, slot):
        vx = vx_sc[pl.ds(pl.multiple_of(j * ck, ck), ck), :]
        softmax_pv(s_sc[slot], mc_sc[slot], vx)

    qk_stage(0, 0)
    qk_stage(1, 1)

    def body(i, carry):
        c = 4 * i
        qk_stage(c + 2, 2)
        qk_stage(c + 3, 3)
        pv_stage(c, 0)
        pv_stage(c + 1, 1)
        qk_stage(c + 4, 0)
        qk_stage(c + 5, 1)
        pv_stage(c + 2, 2)
        pv_stage(c + 3, 3)
        return carry

    lax.fori_loop(0, n_main // 4 - 1, body, 0)
    c = n_main - 4
    qk_stage(c + 2, 2)
    qk_stage(c + 3, 3)
    pv_stage(c, 0)
    pv_stage(c + 1, 1)
    t0 = n_main * ck
    s_tail = scores(k_ref[t0:t0 + LANES, :])
    col = lax.broadcasted_iota(jnp.int32, s_tail.shape, 1)
    s_tail = jnp.where(col < tail_valid, s_tail, NEG_BIG)
    pv_stage(c + 2, 2)
    pv_stage(c + 3, 3)
    softmax_pv(s_tail, row_max(s_tail), vx_sc[t0:t0 + LANES, :])

    o = acc_sc[:, 0:LANES] / acc_sc[:, LANES:2 * LANES]
    for g in range(n_grp):
        o_ref[g] = o[g * tq:(g + 1) * tq]


def _final_kernel(h_ref, oa_ref, ob_ref, oc_ref, du_ref, dup_ref, dun_ref, sg_ref,
                  wout_ref, lam_ref, subln_ref, wg_ref, dsc_ref, out_ref, ext_sc, *,
                  lambda_init, seq_len, tl):
    i = pl.program_id(0)

    lp = lam_ref[...]
    lam = (jnp.exp(jnp.sum(lp[0:1, :] * lp[1:2, :], axis=-1, keepdims=True))
           - jnp.exp(jnp.sum(lp[2:3, :] * lp[3:4, :], axis=-1, keepdims=True)) + lambda_init)

    ys = []
    for hd in range(4):
        d = oa_ref[hd, 0] - lam * oa_ref[hd, 1]
        d = d * lax.rsqrt(_mean_sq(d, LANES) + EPS) * subln_ref[...]
        d = d * (1.0 - lambda_init)
        ys.append(d * sg_ref[:, hd * LANES:(hd + 1) * LANES].astype(F32))
    for g in range(2):
        for r in range(2):
            c = 4 + 2 * g + r
            ys.append(ob_ref[g, r] * sg_ref[:, c * LANES:(c + 1) * LANES].astype(F32))
    for hd in range(4):
        c = 8 + hd
        ys.append(oc_ref[hd, 0] * sg_ref[:, c * LANES:(c + 1) * LANES].astype(F32))

    row0 = i * tl
    def masked(x, first_row):
        r = first_row + lax.broadcasted_iota(jnp.int32, x.shape, 0)
        return jnp.where((r >= 0) & (r < seq_len), x, 0.0)
    ext_sc[0:POOL_HALO, :] = masked(dup_ref[...], row0 - POOL_HALO)
    ext_sc[POOL_HALO:POOL_HALO + tl, :] = masked(du_ref[...], row0)
    ext_sc[POOL_HALO + tl:2 * POOL_HALO + tl, :] = masked(dun_ref[...], row0 + tl)
    t = row0 + lax.broadcasted_iota(jnp.int32, (tl, LANES), 0)
    for g, w in enumerate(POOL_WINDOWS):
        cols = slice(g * LANES, (g + 1) * LANES)
        acc = ext_sc[POOL_HALO - w // 2:POOL_HALO - w // 2 + tl, cols]
        for j in range(-w // 2 + 1, w // 2):
            acc = acc + ext_sc[POOL_HALO + j:POOL_HALO + j + tl, cols]
        cnt = jnp.minimum(t + w // 2, seq_len) - jnp.maximum(t - w // 2, 0)
        cnt = jnp.maximum(cnt, 1).astype(F32)
        pooled = acc / cnt - ext_sc[POOL_HALO:POOL_HALO + tl, cols]
        mixed = jnp.dot(pooled.astype(BF16), wg_ref[g], preferred_element_type=F32)
        c = 12 + g
        ys.append(mixed * dsc_ref[:, cols] * sg_ref[:, c * LANES:(c + 1) * LANES].astype(F32))

    y = jnp.concatenate([v.astype(BF16) for v in ys], axis=1)
    out_ref[...] = h_ref[...] + jnp.dot(y, wout_ref[...], preferred_element_type=F32)


def _rope_lane_tables(n_tok, lp):
    l = N_META + n_tok
    rows = n_tok // GRID_W
    lane = jnp.arange(LANES, dtype=jnp.int32)

    def inv_freq(idx, dim, theta):
        return theta ** (-(2 * idx).astype(F32) / dim)

    def padded(v):
        return jnp.pad(v.astype(F32), (0, lp - l))[:, None]

    pos = padded(jnp.arange(l, dtype=jnp.int32))
    row_pos = padded(jnp.concatenate([jnp.full((N_META,), -1, jnp.int32),
                                      jnp.repeat(jnp.arange(rows, dtype=jnp.int32), GRID_W)]))
    col_pos = padded(jnp.concatenate([jnp.arange(N_META, dtype=jnp.int32),
                                      jnp.tile(jnp.arange(GRID_W, dtype=jnp.int32), rows)]))

    ja = lane % A_QK
    inv_a = jnp.where(ja < A_ROT, inv_freq(ja % (A_ROT // 2), A_ROT, ROPE_THETA), 0.0)
    sign_a = jnp.where(ja < A_ROT // 2, -1.0, 1.0).astype(F32)
    ang_a = pos * inv_a[None, :]
    jb = lane % (B_HD // 2)
    inv_b = inv_freq(jb % (B_HD // 4), B_HD // 2, AXIAL_THETA)
    inv_r = jnp.where(lane < B_HD // 2, inv_b, 0.0)
    inv_c = jnp.where(lane < B_HD // 2, 0.0, inv_b)
    sign_b = jnp.where(jb < B_HD // 4, -1.0, 1.0).astype(F32)
    ang_b = row_pos * inv_r[None, :] + col_pos * inv_c[None, :]
    inv_m = jnp.where(lane < C_ROPE, inv_freq(lane % (C_ROPE // 2), C_ROPE, ROPE_THETA), 0.0)
    sign_m = jnp.where(lane < C_ROPE // 2, -1.0, 1.0).astype(F32)
    ang_m = pos * inv_m[None, :]
    return (jnp.cos(ang_a), jnp.sin(ang_a) * sign_a[None, :],
            jnp.cos(ang_b), jnp.sin(ang_b) * sign_b[None, :],
            jnp.cos(ang_m), jnp.sin(ang_m) * sign_m[None, :])


def _row_spec(tl, width):
    return pl.BlockSpec((tl, width), lambda i: (i, 0))


def _const_spec(shape):
    nd = len(shape)
    return pl.BlockSpec(shape, lambda i: (0,) * nd, pipeline_mode=pl.Buffered(1))


def _proj_call(h, lw, tables, lp, tl):
    n_tiles = lp // tl
    small = [lw["aqn"], lw["akn"], lw["bqn"], lw["bkn"], lw["cql"], lw["ckvl"],
             lw["cqnn"], lw["cqnr"], lw["cknn"], lw["cknr"]]
    in_specs = ([_row_spec(tl, D_MODEL), _const_spec((1, D_MODEL)),
                 _const_spec((D_MODEL, P_END)), _const_spec(lw["wuq"].shape),
                 _const_spec(lw["wukv"].shape)]
                + [_const_spec(a.shape) for a in small]
                + [_row_spec(tl, LANES)] * 6)
    out_widths = [512, 512, 512, 512, 256, 256, 1024, 1024, 512, 2048, 512]
    out_dtypes = [BF16] * 10 + [F32]
    out_shape = [jax.ShapeDtypeStruct((lp, w), dt) for w, dt in zip(out_widths, out_dtypes)]
    out_specs = [_row_spec(tl, w) for w in out_widths]
    return pl.pallas_call(
        _proj_kernel,
        grid=(n_tiles,),
        in_specs=in_specs,
        out_specs=out_specs,
        out_shape=out_shape,
        compiler_params=pltpu.CompilerParams(
            dimension_semantics=("arbitrary",), vmem_limit_bytes=VMEM_LIMIT),
        name="proj",
    )(h, lw["nw"], lw["win"], lw["wuq"], lw["wukv"], *small, *tables)


def _attn_call(q, k, v, mode, lp, n_tok, tq, ck):
    n_heads = {"A": 4, "B": 2, "C": 4}[mode]
    n_grp = {"A": 2, "B": 2, "C": 1}[mode]
    qw = {"A": LANES, "B": 2 * LANES, "C": 2 * LANES}[mode]
    kw = {"A": LANES, "B": LANES, "C": 2 * LANES}[mode]
    n_main = n_tok // ck
    tail_valid = N_META + n_tok - n_main * ck
    kern = functools.partial(_attn_kernel, mode=mode, tq=tq, ck=ck, n_main=n_main,
                             tail_valid=tail_valid)
    return pl.pallas_call(
        kern,
        grid=(n_heads, lp // tq),
        in_specs=[pl.BlockSpec((tq, qw), lambda hd, i: (i, hd)),
                  pl.BlockSpec((lp, kw), lambda hd, i: (0, hd)),
                  pl.BlockSpec((lp, LANES), lambda hd, i: (0, hd))],
        out_specs=pl.BlockSpec((None, n_grp, tq, LANES), lambda hd, i: (hd, 0, i, 0)),
        out_shape=jax.ShapeDtypeStruct((n_heads, n_grp, lp, LANES), F32),
        scratch_shapes=[pltpu.VMEM((lp, 2 * LANES), BF16),
                        pltpu.VMEM((4, n_grp * tq, ck), F32),
                        pltpu.VMEM((4, n_grp * tq, LANES), F32),
                        pltpu.VMEM((n_grp * tq, LANES), F32),
                        pltpu.VMEM((n_grp * tq, 2 * LANES), F32)],
        compiler_params=pltpu.CompilerParams(
            dimension_semantics=("arbitrary", "arbitrary"), vmem_limit_bytes=VMEM_LIMIT),
        name="attn_" + mode,
    )(q, k, v)


def _final_call(h, oa, ob, oc, du, sg, lw, lambda_init, lp, n_tok, tl):
    n_tiles = lp // tl
    hb = tl // POOL_HALO
    n_hblk = lp // POOL_HALO
    kern = functools.partial(_final_kernel, lambda_init=lambda_init,
                             seq_len=N_META + n_tok, tl=tl)
    in_specs = [
        _row_spec(tl, D_MODEL),
        pl.BlockSpec((4, 2, tl, LANES), lambda i: (0, 0, i, 0)),
        pl.BlockSpec((2, 2, tl, LANES), lambda i: (0, 0, i, 0)),
        pl.BlockSpec((4, 1, tl, LANES), lambda i: (0, 0, i, 0)),
        _row_spec(tl, BRANCH_W),
        pl.BlockSpec((POOL_HALO, BRANCH_W), lambda i: (jnp.maximum(i * hb - 1, 0), 0)),
        pl.BlockSpec((POOL_HALO, BRANCH_W), lambda i: (jnp.minimum((i + 1) * hb, n_hblk - 1), 0)),
        _row_spec(tl, D_MODEL),
        _const_spec((D_MODEL, D_MODEL)),
        _const_spec((4, A_QK)),
        _const_spec((1, LANES)),
        _const_spec((4, LANES, LANES)),
        _const_spec((1, BRANCH_W)),
    ]
    return pl.pallas_call(
        kern,
        grid=(n_tiles,),
        in_specs=in_specs,
        out_specs=_row_spec(tl, D_MODEL),
        out_shape=jax.ShapeDtypeStruct((lp, D_MODEL), F32),
        scratch_shapes=[pltpu.VMEM((tl + 2 * POOL_HALO, BRANCH_W), F32)],
        compiler_params=pltpu.CompilerParams(
            dimension_semantics=("arbitrary",), vmem_limit_bytes=VMEM_LIMIT),
        name="final",
    )(h, oa, ob, oc, du, du, du, sg, lw["wout"], lw["lam"], lw["subln"], lw["wg"], lw["dsc"])


def _layer_weights(i, norm_w, w_in, w_out, a_q_norm, a_k_norm, a_lambda, a_subln, b_q_norm,
                   b_k_norm, c_q_lat_norm, c_kv_lat_norm, c_w_uq, c_w_ukv, c_q_norm, c_k_norm,
                   d_w_group, d_scale):
    w = w_in[i]
    col = lambda lo, n: w[:, lo:lo + n].astype(BF16)
    win = jnp.concatenate([
        col(0, 512), col(512, 512), col(1024, 512),
        col(2048, 512), col(2560, 256), col(2816, 256),
        col(3584, 384), col(3968, 256), col(4224, 64), jnp.zeros((D_MODEL, 64), BF16),
        col(1536, 512), col(3072, 512), col(4288, 512), col(5312, 512),
        col(4800, 512),
    ], axis=1)
    uq = c_w_uq[i].reshape(C_Q_LORA, 4, C_NOPE + C_ROPE)
    uq = jnp.pad(uq, ((0, 0), (0, 0), (0, 2 * LANES - C_NOPE - C_ROPE)))
    wuq = uq.reshape(C_Q_LORA, 4 * 2 * LANES).astype(BF16)
    row = lambda v: v.reshape(1, -1).astype(F32)
    pad_row = lambda v: jnp.pad(v.astype(F32), (0, LANES - v.shape[0])).reshape(1, LANES)
    return dict(
        nw=row(norm_w[i]), win=win, wuq=wuq, wukv=c_w_ukv[i].astype(BF16),
        aqn=row(jnp.tile(a_q_norm[i], 2)), akn=row(jnp.tile(a_k_norm[i], 2)),
        bqn=row(b_q_norm[i]), bkn=row(b_k_norm[i]),
        cql=row(c_q_lat_norm[i]), ckvl=row(c_kv_lat_norm[i]),
        cqnn=row(c_q_norm[i][:C_NOPE]), cqnr=pad_row(c_q_norm[i][C_NOPE:]),
        cknn=row(c_k_norm[i][:C_NOPE]), cknr=pad_row(c_k_norm[i][C_NOPE:]),
        wout=w_out[i].astype(BF16), lam=a_lambda[i].astype(F32), subln=row(a_subln[i]),
        wg=d_w_group[i].astype(BF16), dsc=row(d_scale[i]),
    )


def kernel(x, meta_tokens, norm_w, w_in, w_out, a_q_norm, a_k_norm, a_lambda, a_subln, b_q_norm,
           b_k_norm, c_q_lat_norm, c_kv_lat_norm, c_w_uq, c_w_ukv, c_q_norm, c_k_norm,
           d_w_group, d_scale):
    b, n_tok, d = x.shape
    assert b == 1 and d == D_MODEL
    lp = n_tok + SEQ_PAD
    tl = ROW_TILE
    ck = KV_CHUNK
    assert lp % tl == 0 and lp % ATTN_ROWS == 0 and n_tok % (4 * ck) == 0 and n_tok % GRID_W == 0
    depth = norm_w.shape[0]

    h = jnp.concatenate([meta_tokens.astype(x.dtype), x[0],
                         jnp.zeros((lp - N_META - n_tok, d), x.dtype)], axis=0)
    tables = _rope_lane_tables(n_tok, lp)

    for i in range(depth):
        lambda_init = 0.8 - 0.6 * math.exp(-0.3 * i)
        lw = _layer_weights(i, norm_w, w_in, w_out, a_q_norm, a_k_norm, a_lambda, a_subln,
                            b_q_norm, b_k_norm, c_q_lat_norm, c_kv_lat_norm, c_w_uq, c_w_ukv,
                            c_q_norm, c_k_norm, d_w_group, d_scale)
        qa, ka, va, qb, kb, vb, qc, kc, vc, sg, du = _proj_call(h, lw, tables, lp, tl)
        oa = _attn_call(qa, ka, va, "A", lp, n_tok, ATTN_ROWS // 2, ck)
        ob = _attn_call(qb, kb, vb, "B", lp, n_tok, ATTN_ROWS // 2, ck)
        oc = _attn_call(qc, kc, vc, "C", lp, n_tok, ATTN_ROWS, ck)
        h = _final_call(h, oa, ob, oc, du, sg, lw, lambda_init, lp, n_tok, tl)
    return h[N_META:N_META + n_tok][None]
```

```python
import functools
import math

import jax
import jax.numpy as jnp
from jax import lax
from jax.experimental import pallas as pl
from jax.experimental.pallas import tpu as pltpu

F32 = jnp.float32
BF16 = jnp.bfloat16

D_MODEL = 2048
N_META = 16
GRID_W = 64
EPS = 1e-6
ROPE_THETA = 500000.0
AXIAL_THETA = 10000.0
BRANCH_W = 512
LANES = 128
SEQ_PAD = 512
ROW_TILE = 384
ATTN_ROWS = 768
KV_CHUNK = 1024
A_QK = 64
A_ROT = 16
B_HD = 128
C_NOPE = 128
C_ROPE = 64
C_Q_LORA = 384
C_KV_LORA = 256
POOL_WINDOWS = (2, 4, 8, 16)
POOL_HALO = 8
LOG2E = 1.4426950408889634
NEG_BIG = -1e30
VMEM_LIMIT = 56 * 1024 * 1024

P_AQ, P_AV, P_AG = 0, 1024, 1536
P_BQ, P_BV, P_BG = 2048, 2816, 3072
P_CQ = 3584
P_TAIL = 4288
IN_COLS = 5824
P_HEAD_END = P_CQ + C_Q_LORA + C_KV_LORA + LANES


def _mean_sq(x, n):
    return jnp.sum(x * x, axis=-1, keepdims=True) * (1.0 / n)


def _half_mean_sq(x):
    xx = x * x
    lower = lax.broadcasted_iota(jnp.int32, x.shape, 1) < (LANES // 2)
    lo = jnp.sum(jnp.where(lower, xx, 0.0), axis=-1, keepdims=True)
    hi = jnp.sum(jnp.where(lower, 0.0, xx), axis=-1, keepdims=True)
    return jnp.where(lower, lo, hi) * (2.0 / LANES)


def _rope(x, c, s, half):
    lane = lax.broadcasted_iota(jnp.int32, x.shape, 1)
    fwd = pltpu.roll(x, LANES - half, axis=1)
    bwd = pltpu.roll(x, half, axis=1)
    partner = jnp.where((lane % (2 * half)) < half, fwd, bwd)
    return x * c + partner * s


def _proj_kernel(h_ref, nw_ref, win_ref, wtail_ref, wuq_ref, wukv_ref,
                 aqn_ref, akn_ref, bqn_ref, bkn_ref, cql_ref, ckvl_ref,
                 cqnn_ref, cqnr_ref, cknn_ref, cknr_ref,
                 ca_ref, sa_ref, cb_ref, sb_ref, cc_ref, sc_ref,
                 qa_ref, ka_ref, va_ref, qb_ref, kb_ref, vb_ref,
                 qc_ref, kc_ref, vc_ref, sg_ref, du_ref):
    h = h_ref[...]
    ms = jnp.mean(h * h, axis=-1, keepdims=True)
    hb = (h * lax.rsqrt(ms + EPS) * nw_ref[...]).astype(BF16)

    def mm(lo, hi):
        return jnp.dot(hb, win_ref[:, lo:hi], preferred_element_type=F32)

    def mm_tail(blk):
        return jnp.dot(hb, wtail_ref[:, blk * BRANCH_W:(blk + 1) * BRANCH_W],
                       preferred_element_type=F32)

    ca, sa = ca_ref[...], sa_ref[...]
    cb, sb = cb_ref[...], sb_ref[...]
    cc, sc = cc_ref[...], sc_ref[...]

    pa = mm(P_AQ, P_AV)
    a_scale = (A_QK ** -0.5) * LOG2E
    for blk in range(8):
        x = pa[:, blk * LANES:(blk + 1) * LANES]
        w = aqn_ref[...] if blk < 4 else akn_ref[...]
        x = x * lax.rsqrt(_half_mean_sq(x) + EPS) * w
        x = _rope(x, ca, sa, A_ROT // 2)
        if blk < 4:
            qa_ref[:, blk * LANES:(blk + 1) * LANES] = (x * a_scale).astype(BF16)
        else:
            ka_ref[:, (blk - 4) * LANES:(blk - 3) * LANES] = x.astype(BF16)
    va_ref[...] = mm(P_AV, P_AG).astype(BF16)

    pb = mm(P_BQ, P_BV)
    b_scale = (B_HD ** -0.5) * LOG2E
    for blk in range(6):
        x = pb[:, blk * LANES:(blk + 1) * LANES]
        w = bqn_ref[...] if blk < 4 else bkn_ref[...]
        x = x * lax.rsqrt(_mean_sq(x, B_HD) + EPS) * w
        x = _rope(x, cb, sb, B_HD // 4)
        if blk < 4:
            qb_ref[:, blk * LANES:(blk + 1) * LANES] = (x * b_scale).astype(BF16)
        else:
            kb_ref[:, (blk - 4) * LANES:(blk - 3) * LANES] = x.astype(BF16)
    vb_ref[...] = mm(P_BV, P_BG).astype(BF16)

    pc = mm(P_CQ, P_HEAD_END)
    cq = pc[:, 0:C_Q_LORA]
    cq = cq * lax.rsqrt(jnp.mean(cq * cq, axis=-1, keepdims=True) + EPS) * cql_ref[...]
    qfull = jnp.dot(cq.astype(BF16), wuq_ref[...], preferred_element_type=F32)
    ckv = pc[:, C_Q_LORA:C_Q_LORA + C_KV_LORA]
    ckv = ckv * lax.rsqrt(jnp.mean(ckv * ckv, axis=-1, keepdims=True) + EPS) * ckvl_ref[...]
    kvfull = jnp.dot(ckv.astype(BF16), wukv_ref[...], preferred_element_type=F32)
    kr = pc[:, C_Q_LORA + C_KV_LORA:]
    kr = jnp.where(lax.broadcasted_iota(jnp.int32, kr.shape, 1) < C_ROPE, kr, 0.0)
    kr = kr * lax.rsqrt(_mean_sq(kr, C_ROPE) + EPS) * cknr_ref[...]
    kr = _rope(kr, cc, sc, C_ROPE // 2).astype(BF16)
    c_scale = ((C_NOPE + C_ROPE) ** -0.5) * LOG2E
    for hd in range(4):
        base = hd * 2 * LANES
        qn = qfull[:, base:base + LANES]
        qn = qn * lax.rsqrt(_mean_sq(qn, C_NOPE) + EPS) * cqnn_ref[...]
        qr = qfull[:, base + LANES:base + 2 * LANES]
        qr = qr * lax.rsqrt(_mean_sq(qr, C_ROPE) + EPS) * cqnr_ref[...]
        qr = _rope(qr, cc, sc, C_ROPE // 2)
        qc_ref[:, base:base + LANES] = (qn * c_scale).astype(BF16)
        qc_ref[:, base + LANES:base + 2 * LANES] = (qr * c_scale).astype(BF16)
        kn = kvfull[:, base:base + LANES]
        kn = kn * lax.rsqrt(_mean_sq(kn, C_NOPE) + EPS) * cknn_ref[...]
        kc_ref[:, base:base + LANES] = kn.astype(BF16)
        kc_ref[:, base + LANES:base + 2 * LANES] = kr
        vc_ref[:, hd * LANES:(hd + 1) * LANES] = kvfull[:, base + LANES:base + 2 * LANES].astype(BF16)

    gates = [mm(P_AG, P_BQ), mm(P_BG, P_CQ), mm_tail(0), mm_tail(2)]
    for blk, g in enumerate(gates):
        sg_ref[:, blk * BRANCH_W:(blk + 1) * BRANCH_W] = jax.nn.silu(g).astype(BF16)
    du_ref[...] = mm_tail(1)


def _lane_repeat(x, n):
    return jnp.concatenate([x] * n, axis=1)


def _attn_tile(q_ref, k_ref, o_ref, vx_sc, s_sc, mc_sc, m_sc, acc_sc, *,
               mode, tq, ck, n_main, tail_valid):
    if mode == "A":
        q = q_ref[...]
        lane = lax.broadcasted_iota(jnp.int32, q.shape, 1)
        zero = jnp.zeros_like(q)
        qs = jnp.concatenate([jnp.where(lane < A_QK, q, zero),
                              jnp.where(lane >= A_QK, q, zero)], axis=0)
    elif mode == "B":
        qs = jnp.concatenate([q_ref[:, 0:LANES], q_ref[:, LANES:2 * LANES]], axis=0)
    else:
        qs = q_ref[...]
    rows = qs.shape[0]
    n_grp = rows // tq

    m_sc[...] = jnp.full(m_sc.shape, NEG_BIG, F32)
    acc_sc[...] = jnp.zeros(acc_sc.shape, F32)

    def scores(k):
        return lax.dot_general(qs, k, (((1,), (1,)), ((), ())), preferred_element_type=F32)

    def row_max(s):
        return jnp.broadcast_to(jnp.max(s, axis=1, keepdims=True), (rows, LANES))

    def qk_stage(j, slot):
        s = scores(k_ref[pl.ds(pl.multiple_of(j * ck, ck), ck), :])
        s_sc[slot] = s
        mc_sc[slot] = row_max(s)

    def softmax_pv(s_parts, m_cur, vx):
        m_prev = m_sc[...]
        m_new = jnp.maximum(m_prev, m_cur)
        alpha = jnp.exp2(m_prev - m_new)
        p = jnp.concatenate(
            [jnp.exp2(s - _lane_repeat(m_new, s.shape[1] // LANES)).astype(BF16) for s in s_parts],
            axis=1)
        pv = jnp.dot(p, vx, preferred_element_type=F32)
        acc_sc[...] = _lane_repeat(alpha, 2) * acc_sc[...] + pv
        m_sc[...] = m_new

    def pv_stage(j, slot):
        vx = vx_sc[pl.ds(pl.multiple_of(j * ck, ck), 2 * ck), :]
        softmax_pv([s_sc[slot], s_sc[slot + 1]],
                   jnp.maximum(mc_sc[slot], mc_sc[slot + 1]), vx)

    qk_stage(0, 0)
    qk_stage(1, 1)

    def body(i, carry):
        c = 4 * i
        qk_stage(c + 2, 2)
        qk_stage(c + 3, 3)
        pv_stage(c, 0)
        qk_stage(c + 4, 0)
        qk_stage(c + 5, 1)
        pv_stage(c + 2, 2)
        return carry

    lax.fori_loop(0, n_main // 4 - 1, body, 0)
    c = n_main - 4
    qk_stage(c + 2, 2)
    qk_stage(c + 3, 3)
    pv_stage(c, 0)
    t0 = n_main * ck
    s_tail = scores(k_ref[t0:t0 + LANES, :])
    col = lax.broadcasted_iota(jnp.int32, s_tail.shape, 1)
    s_tail = jnp.where(col < tail_valid, s_tail, NEG_BIG)
    pv_stage(c + 2, 2)
    softmax_pv([s_tail], row_max(s_tail), vx_sc[t0:t0 + LANES, :])

    o = acc_sc[:, 0:LANES] / acc_sc[:, LANES:2 * LANES]
    for g in range(n_grp):
        o_ref[g] = o[g * tq:(g + 1) * tq]


def _attn_kernel(q_ref, k_ref, v_ref, o_ref, vx_sc, *scratch, tq, q_valid, **tile_args):
    lp = v_ref.shape[0]

    @pl.when(pl.program_id(1) == 0)
    def _():
        def fill(c, carry):
            r = pl.multiple_of(c * tq, tq)
            vx_sc[pl.ds(r, tq), 0:LANES] = v_ref[pl.ds(r, tq), :]
            vx_sc[pl.ds(r, tq), LANES:2 * LANES] = jnp.ones((tq, LANES), BF16)
            return carry
        lax.fori_loop(0, lp // tq, fill, 0)

    has_rows = pl.program_id(1) * tq < q_valid

    @pl.when(has_rows)
    def _():
        _attn_tile(q_ref, k_ref, o_ref, vx_sc, *scratch, tq=tq, **tile_args)

    @pl.when(jnp.logical_not(has_rows))
    def _():
        o_ref[...] = jnp.zeros(o_ref.shape, F32)


def _final_kernel(h_ref, oa_ref, ob_ref, oc_ref, du_ref, dup_ref, dun_ref, sg_ref,
                  wout_ref, lam_ref, subln_ref, wg_ref, dsc_ref, out_ref, ext_sc, *,
                  lambda_init, seq_len, tl):
    i = pl.program_id(0)

    lp = lam_ref[...]
    lam = (jnp.exp(jnp.sum(lp[0:1, :] * lp[1:2, :], axis=-1, keepdims=True))
           - jnp.exp(jnp.sum(lp[2:3, :] * lp[3:4, :], axis=-1, keepdims=True)) + lambda_init)

    ys = []
    for hd in range(4):
        d = oa_ref[hd, 0] - lam * oa_ref[hd, 1]
        d = d * lax.rsqrt(_mean_sq(d, LANES) + EPS) * subln_ref[...]
        d = d * (1.0 - lambda_init)
        ys.append(d * sg_ref[:, hd * LANES:(hd + 1) * LANES].astype(F32))
    for g in range(2):
        for r in range(2):
            c = 4 + 2 * g + r
            ys.append(ob_ref[g, r] * sg_ref[:, c * LANES:(c + 1) * LANES].astype(F32))
    for hd in range(4):
        c = 8 + hd
        ys.append(oc_ref[hd, 0] * sg_ref[:, c * LANES:(c + 1) * LANES].astype(F32))

    row0 = i * tl
    def masked(x, first_row):
        r = first_row + lax.broadcasted_iota(jnp.int32, x.shape, 0)
        return jnp.where((r >= 0) & (r < seq_len), x, 0.0)
    ext_sc[0:POOL_HALO, :] = masked(dup_ref[...], row0 - POOL_HALO)
    ext_sc[POOL_HALO:POOL_HALO + tl, :] = masked(du_ref[...], row0)
    ext_sc[POOL_HALO + tl:2 * POOL_HALO + tl, :] = masked(dun_ref[...], row0 + tl)
    t = row0 + lax.broadcasted_iota(jnp.int32, (tl, LANES), 0)
    for g, w in enumerate(POOL_WINDOWS):
        cols = slice(g * LANES, (g + 1) * LANES)
        acc = ext_sc[POOL_HALO - w // 2:POOL_HALO - w // 2 + tl, cols]
        for j in range(-w // 2 + 1, w // 2):
            acc = acc + ext_sc[POOL_HALO + j:POOL_HALO + j + tl, cols]
        cnt = jnp.minimum(t + w // 2, seq_len) - jnp.maximum(t - w // 2, 0)
        cnt = jnp.maximum(cnt, 1).astype(F32)
        pooled = acc / cnt - ext_sc[POOL_HALO:POOL_HALO + tl, cols]
        mixed = jnp.dot(pooled.astype(BF16), wg_ref[g], preferred_element_type=F32)
        c = 12 + g
        ys.append(mixed * dsc_ref[:, cols] * sg_ref[:, c * LANES:(c + 1) * LANES].astype(F32))

    y = jnp.concatenate([v.astype(BF16) for v in ys], axis=1)
    out_ref[...] = h_ref[...] + jnp.dot(y, wout_ref[...], preferred_element_type=F32)


def _rope_lane_tables(n_tok, lp):
    l = N_META + n_tok
    rows = n_tok // GRID_W
    lane = jnp.arange(LANES, dtype=jnp.int32)

    def inv_freq(idx, dim, theta):
        return theta ** (-(2 * idx).astype(F32) / dim)

    def padded(v):
        return jnp.pad(v.astype(F32), (0, lp - l))[:, None]

    pos = padded(jnp.arange(l, dtype=jnp.int32))
    row_pos = padded(jnp.concatenate([jnp.full((N_META,), -1, jnp.int32),
                                      jnp.repeat(jnp.arange(rows, dtype=jnp.int32), GRID_W)]))
    col_pos = padded(jnp.concatenate([jnp.arange(N_META, dtype=jnp.int32),
                                      jnp.tile(jnp.arange(GRID_W, dtype=jnp.int32), rows)]))

    ja = lane % A_QK
    inv_a = jnp.where(ja < A_ROT, inv_freq(ja % (A_ROT // 2), A_ROT, ROPE_THETA), 0.0)
    sign_a = jnp.where(ja < A_ROT // 2, -1.0, 1.0).astype(F32)
    ang_a = pos * inv_a[None, :]
    jb = lane % (B_HD // 2)
    inv_b = inv_freq(jb % (B_HD // 4), B_HD // 2, AXIAL_THETA)
    inv_r = jnp.where(lane < B_HD // 2, inv_b, 0.0)
    inv_c = jnp.where(lane < B_HD // 2, 0.0, inv_b)
    sign_b = jnp.where(jb < B_HD // 4, -1.0, 1.0).astype(F32)
    ang_b = row_pos * inv_r[None, :] + col_pos * inv_c[None, :]
    inv_m = jnp.where(lane < C_ROPE, inv_freq(lane % (C_ROPE // 2), C_ROPE, ROPE_THETA), 0.0)
    sign_m = jnp.where(lane < C_ROPE // 2, -1.0, 1.0).astype(F32)
    ang_m = pos * inv_m[None, :]
    return (jnp.cos(ang_a), jnp.sin(ang_a) * sign_a[None, :],
            jnp.cos(ang_b), jnp.sin(ang_b) * sign_b[None, :],
            jnp.cos(ang_m), jnp.sin(ang_m) * sign_m[None, :])


def _row_spec(tl, width):
    return pl.BlockSpec((tl, width), lambda i: (i, 0))


def _const_spec(shape):
    nd = len(shape)
    return pl.BlockSpec(shape, lambda i: (0,) * nd, pipeline_mode=pl.Buffered(1))


def _proj_call(h, lw, tables, lp, tl):
    n_tiles = lp // tl
    small = [lw["aqn"], lw["akn"], lw["bqn"], lw["bkn"], lw["cql"], lw["ckvl"],
             lw["cqnn"], lw["cqnr"], lw["cknn"], lw["cknr"]]
    in_specs = ([_row_spec(tl, D_MODEL), _const_spec((1, D_MODEL)),
                 _const_spec((D_MODEL, P_HEAD_END)), _const_spec((D_MODEL, IN_COLS - P_TAIL)),
                 _const_spec(lw["wuq"].shape),
                 _const_spec(lw["wukv"].shape)]
                + [_const_spec(a.shape) for a in small]
                + [_row_spec(tl, LANES)] * 6)
    out_widths = [512, 512, 512, 512, 256, 256, 1024, 1024, 512, 2048, 512]
    out_dtypes = [BF16] * 10 + [F32]
    out_shape = [jax.ShapeDtypeStruct((lp, w), dt) for w, dt in zip(out_widths, out_dtypes)]
    out_specs = [_row_spec(tl, w) for w in out_widths]
    return pl.pallas_call(
        _proj_kernel,
        grid=(n_tiles,),
        in_specs=in_specs,
        out_specs=out_specs,
        out_shape=out_shape,
        compiler_params=pltpu.CompilerParams(
            dimension_semantics=("arbitrary",), vmem_limit_bytes=VMEM_LIMIT),
        name="proj",
    )(h, lw["nw"], lw["win"], lw["wtail"], lw["wuq"], lw["wukv"], *small, *tables)


def _attn_call(q, k, v, mode, lp, n_tok, tq, ck):
    n_heads = {"A": 4, "B": 2, "C": 4}[mode]
    n_grp = {"A": 2, "B": 2, "C": 1}[mode]
    qw = {"A": LANES, "B": 2 * LANES, "C": 2 * LANES}[mode]
    kw = {"A": LANES, "B": LANES, "C": 2 * LANES}[mode]
    n_main = n_tok // ck
    tail_valid = N_META + n_tok - n_main * ck
    kern = functools.partial(_attn_kernel, mode=mode, tq=tq, ck=ck, n_main=n_main,
                             tail_valid=tail_valid, q_valid=N_META + n_tok)
    return pl.pallas_call(
        kern,
        grid=(n_heads, lp // tq),
        in_specs=[pl.BlockSpec((tq, qw), lambda hd, i: (i, hd)),
                  pl.BlockSpec((lp, kw), lambda hd, i: (0, hd)),
                  pl.BlockSpec((lp, LANES), lambda hd, i: (0, hd))],
        out_specs=pl.BlockSpec((None, n_grp, tq, LANES), lambda hd, i: (hd, 0, i, 0)),
        out_shape=jax.ShapeDtypeStruct((n_heads, n_grp, lp, LANES), F32),
        scratch_shapes=[pltpu.VMEM((lp, 2 * LANES), BF16),
                        pltpu.VMEM((4, n_grp * tq, ck), F32),
                        pltpu.VMEM((4, n_grp * tq, LANES), F32),
                        pltpu.VMEM((n_grp * tq, LANES), F32),
                        pltpu.VMEM((n_grp * tq, 2 * LANES), F32)],
        compiler_params=pltpu.CompilerParams(
            dimension_semantics=("arbitrary", "arbitrary"), vmem_limit_bytes=VMEM_LIMIT),
        name="attn_" + mode,
    )(q, k, v)


def _final_call(h, oa, ob, oc, du, sg, lw, lambda_init, lp, n_tok, tl):
    n_tiles = lp // tl
    hb = tl // POOL_HALO
    n_hblk = lp // POOL_HALO
    kern = functools.partial(_final_kernel, lambda_init=lambda_init,
                             seq_len=N_META + n_tok, tl=tl)
    in_specs = [
        _row_spec(tl, D_MODEL),
        pl.BlockSpec((4, 2, tl, LANES), lambda i: (0, 0, i, 0)),
        pl.BlockSpec((2, 2, tl, LANES), lambda i: (0, 0, i, 0)),
        pl.BlockSpec((4, 1, tl, LANES), lambda i: (0, 0, i, 0)),
        _row_spec(tl, BRANCH_W),
        pl.BlockSpec((POOL_HALO, BRANCH_W), lambda i: (jnp.maximum(i * hb - 1, 0), 0)),
        pl.BlockSpec((POOL_HALO, BRANCH_W), lambda i: (jnp.minimum((i + 1) * hb, n_hblk - 1), 0)),
        _row_spec(tl, D_MODEL),
        _const_spec((D_MODEL, D_MODEL)),
        _const_spec((4, A_QK)),
        _const_spec((1, LANES)),
        _const_spec((4, LANES, LANES)),
        _const_spec((1, BRANCH_W)),
    ]
    return pl.pallas_call(
        kern,
        grid=(n_tiles,),
        in_specs=in_specs,
        out_specs=_row_spec(tl, D_MODEL),
        out_shape=jax.ShapeDtypeStruct((lp, D_MODEL), F32),
        scratch_shapes=[pltpu.VMEM((tl + 2 * POOL_HALO, BRANCH_W), F32)],
        compiler_params=pltpu.CompilerParams(
            dimension_semantics=("arbitrary",), vmem_limit_bytes=VMEM_LIMIT),
        name="final",
    )(h, oa, ob, oc, du, du, du, sg, lw["wout"], lw["lam"], lw["subln"], lw["wg"], lw["dsc"])


def _layer_weights(i, norm_w, w_in, w_out, a_q_norm, a_k_norm, a_lambda, a_subln, b_q_norm,
                   b_k_norm, c_q_lat_norm, c_kv_lat_norm, c_w_uq, c_w_ukv, c_q_norm, c_k_norm,
                   d_w_group, d_scale):
    win = w_in[i][:, :P_HEAD_END].astype(BF16)
    wtail = w_in[i][:, P_TAIL:].astype(BF16)
    uq = c_w_uq[i].reshape(C_Q_LORA, 4, C_NOPE + C_ROPE)
    uq = jnp.pad(uq, ((0, 0), (0, 0), (0, 2 * LANES - C_NOPE - C_ROPE)))
    wuq = uq.reshape(C_Q_LORA, 4 * 2 * LANES).astype(BF16)
    row = lambda v: v.reshape(1, -1).astype(F32)
    pad_row = lambda v: jnp.pad(v.astype(F32), (0, LANES - v.shape[0])).reshape(1, LANES)
    return dict(
        nw=row(norm_w[i]), win=win, wtail=wtail, wuq=wuq, wukv=c_w_ukv[i].astype(BF16),
        aqn=row(jnp.tile(a_q_norm[i], 2)), akn=row(jnp.tile(a_k_norm[i], 2)),
        bqn=row(b_q_norm[i]), bkn=row(b_k_norm[i]),
        cql=row(c_q_lat_norm[i]), ckvl=row(c_kv_lat_norm[i]),
        cqnn=row(c_q_norm[i][:C_NOPE]), cqnr=pad_row(c_q_norm[i][C_NOPE:]),
        cknn=row(c_k_norm[i][:C_NOPE]), cknr=pad_row(c_k_norm[i][C_NOPE:]),
        wout=w_out[i].astype(BF16), lam=a_lambda[i].astype(F32), subln=row(a_subln[i]),
        wg=d_w_group[i].astype(BF16), dsc=row(d_scale[i]),
    )


def kernel(x, meta_tokens, norm_w, w_in, w_out, a_q_norm, a_k_norm, a_lambda, a_subln, b_q_norm,
           b_k_norm, c_q_lat_norm, c_kv_lat_norm, c_w_uq, c_w_ukv, c_q_norm, c_k_norm,
           d_w_group, d_scale):
    b, n_tok, d = x.shape
    assert b == 1 and d == D_MODEL
    lp = n_tok + SEQ_PAD
    tl = ROW_TILE
    ck = KV_CHUNK
    assert lp % tl == 0 and lp % ATTN_ROWS == 0 and n_tok % (4 * ck) == 0 and n_tok % GRID_W == 0
    depth = norm_w.shape[0]

    h = jnp.concatenate([meta_tokens.astype(x.dtype), x[0],
                         jnp.zeros((lp - N_META - n_tok, d), x.dtype)], axis=0)
    tables = _rope_lane_tables(n_tok, lp)

    for i in range(depth):
        lambda_init = 0.8 - 0.6 * math.exp(-0.3 * i)
        lw = _layer_weights(i, norm_w, w_in, w_out, a_q_norm, a_k_norm, a_lambda, a_subln,
                            b_q_norm, b_k_norm, c_q_lat_norm, c_kv_lat_norm, c_w_uq, c_w_ukv,
                            c_q_norm, c_k_norm, d_w_group, d_scale)
        qa, ka, va, qb, kb, vb, qc, kc, vc, sg, du = _proj_call(h, lw, tables, lp, tl)
        oa = _attn_call(qa, ka, va, "A", lp, n_tok, ATTN_ROWS // 2, ck)
        ob = _attn_call(qb, kb, vb, "B", lp, n_tok, ATTN_ROWS // 2, ck)
        oc = _attn_call(qc, kc, vc, "C", lp, n_tok, ATTN_ROWS, ck)
        h = _final_call(h, oa, ob, oc, du, sg, lw, lambda_init, lp, n_tok, tl)
    return h[N_META:N_META + n_tok][None]
```

```python
import functools
import math

import jax
import jax.numpy as jnp
from jax import lax
from jax.experimental import pallas as pl
from jax.experimental.pallas import tpu as pltpu

F32 = jnp.float32
BF16 = jnp.bfloat16

D_MODEL = 2048
N_META = 16
GRID_W = 64
EPS = 1e-6
ROPE_THETA = 500000.0
AXIAL_THETA = 10000.0
BRANCH_W = 512
LANES = 128
SEQ_PAD = 512
ROW_TILE = 384
ATTN_ROWS = 768
KV_CHUNK = 1024
A_QK = 64
A_ROT = 16
B_HD = 128
C_NOPE = 128
C_ROPE = 64
C_Q_LORA = 384
C_KV_LORA = 256
POOL_WINDOWS = (2, 4, 8, 16)
POOL_HALO = 8
LOG2E = 1.4426950408889634
NEG_BIG = -1e30
VMEM_LIMIT = 56 * 1024 * 1024

P_AQ, P_AV, P_AG = 0, 1024, 1536
P_BQ, P_BV, P_BG = 2048, 2816, 3072
P_CQ = 3584
P_TAIL = 4288
IN_COLS = 5824
P_HEAD_END = P_CQ + C_Q_LORA + C_KV_LORA + LANES


def _mean_sq(x, n):
    return jnp.sum(x * x, axis=-1, keepdims=True) * (1.0 / n)


def _half_mean_sq(x):
    xx = x * x
    lower = lax.broadcasted_iota(jnp.int32, x.shape, 1) < (LANES // 2)
    lo = jnp.sum(jnp.where(lower, xx, 0.0), axis=-1, keepdims=True)
    hi = jnp.sum(jnp.where(lower, 0.0, xx), axis=-1, keepdims=True)
    return jnp.where(lower, lo, hi) * (2.0 / LANES)


def _rope(x, c, s, half):
    lane = lax.broadcasted_iota(jnp.int32, x.shape, 1)
    fwd = pltpu.roll(x, LANES - half, axis=1)
    bwd = pltpu.roll(x, half, axis=1)
    partner = jnp.where((lane % (2 * half)) < half, fwd, bwd)
    return x * c + partner * s


def _proj_tile(h_ref, nw_ref, win_ref, wtail_ref, wuq_ref, wukv_ref,
               aqn_ref, akn_ref, bqn_ref, bkn_ref, cql_ref, ckvl_ref,
               cqnn_ref, cqnr_ref, cknn_ref, cknr_ref,
               ca_ref, sa_ref, cb_ref, sb_ref, cc_ref, sc_ref,
               qa_ref, ka_ref, va_ref, qb_ref, kb_ref, vb_ref,
               qc_ref, kc_ref, vc_ref, sg_ref, du_ref):
    h = h_ref[...]
    ms = jnp.mean(h * h, axis=-1, keepdims=True)
    hb = (h * lax.rsqrt(ms + EPS) * nw_ref[...]).astype(BF16)

    def mm(lo, hi):
        return jnp.dot(hb, win_ref[:, lo:hi], preferred_element_type=F32)

    def mm_tail(blk):
        return jnp.dot(hb, wtail_ref[:, blk * BRANCH_W:(blk + 1) * BRANCH_W],
                       preferred_element_type=F32)

    ca, sa = ca_ref[...], sa_ref[...]
    cb, sb = cb_ref[...], sb_ref[...]
    cc, sc = cc_ref[...], sc_ref[...]

    pa = mm(P_AQ, P_AV)
    a_scale = (A_QK ** -0.5) * LOG2E
    for blk in range(8):
        x = pa[:, blk * LANES:(blk + 1) * LANES]
        w = aqn_ref[...] if blk < 4 else akn_ref[...]
        x = x * lax.rsqrt(_half_mean_sq(x) + EPS) * w
        x = _rope(x, ca, sa, A_ROT // 2)
        if blk < 4:
            qa_ref[:, blk * LANES:(blk + 1) * LANES] = (x * a_scale).astype(BF16)
        else:
            ka_ref[:, (blk - 4) * LANES:(blk - 3) * LANES] = x.astype(BF16)
    va_ref[...] = mm(P_AV, P_AG).astype(BF16)

    pb = mm(P_BQ, P_BV)
    b_scale = (B_HD ** -0.5) * LOG2E
    for blk in range(6):
        x = pb[:, blk * LANES:(blk + 1) * LANES]
        w = bqn_ref[...] if blk < 4 else bkn_ref[...]
        x = x * lax.rsqrt(_mean_sq(x, B_HD) + EPS) * w
        x = _rope(x, cb, sb, B_HD // 4)
        if blk < 4:
            qb_ref[:, blk * LANES:(blk + 1) * LANES] = (x * b_scale).astype(BF16)
        else:
            kb_ref[:, (blk - 4) * LANES:(blk - 3) * LANES] = x.astype(BF16)
    vb_ref[...] = mm(P_BV, P_BG).astype(BF16)

    pc = mm(P_CQ, P_HEAD_END)
    cq = pc[:, 0:C_Q_LORA]
    cq = cq * lax.rsqrt(jnp.mean(cq * cq, axis=-1, keepdims=True) + EPS) * cql_ref[...]
    qfull = jnp.dot(cq.astype(BF16), wuq_ref[...], preferred_element_type=F32)
    ckv = pc[:, C_Q_LORA:C_Q_LORA + C_KV_LORA]
    ckv = ckv * lax.rsqrt(jnp.mean(ckv * ckv, axis=-1, keepdims=True) + EPS) * ckvl_ref[...]
    kvfull = jnp.dot(ckv.astype(BF16), wukv_ref[...], preferred_element_type=F32)
    kr = pc[:, C_Q_LORA + C_KV_LORA:]
    kr = jnp.where(lax.broadcasted_iota(jnp.int32, kr.shape, 1) < C_ROPE, kr, 0.0)
    kr = kr * lax.rsqrt(_mean_sq(kr, C_ROPE) + EPS) * cknr_ref[...]
    kr = _rope(kr, cc, sc, C_ROPE // 2).astype(BF16)
    c_scale = ((C_NOPE + C_ROPE) ** -0.5) * LOG2E
    for hd in range(4):
        base = hd * 2 * LANES
        qn = qfull[:, base:base + LANES]
        qn = qn * lax.rsqrt(_mean_sq(qn, C_NOPE) + EPS) * cqnn_ref[...]
        qr = qfull[:, base + LANES:base + 2 * LANES]
        qr = qr * lax.rsqrt(_mean_sq(qr, C_ROPE) + EPS) * cqnr_ref[...]
        qr = _rope(qr, cc, sc, C_ROPE // 2)
        qc_ref[:, base:base + LANES] = (qn * c_scale).astype(BF16)
        qc_ref[:, base + LANES:base + 2 * LANES] = (qr * c_scale).astype(BF16)
        kn = kvfull[:, base:base + LANES]
        kn = kn * lax.rsqrt(_mean_sq(kn, C_NOPE) + EPS) * cknn_ref[...]
        kc_ref[:, base:base + LANES] = kn.astype(BF16)
        kc_ref[:, base + LANES:base + 2 * LANES] = kr
        vc_ref[:, hd * LANES:(hd + 1) * LANES] = kvfull[:, base + LANES:base + 2 * LANES].astype(BF16)

    gates = [mm(P_AG, P_BQ), mm(P_BG, P_CQ), mm_tail(0), mm_tail(2)]
    for blk, g in enumerate(gates):
        sg_ref[:, blk * BRANCH_W:(blk + 1) * BRANCH_W] = jax.nn.silu(g).astype(BF16)
    du_ref[...] = mm_tail(1)


def _lane_repeat(x, n):
    return jnp.concatenate([x] * n, axis=1)


def _attn_tile(q_ref, k_ref, o_ref, vx_sc, s_sc, mc_sc, m_sc, acc_sc, *,
               mode, tq, ck, n_main, tail_valid):
    if mode == "A":
        q = q_ref[...]
        lane = lax.broadcasted_iota(jnp.int32, q.shape, 1)
        zero = jnp.zeros_like(q)
        qs = jnp.concatenate([jnp.where(lane < A_QK, q, zero),
                              jnp.where(lane >= A_QK, q, zero)], axis=0)
    elif mode == "B":
        qs = jnp.concatenate([q_ref[:, 0:LANES], q_ref[:, LANES:2 * LANES]], axis=0)
    else:
        qs = q_ref[...]
    rows = qs.shape[0]
    n_grp = rows // tq

    m_sc[...] = jnp.full(m_sc.shape, NEG_BIG, F32)
    acc_sc[...] = jnp.zeros(acc_sc.shape, F32)

    def scores(k):
        return lax.dot_general(qs, k, (((1,), (1,)), ((), ())), preferred_element_type=F32)

    def row_max(s):
        return jnp.broadcast_to(jnp.max(s, axis=1, keepdims=True), (rows, LANES))

    def lane_max(s):
        parts = [s[:, t * LANES:(t + 1) * LANES] for t in range(s.shape[1] // LANES)]
        return functools.reduce(jnp.maximum, parts)

    def qk_stage(j, slot):
        s = scores(k_ref[pl.ds(pl.multiple_of(j * ck, ck), ck), :])
        s_sc[slot] = s
        mc_sc[slot] = lane_max(s)

    def softmax_pv(s_parts, m_cur, vx):
        m_prev = m_sc[...]
        m_new = jnp.maximum(m_prev, m_cur)
        alpha = jnp.exp2(m_prev - m_new)
        p = jnp.concatenate(
            [jnp.exp2(s - _lane_repeat(m_new, s.shape[1] // LANES)).astype(BF16) for s in s_parts],
            axis=1)
        pv = jnp.dot(p, vx, preferred_element_type=F32)
        acc_sc[...] = _lane_repeat(alpha, 2) * acc_sc[...] + pv
        m_sc[...] = m_new

    def pv_stage(j, slot):
        vx = vx_sc[pl.ds(pl.multiple_of(j * ck, ck), 2 * ck), :]
        softmax_pv([s_sc[slot], s_sc[slot + 1]],
                   row_max(jnp.maximum(mc_sc[slot], mc_sc[slot + 1])), vx)

    qk_stage(0, 0)
    qk_stage(1, 1)

    def body(i, carry):
        c = 4 * i
        qk_stage(c + 2, 2)
        qk_stage(c + 3, 3)
        pv_stage(c, 0)
        qk_stage(c + 4, 0)
        qk_stage(c + 5, 1)
        pv_stage(c + 2, 2)
        return carry

    lax.fori_loop(0, n_main // 4 - 1, body, 0)
    c = n_main - 4
    qk_stage(c + 2, 2)
    qk_stage(c + 3, 3)
    pv_stage(c, 0)
    t0 = n_main * ck
    s_tail = scores(k_ref[t0:t0 + LANES, :])
    col = lax.broadcasted_iota(jnp.int32, s_tail.shape, 1)
    s_tail = jnp.where(col < tail_valid, s_tail, NEG_BIG)
    pv_stage(c + 2, 2)
    softmax_pv([s_tail], row_max(s_tail), vx_sc[t0:t0 + LANES, :])

    o = acc_sc[:, 0:LANES] / acc_sc[:, LANES:2 * LANES]
    for g in range(n_grp):
        o_ref[g] = o[g * tq:(g + 1) * tq]


def _attn_kernel(q_ref, k_ref, v_ref, o_ref, vx_sc, *scratch, tq, q_valid, **tile_args):
    lp = v_ref.shape[0]

    @pl.when(pl.program_id(1) == 0)
    def _():
        def fill(c, carry):
            r = pl.multiple_of(c * tq, tq)
            vx_sc[pl.ds(r, tq), 0:LANES] = v_ref[pl.ds(r, tq), :]
            vx_sc[pl.ds(r, tq), LANES:2 * LANES] = jnp.ones((tq, LANES), BF16)
            return carry
        lax.fori_loop(0, lp // tq, fill, 0)

    has_rows = pl.program_id(1) * tq < q_valid

    @pl.when(has_rows)
    def _():
        _attn_tile(q_ref, k_ref, o_ref, vx_sc, *scratch, tq=tq, **tile_args)

    @pl.when(jnp.logical_not(has_rows))
    def _():
        o_ref[...] = jnp.zeros(o_ref.shape, F32)


def _final_tile(h_ref, oa_ref, ob_ref, oc_ref, du_ref, dup_ref, dun_ref, sg_ref,
                wout_ref, lam_ref, subln_ref, wg_ref, dsc_ref, out_ref, ext_sc, *,
                lambda_init, seq_len, tl):
    i = pl.program_id(0)

    lp = lam_ref[...]
    lam = (jnp.exp(jnp.sum(lp[0:1, :] * lp[1:2, :], axis=-1, keepdims=True))
           - jnp.exp(jnp.sum(lp[2:3, :] * lp[3:4, :], axis=-1, keepdims=True)) + lambda_init)

    ys = []
    for hd in range(4):
        d = oa_ref[hd, 0] - lam * oa_ref[hd, 1]
        d = d * lax.rsqrt(_mean_sq(d, LANES) + EPS) * subln_ref[...]
        d = d * (1.0 - lambda_init)
        ys.append(d * sg_ref[:, hd * LANES:(hd + 1) * LANES].astype(F32))
    for g in range(2):
        for r in range(2):
            c = 4 + 2 * g + r
            ys.append(ob_ref[g, r] * sg_ref[:, c * LANES:(c + 1) * LANES].astype(F32))
    for hd in range(4):
        c = 8 + hd
        ys.append(oc_ref[hd, 0] * sg_ref[:, c * LANES:(c + 1) * LANES].astype(F32))

    row0 = i * tl
    def masked(x, first_row):
        r = first_row + lax.broadcasted_iota(jnp.int32, x.shape, 0)
        return jnp.where((r >= 0) & (r < seq_len), x, 0.0)
    ext_sc[0:POOL_HALO, :] = masked(dup_ref[...], row0 - POOL_HALO)
    ext_sc[POOL_HALO:POOL_HALO + tl, :] = masked(du_ref[...], row0)
    ext_sc[POOL_HALO + tl:2 * POOL_HALO + tl, :] = masked(dun_ref[...], row0 + tl)
    t = row0 + lax.broadcasted_iota(jnp.int32, (tl, LANES), 0)
    for g, w in enumerate(POOL_WINDOWS):
        cols = slice(g * LANES, (g + 1) * LANES)
        acc = ext_sc[POOL_HALO - w // 2:POOL_HALO - w // 2 + tl, cols]
        for j in range(-w // 2 + 1, w // 2):
            acc = acc + ext_sc[POOL_HALO + j:POOL_HALO + j + tl, cols]
        cnt = jnp.minimum(t + w // 2, seq_len) - jnp.maximum(t - w // 2, 0)
        cnt = jnp.maximum(cnt, 1).astype(F32)
        pooled = acc / cnt - ext_sc[POOL_HALO:POOL_HALO + tl, cols]
        mixed = jnp.dot(pooled.astype(BF16), wg_ref[g], preferred_element_type=F32)
        c = 12 + g
        ys.append(mixed * dsc_ref[:, cols] * sg_ref[:, c * LANES:(c + 1) * LANES].astype(F32))

    y = jnp.concatenate([v.astype(BF16) for v in ys], axis=1)
    out_ref[...] = h_ref[...] + jnp.dot(y, wout_ref[...], preferred_element_type=F32)


def _rope_lane_tables(n_tok, lp):
    l = N_META + n_tok
    rows = n_tok // GRID_W
    lane = jnp.arange(LANES, dtype=jnp.int32)

    def inv_freq(idx, dim, theta):
        return theta ** (-(2 * idx).astype(F32) / dim)

    def padded(v):
        return jnp.pad(v.astype(F32), (0, lp - l))[:, None]

    pos = padded(jnp.arange(l, dtype=jnp.int32))
    row_pos = padded(jnp.concatenate([jnp.full((N_META,), -1, jnp.int32),
                                      jnp.repeat(jnp.arange(rows, dtype=jnp.int32), GRID_W)]))
    col_pos = padded(jnp.concatenate([jnp.arange(N_META, dtype=jnp.int32),
                                      jnp.tile(jnp.arange(GRID_W, dtype=jnp.int32), rows)]))

    ja = lane % A_QK
    inv_a = jnp.where(ja < A_ROT, inv_freq(ja % (A_ROT // 2), A_ROT, ROPE_THETA), 0.0)
    sign_a = jnp.where(ja < A_ROT // 2, -1.0, 1.0).astype(F32)
    ang_a = pos * inv_a[None, :]
    jb = lane % (B_HD // 2)
    inv_b = inv_freq(jb % (B_HD // 4), B_HD // 2, AXIAL_THETA)
    inv_r = jnp.where(lane < B_HD // 2, inv_b, 0.0)
    inv_c = jnp.where(lane < B_HD // 2, 0.0, inv_b)
    sign_b = jnp.where(jb < B_HD // 4, -1.0, 1.0).astype(F32)
    ang_b = row_pos * inv_r[None, :] + col_pos * inv_c[None, :]
    inv_m = jnp.where(lane < C_ROPE, inv_freq(lane % (C_ROPE // 2), C_ROPE, ROPE_THETA), 0.0)
    sign_m = jnp.where(lane < C_ROPE // 2, -1.0, 1.0).astype(F32)
    ang_m = pos * inv_m[None, :]
    return (jnp.cos(ang_a), jnp.sin(ang_a) * sign_a[None, :],
            jnp.cos(ang_b), jnp.sin(ang_b) * sign_b[None, :],
            jnp.cos(ang_m), jnp.sin(ang_m) * sign_m[None, :])


def _row_tile_kernel(tile_fn, n_out, n_scratch, tl, seq_len):
    def kern(*refs):
        outs = refs[len(refs) - n_scratch - n_out:len(refs) - n_scratch]
        has_rows = pl.program_id(0) * tl < seq_len

        @pl.when(has_rows)
        def _():
            tile_fn(*refs)

        @pl.when(jnp.logical_not(has_rows))
        def _():
            for o in outs:
                o[...] = jnp.zeros(o.shape, o.dtype)
    return kern


def _row_spec(tl, width):
    return pl.BlockSpec((tl, width), lambda i: (i, 0))


def _const_spec(shape):
    nd = len(shape)
    return pl.BlockSpec(shape, lambda i: (0,) * nd, pipeline_mode=pl.Buffered(1))


def _proj_call(h, lw, tables, lp, tl, seq_len):
    n_tiles = lp // tl
    small = [lw["aqn"], lw["akn"], lw["bqn"], lw["bkn"], lw["cql"], lw["ckvl"],
             lw["cqnn"], lw["cqnr"], lw["cknn"], lw["cknr"]]
    in_specs = ([_row_spec(tl, D_MODEL), _const_spec((1, D_MODEL)),
                 _const_spec((D_MODEL, P_HEAD_END)), _const_spec((D_MODEL, IN_COLS - P_TAIL)),
                 _const_spec(lw["wuq"].shape),
                 _const_spec(lw["wukv"].shape)]
                + [_const_spec(a.shape) for a in small]
                + [_row_spec(tl, LANES)] * 6)
    out_widths = [512, 512, 512, 512, 256, 256, 1024, 1024, 512, 2048, 512]
    out_dtypes = [BF16] * 10 + [F32]
    out_shape = [jax.ShapeDtypeStruct((lp, w), dt) for w, dt in zip(out_widths, out_dtypes)]
    out_specs = [_row_spec(tl, w) for w in out_widths]
    return pl.pallas_call(
        _row_tile_kernel(_proj_tile, len(out_widths), 0, tl, seq_len),
        grid=(n_tiles,),
        in_specs=in_specs,
        out_specs=out_specs,
        out_shape=out_shape,
        compiler_params=pltpu.CompilerParams(
            dimension_semantics=("arbitrary",), vmem_limit_bytes=VMEM_LIMIT),
        name="proj",
    )(h, lw["nw"], lw["win"], lw["wtail"], lw["wuq"], lw["wukv"], *small, *tables)


def _attn_call(q, k, v, mode, lp, n_tok, tq, ck):
    n_heads = {"A": 4, "B": 2, "C": 4}[mode]
    n_grp = {"A": 2, "B": 2, "C": 1}[mode]
    qw = {"A": LANES, "B": 2 * LANES, "C": 2 * LANES}[mode]
    kw = {"A": LANES, "B": LANES, "C": 2 * LANES}[mode]
    n_main = n_tok // ck
    tail_valid = N_META + n_tok - n_main * ck
    kern = functools.partial(_attn_kernel, mode=mode, tq=tq, ck=ck, n_main=n_main,
                             tail_valid=tail_valid, q_valid=N_META + n_tok)
    return pl.pallas_call(
        kern,
        grid=(n_heads, lp // tq),
        in_specs=[pl.BlockSpec((tq, qw), lambda hd, i: (i, hd)),
                  pl.BlockSpec((lp, kw), lambda hd, i: (0, hd)),
                  pl.BlockSpec((lp, LANES), lambda hd, i: (0, hd))],
        out_specs=pl.BlockSpec((None, n_grp, tq, LANES), lambda hd, i: (hd, 0, i, 0)),
        out_shape=jax.ShapeDtypeStruct((n_heads, n_grp, lp, LANES), F32),
        scratch_shapes=[pltpu.VMEM((lp, 2 * LANES), BF16),
                        pltpu.VMEM((4, n_grp * tq, ck), F32),
                        pltpu.VMEM((4, n_grp * tq, LANES), F32),
                        pltpu.VMEM((n_grp * tq, LANES), F32),
                        pltpu.VMEM((n_grp * tq, 2 * LANES), F32)],
        compiler_params=pltpu.CompilerParams(
            dimension_semantics=("arbitrary", "arbitrary"), vmem_limit_bytes=VMEM_LIMIT),
        name="attn_" + mode,
    )(q, k, v)


def _final_call(h, oa, ob, oc, du, sg, lw, lambda_init, lp, n_tok, tl):
    n_tiles = lp // tl
    hb = tl // POOL_HALO
    n_hblk = lp // POOL_HALO
    seq_len = N_META + n_tok
    tile_fn = functools.partial(_final_tile, lambda_init=lambda_init, seq_len=seq_len, tl=tl)
    kern = _row_tile_kernel(tile_fn, 1, 1, tl, seq_len)
    in_specs = [
        _row_spec(tl, D_MODEL),
        pl.BlockSpec((4, 2, tl, LANES), lambda i: (0, 0, i, 0)),
        pl.BlockSpec((2, 2, tl, LANES), lambda i: (0, 0, i, 0)),
        pl.BlockSpec((4, 1, tl, LANES), lambda i: (0, 0, i, 0)),
        _row_spec(tl, BRANCH_W),
        pl.BlockSpec((POOL_HALO, BRANCH_W), lambda i: (jnp.maximum(i * hb - 1, 0), 0)),
        pl.BlockSpec((POOL_HALO, BRANCH_W), lambda i: (jnp.minimum((i + 1) * hb, n_hblk - 1), 0)),
        _row_spec(tl, D_MODEL),
        _const_spec((D_MODEL, D_MODEL)),
        _const_spec((4, A_QK)),
        _const_spec((1, LANES)),
        _const_spec((4, LANES, LANES)),
        _const_spec((1, BRANCH_W)),
    ]
    return pl.pallas_call(
        kern,
        grid=(n_tiles,),
        in_specs=in_specs,
        out_specs=_row_spec(tl, D_MODEL),
        out_shape=jax.ShapeDtypeStruct((lp, D_MODEL), F32),
        scratch_shapes=[pltpu.VMEM((tl + 2 * POOL_HALO, BRANCH_W), F32)],
        compiler_params=pltpu.CompilerParams(
            dimension_semantics=("arbitrary",), vmem_limit_bytes=VMEM_LIMIT),
        name="final",
    )(h, oa, ob, oc, du, du, du, sg, lw["wout"], lw["lam"], lw["subln"], lw["wg"], lw["dsc"])


def _layer_weights(i, norm_w, w_in, w_out, a_q_norm, a_k_norm, a_lambda, a_subln, b_q_norm,
                   b_k_norm, c_q_lat_norm, c_kv_lat_norm, c_w_uq, c_w_ukv, c_q_norm, c_k_norm,
                   d_w_group, d_scale):
    win = w_in[i][:, :P_HEAD_END].astype(BF16)
    wtail = w_in[i][:, P_TAIL:].astype(BF16)
    uq = c_w_uq[i].reshape(C_Q_LORA, 4, C_NOPE + C_ROPE)
    uq = jnp.pad(uq, ((0, 0), (0, 0), (0, 2 * LANES - C_NOPE - C_ROPE)))
    wuq = uq.reshape(C_Q_LORA, 4 * 2 * LANES).astype(BF16)
    row = lambda v: v.reshape(1, -1).astype(F32)
    pad_row = lambda v: jnp.pad(v.astype(F32), (0, LANES - v.shape[0])).reshape(1, LANES)
    return dict(
        nw=row(norm_w[i]), win=win, wtail=wtail, wuq=wuq, wukv=c_w_ukv[i].astype(BF16),
        aqn=row(jnp.tile(a_q_norm[i], 2)), akn=row(jnp.tile(a_k_norm[i], 2)),
        bqn=row(b_q_norm[i]), bkn=row(b_k_norm[i]),
        cql=row(c_q_lat_norm[i]), ckvl=row(c_kv_lat_norm[i]),
        cqnn=row(c_q_norm[i][:C_NOPE]), cqnr=pad_row(c_q_norm[i][C_NOPE:]),
        cknn=row(c_k_norm[i][:C_NOPE]), cknr=pad_row(c_k_norm[i][C_NOPE:]),
        wout=w_out[i].astype(BF16), lam=a_lambda[i].astype(F32), subln=row(a_subln[i]),
        wg=d_w_group[i].astype(BF16), dsc=row(d_scale[i]),
    )


def kernel(x, meta_tokens, norm_w, w_in, w_out, a_q_norm, a_k_norm, a_lambda, a_subln, b_q_norm,
           b_k_norm, c_q_lat_norm, c_kv_lat_norm, c_w_uq, c_w_ukv, c_q_norm, c_k_norm,
           d_w_group, d_scale):
    b, n_tok, d = x.shape
    assert b == 1 and d == D_MODEL
    lp = n_tok + SEQ_PAD
    tl = ROW_TILE
    ck = KV_CHUNK
    assert lp % tl == 0 and lp % ATTN_ROWS == 0 and n_tok % (4 * ck) == 0 and n_tok % GRID_W == 0
    depth = norm_w.shape[0]

    h = jnp.concatenate([meta_tokens.astype(x.dtype), x[0],
                         jnp.zeros((lp - N_META - n_tok, d), x.dtype)], axis=0)
    tables = _rope_lane_tables(n_tok, lp)

    for i in range(depth):
        lambda_init = 0.8 - 0.6 * math.exp(-0.3 * i)
        lw = _layer_weights(i, norm_w, w_in, w_out, a_q_norm, a_k_norm, a_lambda, a_subln,
                            b_q_norm, b_k_norm, c_q_lat_norm, c_kv_lat_norm, c_w_uq, c_w_ukv,
                            c_q_norm, c_k_norm, d_w_group, d_scale)
        qa, ka, va, qb, kb, vb, qc, kc, vc, sg, du = _proj_call(h, lw, tables, lp, tl, N_META + n_tok)
        oa = _attn_call(qa, ka, va, "A", lp, n_tok, ATTN_ROWS // 2, ck)
        ob = _attn_call(qb, kb, vb, "B", lp, n_tok, ATTN_ROWS // 2, ck)
        oc = _attn_call(qc, kc, vc, "C", lp, n_tok, ATTN_ROWS, ck)
        h = _final_call(h, oa, ob, oc, du, sg, lw, lambda_init, lp, n_tok, tl)
    return h[N_META:N_META + n_tok][None]
```

```python
import functools
import math

import jax
import jax.numpy as jnp
from jax import lax
from jax.experimental import pallas as pl
from jax.experimental.pallas import tpu as pltpu

F32 = jnp.float32
BF16 = jnp.bfloat16
F8 = jnp.float8_e4m3fn

D_MODEL = 2048
N_META = 16
GRID_W = 64
EPS = 1e-6
ROPE_THETA = 500000.0
AXIAL_THETA = 10000.0
BRANCH_W = 512
LANES = 128
SEQ_PAD = 512
ROW_TILE = 384
ATTN_ROWS = 768
KV_CHUNK = 1024
A_QK = 64
A_ROT = 16
B_HD = 128
C_NOPE = 128
C_ROPE = 64
C_Q_LORA = 384
C_KV_LORA = 256
POOL_WINDOWS = (2, 4, 8, 16)
POOL_HALO = 8
LOG2E = 1.4426950408889634
NEG_BIG = -1e30
VMEM_LIMIT = 56 * 1024 * 1024

P_AQ, P_AV, P_AG = 0, 1024, 1536
P_BQ, P_BV, P_BG = 2048, 2816, 3072
P_CQ = 3584
P_TAIL = 4288
IN_COLS = 5824
P_HEAD_END = P_CQ + C_Q_LORA + C_KV_LORA + LANES


def _mean_sq(x, n):
    return jnp.sum(x * x, axis=-1, keepdims=True) * (1.0 / n)


def _half_mean_sq(x):
    xx = x * x
    lower = lax.broadcasted_iota(jnp.int32, x.shape, 1) < (LANES // 2)
    lo = jnp.sum(jnp.where(lower, xx, 0.0), axis=-1, keepdims=True)
    hi = jnp.sum(jnp.where(lower, 0.0, xx), axis=-1, keepdims=True)
    return jnp.where(lower, lo, hi) * (2.0 / LANES)


def _rope(x, c, s, half):
    lane = lax.broadcasted_iota(jnp.int32, x.shape, 1)
    fwd = pltpu.roll(x, LANES - half, axis=1)
    bwd = pltpu.roll(x, half, axis=1)
    partner = jnp.where((lane % (2 * half)) < half, fwd, bwd)
    return x * c + partner * s


def _proj_tile(h_ref, nw_ref, win_ref, wtail_ref, wuq_ref, wukv_ref,
               aqf_ref, akf_ref, aqn_ref, akn_ref, bqn_ref, bkn_ref, cql_ref, ckvl_ref,
               cqnn_ref, cqnr_ref, cknn_ref, cknr_ref,
               ca_ref, sa_ref, cb_ref, sb_ref, cc_ref, sc_ref,
               qa_ref, ka_ref, va_ref, qb_ref, kb_ref, vb_ref,
               qc_ref, kc_ref, vc_ref, sg_ref, du_ref):
    h = h_ref[...]
    ms = jnp.mean(h * h, axis=-1, keepdims=True)
    hb = (h * lax.rsqrt(ms + EPS) * nw_ref[...]).astype(BF16)

    def mm(lo, hi):
        return jnp.dot(hb, win_ref[:, lo:hi], preferred_element_type=F32)

    def mm_tail(blk):
        return jnp.dot(hb, wtail_ref[:, blk * BRANCH_W:(blk + 1) * BRANCH_W],
                       preferred_element_type=F32)

    ca, sa = ca_ref[...], sa_ref[...]
    cb, sb = cb_ref[...], sb_ref[...]
    cc, sc = cc_ref[...], sc_ref[...]

    pa = mm(P_AQ, P_AV)
    for blk in range(8):
        x = pa[:, blk * LANES:(blk + 1) * LANES]
        w = aqn_ref[...] if blk < 4 else akn_ref[...]
        x = x * lax.rsqrt(_half_mean_sq(x) + EPS) * w
        x = _rope(x, ca, sa, A_ROT // 2)
        if blk < 4:
            qa_ref[:, blk * LANES:(blk + 1) * LANES] = (x * aqf_ref[...]).astype(BF16)
        else:
            ka_ref[:, (blk - 4) * LANES:(blk - 3) * LANES] = (x * akf_ref[...]).astype(BF16)
    va_ref[...] = mm(P_AV, P_AG).astype(BF16)

    pb = mm(P_BQ, P_BV)
    b_scale = (B_HD ** -0.5) * LOG2E
    for blk in range(6):
        x = pb[:, blk * LANES:(blk + 1) * LANES]
        w = bqn_ref[...] if blk < 4 else bkn_ref[...]
        x = x * lax.rsqrt(_mean_sq(x, B_HD) + EPS) * w
        x = _rope(x, cb, sb, B_HD // 4)
        if blk < 4:
            qb_ref[:, blk * LANES:(blk + 1) * LANES] = (x * b_scale).astype(BF16)
        else:
            kb_ref[:, (blk - 4) * LANES:(blk - 3) * LANES] = x.astype(BF16)
    vb_ref[...] = mm(P_BV, P_BG).astype(BF16)

    pc = mm(P_CQ, P_HEAD_END)
    cq = pc[:, 0:C_Q_LORA]
    cq = cq * lax.rsqrt(jnp.mean(cq * cq, axis=-1, keepdims=True) + EPS) * cql_ref[...]
    qfull = jnp.dot(cq.astype(BF16), wuq_ref[...], preferred_element_type=F32)
    ckv = pc[:, C_Q_LORA:C_Q_LORA + C_KV_LORA]
    ckv = ckv * lax.rsqrt(jnp.mean(ckv * ckv, axis=-1, keepdims=True) + EPS) * ckvl_ref[...]
    kvfull = jnp.dot(ckv.astype(BF16), wukv_ref[...], preferred_element_type=F32)
    kr = pc[:, C_Q_LORA + C_KV_LORA:]
    kr = jnp.where(lax.broadcasted_iota(jnp.int32, kr.shape, 1) < C_ROPE, kr, 0.0)
    kr = kr * lax.rsqrt(_mean_sq(kr, C_ROPE) + EPS) * cknr_ref[...]
    kr = _rope(kr, cc, sc, C_ROPE // 2).astype(BF16)
    c_scale = ((C_NOPE + C_ROPE) ** -0.5) * LOG2E
    for hd in range(4):
        base = hd * 2 * LANES
        qn = qfull[:, base:base + LANES]
        qn = qn * lax.rsqrt(_mean_sq(qn, C_NOPE) + EPS) * cqnn_ref[...]
        qr = qfull[:, base + LANES:base + 2 * LANES]
        qr = qr * lax.rsqrt(_mean_sq(qr, C_ROPE) + EPS) * cqnr_ref[...]
        qr = _rope(qr, cc, sc, C_ROPE // 2)
        qc_ref[:, base:base + LANES] = (qn * c_scale).astype(BF16)
        qc_ref[:, base + LANES:base + 2 * LANES] = (qr * c_scale).astype(BF16)
        kn = kvfull[:, base:base + LANES]
        kn = kn * lax.rsqrt(_mean_sq(kn, C_NOPE) + EPS) * cknn_ref[...]
        kc_ref[:, base:base + LANES] = kn.astype(BF16)
        kc_ref[:, base + LANES:base + 2 * LANES] = kr
        vc_ref[:, hd * LANES:(hd + 1) * LANES] = kvfull[:, base + LANES:base + 2 * LANES].astype(BF16)

    gates = [mm(P_AG, P_BQ), mm(P_BG, P_CQ), mm_tail(0), mm_tail(2)]
    for blk, g in enumerate(gates):
        sg_ref[:, blk * BRANCH_W:(blk + 1) * BRANCH_W] = jax.nn.silu(g).astype(BF16)
    du_ref[...] = mm_tail(1)


def _f8_split(x):
    hi = x.astype(F8).astype(F32)
    return hi, (x - hi) * 16.0


def _lane_repeat(x, n):
    return jnp.concatenate([x] * n, axis=1)


def _attn_tile(q_ref, k_ref, o_ref, vx_sc, s_sc, mc_sc, m_sc, acc_sc, k8_sc=None, *,
               mode, tq, ck, n_main, tail_valid):
    if mode == "A":
        qh, ql = _f8_split(q_ref[...].astype(F32))
        lower = lax.broadcasted_iota(jnp.int32, qh.shape, 1) < A_QK
        b0 = jnp.where(lower, qh, pltpu.roll(ql, A_QK, axis=1)).astype(F8)
        b1 = jnp.where(lower, pltpu.roll(qh, A_QK, axis=1), ql).astype(F8)
        q8 = [jnp.concatenate([b, b], axis=1) for b in (b0, b1)]
        rows = 2 * tq
    elif mode == "B":
        qs = jnp.concatenate([q_ref[:, 0:LANES], q_ref[:, LANES:2 * LANES]], axis=0)
        rows = qs.shape[0]
    else:
        qs = q_ref[...]
        rows = qs.shape[0]
    n_grp = rows // tq

    m_sc[...] = jnp.full(m_sc.shape, NEG_BIG, F32)
    acc_sc[...] = jnp.zeros(acc_sc.shape, F32)

    contract_last = (((1,), (1,)), ((), ()))

    def scores(start, width):
        if mode == "A":
            parts = [lax.dot_general(q8[m], k8_sc[pl.ds(start, width), m * 2 * LANES:(m + 1) * 2 * LANES],
                                     contract_last, preferred_element_type=F32) for m in range(2)]
            return jnp.concatenate(parts, axis=0)
        return lax.dot_general(qs, k_ref[pl.ds(start, width), :], contract_last,
                               preferred_element_type=F32)

    def row_max(s):
        return jnp.broadcast_to(jnp.max(s, axis=1, keepdims=True), (rows, LANES))

    def lane_max(s):
        parts = [s[:, t * LANES:(t + 1) * LANES] for t in range(s.shape[1] // LANES)]
        return functools.reduce(jnp.maximum, parts)

    def qk_stage(j, slot):
        s = scores(pl.multiple_of(j * ck, ck), ck)
        s_sc[slot] = s
        mc_sc[slot] = lane_max(s)

    def softmax_pv(s_parts, m_cur, vx):
        m_prev = m_sc[...]
        m_new = jnp.maximum(m_prev, m_cur)
        alpha = jnp.exp2(m_prev - m_new)
        p = jnp.concatenate(
            [jnp.exp2(s - _lane_repeat(m_new, s.shape[1] // LANES)).astype(BF16) for s in s_parts],
            axis=1)
        pv = jnp.dot(p, vx, preferred_element_type=F32)
        acc_sc[...] = _lane_repeat(alpha, 2) * acc_sc[...] + pv
        m_sc[...] = m_new

    def pv_stage(j, slot):
        vx = vx_sc[pl.ds(pl.multiple_of(j * ck, ck), 2 * ck), :]
        softmax_pv([s_sc[slot], s_sc[slot + 1]],
                   row_max(jnp.maximum(mc_sc[slot], mc_sc[slot + 1])), vx)

    qk_stage(0, 0)
    qk_stage(1, 1)

    def body(i, carry):
        c = 4 * i
        qk_stage(c + 2, 2)
        qk_stage(c + 3, 3)
        pv_stage(c, 0)
        qk_stage(c + 4, 0)
        qk_stage(c + 5, 1)
        pv_stage(c + 2, 2)
        return carry

    lax.fori_loop(0, n_main // 4 - 1, body, 0)
    c = n_main - 4
    qk_stage(c + 2, 2)
    qk_stage(c + 3, 3)
    pv_stage(c, 0)
    t0 = n_main * ck
    s_tail = scores(t0, LANES)
    col = lax.broadcasted_iota(jnp.int32, s_tail.shape, 1)
    s_tail = jnp.where(col < tail_valid, s_tail, NEG_BIG)
    pv_stage(c + 2, 2)
    softmax_pv([s_tail], row_max(s_tail), vx_sc[t0:t0 + LANES, :])

    o = acc_sc[:, 0:LANES] / acc_sc[:, LANES:2 * LANES]
    for g in range(n_grp):
        o_ref[g] = o[g * tq:(g + 1) * tq]


def _attn_kernel(q_ref, k_ref, v_ref, o_ref, vx_sc, *scratch, tq, q_valid, **tile_args):
    lp = v_ref.shape[0]

    @pl.when(pl.program_id(1) == 0)
    def _():
        def fill(c, carry):
            r = pl.multiple_of(c * tq, tq)
            vx_sc[pl.ds(r, tq), 0:LANES] = v_ref[pl.ds(r, tq), :]
            vx_sc[pl.ds(r, tq), LANES:2 * LANES] = jnp.ones((tq, LANES), BF16)
            if tile_args["mode"] == "A":
                k8_sc = scratch[-1]
                kh, kl = _f8_split(k_ref[pl.ds(r, tq), :].astype(F32))
                kl = kl * (1.0 / 16.0)
                khr = pltpu.roll(kh, A_QK, axis=1)
                klr = pltpu.roll(kl, A_QK, axis=1)
                lower = lax.broadcasted_iota(jnp.int32, kh.shape, 1) < A_QK
                blocks = [jnp.where(lower, kh, khr * 0.0625), jnp.where(lower, kl, klr * 0.0625),
                          jnp.where(lower, khr, kh * 0.0625), jnp.where(lower, klr, kl * 0.0625)]
                for b, blk in enumerate(blocks):
                    k8_sc[pl.ds(r, tq), b * LANES:(b + 1) * LANES] = blk.astype(F8)
            return carry
        lax.fori_loop(0, lp // tq, fill, 0)

    has_rows = pl.program_id(1) * tq < q_valid

    @pl.when(has_rows)
    def _():
        _attn_tile(q_ref, k_ref, o_ref, vx_sc, *scratch, tq=tq, **tile_args)

    @pl.when(jnp.logical_not(has_rows))
    def _():
        o_ref[...] = jnp.zeros(o_ref.shape, F32)


def _final_tile(h_ref, oa_ref, ob_ref, oc_ref, du_ref, dup_ref, dun_ref, sg_ref,
                wout_ref, lam_ref, subln_ref, wg_ref, dsc_ref, out_ref, ext_sc, *,
                lambda_init, seq_len, tl):
    i = pl.program_id(0)

    lp = lam_ref[...]
    lam = (jnp.exp(jnp.sum(lp[0:1, :] * lp[1:2, :], axis=-1, keepdims=True))
           - jnp.exp(jnp.sum(lp[2:3, :] * lp[3:4, :], axis=-1, keepdims=True)) + lambda_init)

    ys = []
    for hd in range(4):
        d = oa_ref[hd, 0] - lam * oa_ref[hd, 1]
        d = d * lax.rsqrt(_mean_sq(d, LANES) + EPS) * subln_ref[...]
        d = d * (1.0 - lambda_init)
        ys.append(d * sg_ref[:, hd * LANES:(hd + 1) * LANES].astype(F32))
    for g in range(2):
        for r in range(2):
            c = 4 + 2 * g + r
            ys.append(ob_ref[g, r] * sg_ref[:, c * LANES:(c + 1) * LANES].astype(F32))
    for hd in range(4):
        c = 8 + hd
        ys.append(oc_ref[hd, 0] * sg_ref[:, c * LANES:(c + 1) * LANES].astype(F32))

    row0 = i * tl
    def masked(x, first_row):
        r = first_row + lax.broadcasted_iota(jnp.int32, x.shape, 0)
        return jnp.where((r >= 0) & (r < seq_len), x, 0.0)
    ext_sc[0:POOL_HALO, :] = masked(dup_ref[...], row0 - POOL_HALO)
    ext_sc[POOL_HALO:POOL_HALO + tl, :] = masked(du_ref[...], row0)
    ext_sc[POOL_HALO + tl:2 * POOL_HALO + tl, :] = masked(dun_ref[...], row0 + tl)
    t = row0 + lax.broadcasted_iota(jnp.int32, (tl, LANES), 0)
    for g, w in enumerate(POOL_WINDOWS):
        cols = slice(g * LANES, (g + 1) * LANES)
        acc = ext_sc[POOL_HALO - w // 2:POOL_HALO - w // 2 + tl, cols]
        for j in range(-w // 2 + 1, w // 2):
            acc = acc + ext_sc[POOL_HALO + j:POOL_HALO + j + tl, cols]
        cnt = jnp.minimum(t + w // 2, seq_len) - jnp.maximum(t - w // 2, 0)
        cnt = jnp.maximum(cnt, 1).astype(F32)
        pooled = acc / cnt - ext_sc[POOL_HALO:POOL_HALO + tl, cols]
        mixed = jnp.dot(pooled.astype(BF16), wg_ref[g], preferred_element_type=F32)
        c = 12 + g
        ys.append(mixed * dsc_ref[:, cols] * sg_ref[:, c * LANES:(c + 1) * LANES].astype(F32))

    y = jnp.concatenate([v.astype(BF16) for v in ys], axis=1)
    out_ref[...] = h_ref[...] + jnp.dot(y, wout_ref[...], preferred_element_type=F32)


def _rope_lane_tables(n_tok, lp):
    l = N_META + n_tok
    rows = n_tok // GRID_W
    lane = jnp.arange(LANES, dtype=jnp.int32)

    def inv_freq(idx, dim, theta):
        return theta ** (-(2 * idx).astype(F32) / dim)

    def padded(v):
        return jnp.pad(v.astype(F32), (0, lp - l))[:, None]

    pos = padded(jnp.arange(l, dtype=jnp.int32))
    row_pos = padded(jnp.concatenate([jnp.full((N_META,), -1, jnp.int32),
                                      jnp.repeat(jnp.arange(rows, dtype=jnp.int32), GRID_W)]))
    col_pos = padded(jnp.concatenate([jnp.arange(N_META, dtype=jnp.int32),
                                      jnp.tile(jnp.arange(GRID_W, dtype=jnp.int32), rows)]))

    ja = lane % A_QK
    inv_a = jnp.where(ja < A_ROT, inv_freq(ja % (A_ROT // 2), A_ROT, ROPE_THETA), 0.0)
    sign_a = jnp.where(ja < A_ROT // 2, -1.0, 1.0).astype(F32)
    ang_a = pos * inv_a[None, :]
    jb = lane % (B_HD // 2)
    inv_b = inv_freq(jb % (B_HD // 4), B_HD // 2, AXIAL_THETA)
    inv_r = jnp.where(lane < B_HD // 2, inv_b, 0.0)
    inv_c = jnp.where(lane < B_HD // 2, 0.0, inv_b)
    sign_b = jnp.where(jb < B_HD // 4, -1.0, 1.0).astype(F32)
    ang_b = row_pos * inv_r[None, :] + col_pos * inv_c[None, :]
    inv_m = jnp.where(lane < C_ROPE, inv_freq(lane % (C_ROPE // 2), C_ROPE, ROPE_THETA), 0.0)
    sign_m = jnp.where(lane < C_ROPE // 2, -1.0, 1.0).astype(F32)
    ang_m = pos * inv_m[None, :]
    return (jnp.cos(ang_a), jnp.sin(ang_a) * sign_a[None, :],
            jnp.cos(ang_b), jnp.sin(ang_b) * sign_b[None, :],
            jnp.cos(ang_m), jnp.sin(ang_m) * sign_m[None, :])


def _row_tile_kernel(tile_fn, n_out, n_scratch, tl, seq_len):
    def kern(*refs):
        outs = refs[len(refs) - n_scratch - n_out:len(refs) - n_scratch]
        has_rows = pl.program_id(0) * tl < seq_len

        @pl.when(has_rows)
        def _():
            tile_fn(*refs)

        @pl.when(jnp.logical_not(has_rows))
        def _():
            for o in outs:
                o[...] = jnp.zeros(o.shape, o.dtype)
    return kern


def _row_spec(tl, width):
    return pl.BlockSpec((tl, width), lambda i: (i, 0))


def _const_spec(shape):
    nd = len(shape)
    return pl.BlockSpec(shape, lambda i: (0,) * nd, pipeline_mode=pl.Buffered(1))


def _proj_call(h, lw, tables, lp, tl, seq_len):
    n_tiles = lp // tl
    small = [lw["aqf"], lw["akf"], lw["aqn"], lw["akn"], lw["bqn"], lw["bkn"], lw["cql"], lw["ckvl"],
             lw["cqnn"], lw["cqnr"], lw["cknn"], lw["cknr"]]
    in_specs = ([_row_spec(tl, D_MODEL), _const_spec((1, D_MODEL)),
                 _const_spec((D_MODEL, P_HEAD_END)), _const_spec((D_MODEL, IN_COLS - P_TAIL)),
                 _const_spec(lw["wuq"].shape),
                 _const_spec(lw["wukv"].shape)]
                + [_const_spec(a.shape) for a in small]
                + [_row_spec(tl, LANES)] * 6)
    out_widths = [512, 512, 512, 512, 256, 256, 1024, 1024, 512, 2048, 512]
    out_dtypes = [BF16] * 10 + [F32]
    out_shape = [jax.ShapeDtypeStruct((lp, w), dt) for w, dt in zip(out_widths, out_dtypes)]
    out_specs = [_row_spec(tl, w) for w in out_widths]
    return pl.pallas_call(
        _row_tile_kernel(_proj_tile, len(out_widths), 0, tl, seq_len),
        grid=(n_tiles,),
        in_specs=in_specs,
        out_specs=out_specs,
        out_shape=out_shape,
        compiler_params=pltpu.CompilerParams(
            dimension_semantics=("arbitrary",), vmem_limit_bytes=VMEM_LIMIT),
        name="proj",
    )(h, lw["nw"], lw["win"], lw["wtail"], lw["wuq"], lw["wukv"], *small, *tables)


def _attn_call(q, k, v, mode, lp, n_tok, tq, ck):
    n_heads = {"A": 4, "B": 2, "C": 4}[mode]
    n_grp = {"A": 2, "B": 2, "C": 1}[mode]
    qw = {"A": LANES, "B": 2 * LANES, "C": 2 * LANES}[mode]
    kw = {"A": LANES, "B": LANES, "C": 2 * LANES}[mode]
    n_main = n_tok // ck
    tail_valid = N_META + n_tok - n_main * ck
    kern = functools.partial(_attn_kernel, mode=mode, tq=tq, ck=ck, n_main=n_main,
                             tail_valid=tail_valid, q_valid=N_META + n_tok)
    return pl.pallas_call(
        kern,
        grid=(n_heads, lp // tq),
        in_specs=[pl.BlockSpec((tq, qw), lambda hd, i: (i, hd)),
                  pl.BlockSpec((lp, kw), lambda hd, i: (0, hd), pipeline_mode=pl.Buffered(1)),
                  pl.BlockSpec((lp, LANES), lambda hd, i: (0, hd), pipeline_mode=pl.Buffered(1))],
        out_specs=pl.BlockSpec((None, n_grp, tq, LANES), lambda hd, i: (hd, 0, i, 0)),
        out_shape=jax.ShapeDtypeStruct((n_heads, n_grp, lp, LANES), F32),
        scratch_shapes=[pltpu.VMEM((lp, 2 * LANES), BF16),
                        pltpu.VMEM((4, n_grp * tq, ck), F32),
                        pltpu.VMEM((4, n_grp * tq, LANES), F32),
                        pltpu.VMEM((n_grp * tq, LANES), F32),
                        pltpu.VMEM((n_grp * tq, 2 * LANES), F32)]
                       + ([pltpu.VMEM((lp, 4 * LANES), F8)] if mode == "A" else []),
        compiler_params=pltpu.CompilerParams(
            dimension_semantics=("arbitrary", "arbitrary"), vmem_limit_bytes=VMEM_LIMIT),
        name="attn_" + mode,
    )(q, k, v)


def _final_call(h, oa, ob, oc, du, sg, lw, lambda_init, lp, n_tok, tl):
    n_tiles = lp // tl
    hb = tl // POOL_HALO
    n_hblk = lp // POOL_HALO
    seq_len = N_META + n_tok
    tile_fn = functools.partial(_final_tile, lambda_init=lambda_init, seq_len=seq_len, tl=tl)
    kern = _row_tile_kernel(tile_fn, 1, 1, tl, seq_len)
    in_specs = [
        _row_spec(tl, D_MODEL),
        pl.BlockSpec((4, 2, tl, LANES), lambda i: (0, 0, i, 0)),
        pl.BlockSpec((2, 2, tl, LANES), lambda i: (0, 0, i, 0)),
        pl.BlockSpec((4, 1, tl, LANES), lambda i: (0, 0, i, 0)),
        _row_spec(tl, BRANCH_W),
        pl.BlockSpec((POOL_HALO, BRANCH_W), lambda i: (jnp.maximum(i * hb - 1, 0), 0)),
        pl.BlockSpec((POOL_HALO, BRANCH_W), lambda i: (jnp.minimum((i + 1) * hb, n_hblk - 1), 0)),
        _row_spec(tl, D_MODEL),
        _const_spec((D_MODEL, D_MODEL)),
        _const_spec((4, A_QK)),
        _const_spec((1, LANES)),
        _const_spec((4, LANES, LANES)),
        _const_spec((1, BRANCH_W)),
    ]
    return pl.pallas_call(
        kern,
        grid=(n_tiles,),
        in_specs=in_specs,
        out_specs=_row_spec(tl, D_MODEL),
        out_shape=jax.ShapeDtypeStruct((lp, D_MODEL), F32),
        scratch_shapes=[pltpu.VMEM((tl + 2 * POOL_HALO, BRANCH_W), F32)],
        compiler_params=pltpu.CompilerParams(
            dimension_semantics=("arbitrary",), vmem_limit_bytes=VMEM_LIMIT),
        name="final",
    )(h, oa, ob, oc, du, du, du, sg, lw["wout"], lw["lam"], lw["subln"], lw["wg"], lw["dsc"])


def _layer_weights(i, norm_w, w_in, w_out, a_q_norm, a_k_norm, a_lambda, a_subln, b_q_norm,
                   b_k_norm, c_q_lat_norm, c_kv_lat_norm, c_w_uq, c_w_ukv, c_q_norm, c_k_norm,
                   d_w_group, d_scale):
    win = w_in[i][:, :P_HEAD_END].astype(BF16)
    wtail = w_in[i][:, P_TAIL:].astype(BF16)
    uq = c_w_uq[i].reshape(C_Q_LORA, 4, C_NOPE + C_ROPE)
    uq = jnp.pad(uq, ((0, 0), (0, 0), (0, 2 * LANES - C_NOPE - C_ROPE)))
    wuq = uq.reshape(C_Q_LORA, 4 * 2 * LANES).astype(BF16)
    row = lambda v: v.reshape(1, -1).astype(F32)
    a_scale = (A_QK ** -0.5) * LOG2E
    tiny = jnp.finfo(F32).tiny
    q_mag = jnp.maximum(jnp.max(jnp.abs(a_q_norm[i])).astype(F32) * a_scale, tiny)
    k_mag = jnp.maximum(jnp.max(jnp.abs(a_k_norm[i])).astype(F32), tiny)
    shift = jnp.clip(jnp.round(0.5 * (jnp.log2(k_mag) - jnp.log2(q_mag))), -60.0, 60.0)
    aqf = jnp.full((1, LANES), a_scale, F32) * jnp.exp2(shift)
    akf = jnp.full((1, LANES), 1.0, F32) * jnp.exp2(-shift)
    pad_row = lambda v: jnp.pad(v.astype(F32), (0, LANES - v.shape[0])).reshape(1, LANES)
    return dict(
        nw=row(norm_w[i]), win=win, wtail=wtail, wuq=wuq, aqf=aqf, akf=akf, wukv=c_w_ukv[i].astype(BF16),
        aqn=row(jnp.tile(a_q_norm[i], 2)), akn=row(jnp.tile(a_k_norm[i], 2)),
        bqn=row(b_q_norm[i]), bkn=row(b_k_norm[i]),
        cql=row(c_q_lat_norm[i]), ckvl=row(c_kv_lat_norm[i]),
        cqnn=row(c_q_norm[i][:C_NOPE]), cqnr=pad_row(c_q_norm[i][C_NOPE:]),
        cknn=row(c_k_norm[i][:C_NOPE]), cknr=pad_row(c_k_norm[i][C_NOPE:]),
        wout=w_out[i].astype(BF16), lam=a_lambda[i].astype(F32), subln=row(a_subln[i]),
        wg=d_w_group[i].astype(BF16), dsc=row(d_scale[i]),
    )


def kernel(x, meta_tokens, norm_w, w_in, w_out, a_q_norm, a_k_norm, a_lambda, a_subln, b_q_norm,
           b_k_norm, c_q_lat_norm, c_kv_lat_norm, c_w_uq, c_w_ukv, c_q_norm, c_k_norm,
           d_w_group, d_scale):
    b, n_tok, d = x.shape
    assert b == 1 and d == D_MODEL
    lp = n_tok + SEQ_PAD
    tl = ROW_TILE
    ck = KV_CHUNK
    assert lp % tl == 0 and lp % ATTN_ROWS == 0 and n_tok % (4 * ck) == 0 and n_tok % GRID_W == 0
    depth = norm_w.shape[0]

    h = jnp.concatenate([meta_tokens.astype(x.dtype), x[0],
                         jnp.zeros((lp - N_META - n_tok, d), x.dtype)], axis=0)
    tables = _rope_lane_tables(n_tok, lp)

    for i in range(depth):
        lambda_init = 0.8 - 0.6 * math.exp(-0.3 * i)
        lw = _layer_weights(i, norm_w, w_in, w_out, a_q_norm, a_k_norm, a_lambda, a_subln,
                            b_q_norm, b_k_norm, c_q_lat_norm, c_kv_lat_norm, c_w_uq, c_w_ukv,
                            c_q_norm, c_k_norm, d_w_group, d_scale)
        qa, ka, va, qb, kb, vb, qc, kc, vc, sg, du = _proj_call(h, lw, tables, lp, tl, N_META + n_tok)
        oa = _attn_call(qa, ka, va, "A", lp, n_tok, ATTN_ROWS // 2, ck)
        ob = _attn_call(qb, kb, vb, "B", lp, n_tok, ATTN_ROWS // 2, ck)
        oc = _attn_call(qc, kc, vc, "C", lp, n_tok, ATTN_ROWS, ck)
        h = _final_call(h, oa, ob, oc, du, sg, lw, lambda_init, lp, n_tok, tl)
    return h[N_META:N_META + n_tok][None]
```

```python
import functools
import math

import jax
import jax.numpy as jnp
from jax import lax
from jax.experimental import pallas as pl
from jax.experimental.pallas import tpu as pltpu

F32 = jnp.float32
BF16 = jnp.bfloat16
F8 = jnp.float8_e4m3fn

D_MODEL = 2048
N_META = 16
GRID_W = 64
EPS = 1e-6
ROPE_THETA = 500000.0
AXIAL_THETA = 10000.0
BRANCH_W = 512
LANES = 128
SEQ_PAD = 512
ROW_TILE = 384
ATTN_ROWS = 768
KV_CHUNK = 1024
PV_ROW_BLOCKS = 2
A_QK = 64
A_ROT = 16
B_HD = 128
C_NOPE = 128
C_ROPE = 64
C_Q_LORA = 384
C_KV_LORA = 256
POOL_WINDOWS = (2, 4, 8, 16)
POOL_HALO = 8
LOG2E = 1.4426950408889634
NEG_BIG = -1e30
VMEM_LIMIT = 56 * 1024 * 1024

P_AQ, P_AV, P_AG = 0, 1024, 1536
P_BQ, P_BV, P_BG = 2048, 2816, 3072
P_CQ = 3584
P_TAIL = 4288
IN_COLS = 5824
P_HEAD_END = P_CQ + C_Q_LORA + C_KV_LORA + LANES


def _mean_sq(x, n):
    return jnp.sum(x * x, axis=-1, keepdims=True) * (1.0 / n)


def _half_mean_sq(x):
    xx = x * x
    lower = lax.broadcasted_iota(jnp.int32, x.shape, 1) < (LANES // 2)
    lo = jnp.sum(jnp.where(lower, xx, 0.0), axis=-1, keepdims=True)
    hi = jnp.sum(jnp.where(lower, 0.0, xx), axis=-1, keepdims=True)
    return jnp.where(lower, lo, hi) * (2.0 / LANES)


def _rope(x, c, s, half):
    lane = lax.broadcasted_iota(jnp.int32, x.shape, 1)
    fwd = pltpu.roll(x, LANES - half, axis=1)
    bwd = pltpu.roll(x, half, axis=1)
    partner = jnp.where((lane % (2 * half)) < half, fwd, bwd)
    return x * c + partner * s


def _proj_tile(h_ref, nw_ref, win_ref, wtail_ref, wuq_ref, wukv_ref,
               aqf_ref, akf_ref, aqn_ref, akn_ref, bqn_ref, bkn_ref, cql_ref, ckvl_ref,
               cqnn_ref, cqnr_ref, cknn_ref, cknr_ref,
               ca_ref, sa_ref, cb_ref, sb_ref, cc_ref, sc_ref,
               qa_ref, ka_ref, va_ref, qb_ref, kb_ref, vb_ref,
               qc_ref, kc_ref, vc_ref, sg_ref, du_ref):
    h = h_ref[...]
    ms = jnp.mean(h * h, axis=-1, keepdims=True)
    hb = (h * lax.rsqrt(ms + EPS) * nw_ref[...]).astype(BF16)

    def mm(lo, hi):
        return jnp.dot(hb, win_ref[:, lo:hi], preferred_element_type=F32)

    def mm_tail(blk):
        return jnp.dot(hb, wtail_ref[:, blk * BRANCH_W:(blk + 1) * BRANCH_W],
                       preferred_element_type=F32)

    ca, sa = ca_ref[...], sa_ref[...]
    cb, sb = cb_ref[...], sb_ref[...]
    cc, sc = cc_ref[...], sc_ref[...]

    pa = mm(P_AQ, P_AV)
    for blk in range(8):
        x = pa[:, blk * LANES:(blk + 1) * LANES]
        w = aqn_ref[...] if blk < 4 else akn_ref[...]
        x = x * lax.rsqrt(_half_mean_sq(x) + EPS) * w
        x = _rope(x, ca, sa, A_ROT // 2)
        if blk < 4:
            qa_ref[:, blk * LANES:(blk + 1) * LANES] = (x * aqf_ref[...]).astype(BF16)
        else:
            ka_ref[:, (blk - 4) * LANES:(blk - 3) * LANES] = (x * akf_ref[...]).astype(BF16)
    va_ref[...] = mm(P_AV, P_AG).astype(BF16)

    pb = mm(P_BQ, P_BV)
    b_scale = (B_HD ** -0.5) * LOG2E
    for blk in range(6):
        x = pb[:, blk * LANES:(blk + 1) * LANES]
        w = bqn_ref[...] if blk < 4 else bkn_ref[...]
        x = x * lax.rsqrt(_mean_sq(x, B_HD) + EPS) * w
        x = _rope(x, cb, sb, B_HD // 4)
        if blk < 4:
            qb_ref[:, blk * LANES:(blk + 1) * LANES] = (x * b_scale).astype(BF16)
        else:
            kb_ref[:, (blk - 4) * LANES:(blk - 3) * LANES] = x.astype(BF16)
    vb_ref[...] = mm(P_BV, P_BG).astype(BF16)

    pc = mm(P_CQ, P_HEAD_END)
    cq = pc[:, 0:C_Q_LORA]
    cq = cq * lax.rsqrt(jnp.mean(cq * cq, axis=-1, keepdims=True) + EPS) * cql_ref[...]
    qfull = jnp.dot(cq.astype(BF16), wuq_ref[...], preferred_element_type=F32)
    ckv = pc[:, C_Q_LORA:C_Q_LORA + C_KV_LORA]
    ckv = ckv * lax.rsqrt(jnp.mean(ckv * ckv, axis=-1, keepdims=True) + EPS) * ckvl_ref[...]
    kvfull = jnp.dot(ckv.astype(BF16), wukv_ref[...], preferred_element_type=F32)
    kr = pc[:, C_Q_LORA + C_KV_LORA:]
    kr = jnp.where(lax.broadcasted_iota(jnp.int32, kr.shape, 1) < C_ROPE, kr, 0.0)
    kr = kr * lax.rsqrt(_mean_sq(kr, C_ROPE) + EPS) * cknr_ref[...]
    kr = _rope(kr, cc, sc, C_ROPE // 2).astype(BF16)
    c_scale = ((C_NOPE + C_ROPE) ** -0.5) * LOG2E
    for hd in range(4):
        base = hd * 2 * LANES
        qn = qfull[:, base:base + LANES]
        qn = qn * lax.rsqrt(_mean_sq(qn, C_NOPE) + EPS) * cqnn_ref[...]
        qr = qfull[:, base + LANES:base + 2 * LANES]
        qr = qr * lax.rsqrt(_mean_sq(qr, C_ROPE) + EPS) * cqnr_ref[...]
        qr = _rope(qr, cc, sc, C_ROPE // 2)
        qc_ref[:, base:base + LANES] = (qn * c_scale).astype(BF16)
        qc_ref[:, base + LANES:base + 2 * LANES] = (qr * c_scale).astype(BF16)
        kn = kvfull[:, base:base + LANES]
        kn = kn * lax.rsqrt(_mean_sq(kn, C_NOPE) + EPS) * cknn_ref[...]
        kc_ref[:, base:base + LANES] = kn.astype(BF16)
        kc_ref[:, base + LANES:base + 2 * LANES] = kr
        vc_ref[:, hd * LANES:(hd + 1) * LANES] = kvfull[:, base + LANES:base + 2 * LANES].astype(BF16)

    gates = [mm(P_AG, P_BQ), mm(P_BG, P_CQ), mm_tail(0), mm_tail(2)]
    for blk, g in enumerate(gates):
        sg_ref[:, blk * BRANCH_W:(blk + 1) * BRANCH_W] = jax.nn.silu(g).astype(BF16)
    du_ref[...] = mm_tail(1)


def _f8_split(x):
    hi = x.astype(F8).astype(F32)
    return hi, (x - hi) * 16.0


def _lane_repeat(x, n):
    return jnp.concatenate([x] * n, axis=1)


def _attn_tile(q_ref, k_ref, o_ref, vx_sc, s_sc, mc_sc, m_sc, acc_sc, k8_sc=None, *,
               mode, tq, ck, n_main, tail_valid):
    if mode == "A":
        qh, ql = _f8_split(q_ref[...].astype(F32))
        lower = lax.broadcasted_iota(jnp.int32, qh.shape, 1) < A_QK
        b0 = jnp.where(lower, qh, pltpu.roll(ql, A_QK, axis=1)).astype(F8)
        b1 = jnp.where(lower, pltpu.roll(qh, A_QK, axis=1), ql).astype(F8)
        q8 = [jnp.concatenate([b, b], axis=1) for b in (b0, b1)]
        rows = 2 * tq
    elif mode == "B":
        qs = jnp.concatenate([q_ref[:, 0:LANES], q_ref[:, LANES:2 * LANES]], axis=0)
        rows = qs.shape[0]
    else:
        qs = q_ref[...]
        rows = qs.shape[0]
    n_grp = rows // tq

    m_sc[...] = jnp.full(m_sc.shape, NEG_BIG, F32)
    acc_sc[...] = jnp.zeros(acc_sc.shape, F32)

    contract_last = (((1,), (1,)), ((), ()))

    def scores(start, width):
        if mode == "A":
            parts = [lax.dot_general(q8[m], k8_sc[pl.ds(start, width), m * 2 * LANES:(m + 1) * 2 * LANES],
                                     contract_last, preferred_element_type=F32) for m in range(2)]
            return jnp.concatenate(parts, axis=0)
        return lax.dot_general(qs, k_ref[pl.ds(start, width), :], contract_last,
                               preferred_element_type=F32)

    def row_max(s):
        return jnp.broadcast_to(jnp.max(s, axis=1, keepdims=True), (rows, LANES))

    def lane_max(s):
        parts = [s[:, t * LANES:(t + 1) * LANES] for t in range(s.shape[1] // LANES)]
        return functools.reduce(jnp.maximum, parts)

    def qk_stage(j, slot):
        s = scores(pl.multiple_of(j * ck, ck), ck)
        s_sc[slot] = s
        mc_sc[slot] = lane_max(s)

    def softmax_pv(s_parts, m_cur, vx):
        m_prev = m_sc[...]
        m_new = jnp.maximum(m_prev, m_cur)
        alpha = jnp.exp2(m_prev - m_new)
        m_sc[...] = m_new
        blk = rows // PV_ROW_BLOCKS
        for r0 in range(0, rows, blk):
            mr = m_new[r0:r0 + blk]
            p = jnp.concatenate(
                [jnp.exp2(s[r0:r0 + blk] - _lane_repeat(mr, s.shape[1] // LANES)).astype(BF16)
                 for s in s_parts], axis=1)
            pv = jnp.dot(p, vx, preferred_element_type=F32)
            acc_sc[r0:r0 + blk, :] = _lane_repeat(alpha[r0:r0 + blk], 2) * acc_sc[r0:r0 + blk, :] + pv

    def pv_stage(j, slot):
        vx = vx_sc[pl.ds(pl.multiple_of(j * ck, ck), 2 * ck), :]
        softmax_pv([s_sc[slot], s_sc[slot + 1]],
                   row_max(jnp.maximum(mc_sc[slot], mc_sc[slot + 1])), vx)

    qk_stage(0, 0)
    qk_stage(1, 1)

    def body(i, carry):
        c = 4 * i
        qk_stage(c + 2, 2)
        qk_stage(c + 3, 3)
        pv_stage(c, 0)
        qk_stage(c + 4, 0)
        qk_stage(c + 5, 1)
        pv_stage(c + 2, 2)
        return carry

    lax.fori_loop(0, n_main // 4 - 1, body, 0)
    c = n_main - 4
    qk_stage(c + 2, 2)
    qk_stage(c + 3, 3)
    pv_stage(c, 0)
    t0 = n_main * ck
    s_tail = scores(t0, LANES)
    col = lax.broadcasted_iota(jnp.int32, s_tail.shape, 1)
    s_tail = jnp.where(col < tail_valid, s_tail, NEG_BIG)
    pv_stage(c + 2, 2)
    softmax_pv([s_tail], row_max(s_tail), vx_sc[t0:t0 + LANES, :])

    o = acc_sc[:, 0:LANES] / acc_sc[:, LANES:2 * LANES]
    for g in range(n_grp):
        o_ref[g] = o[g * tq:(g + 1) * tq]


def _attn_kernel(q_ref, k_ref, v_ref, o_ref, vx_sc, *scratch, tq, q_valid, **tile_args):
    lp = v_ref.shape[0]

    @pl.when(pl.program_id(1) == 0)
    def _():
        def fill(c, carry):
            r = pl.multiple_of(c * tq, tq)
            vx_sc[pl.ds(r, tq), 0:LANES] = v_ref[pl.ds(r, tq), :]
            vx_sc[pl.ds(r, tq), LANES:2 * LANES] = jnp.ones((tq, LANES), BF16)
            if tile_args["mode"] == "A":
                k8_sc = scratch[-1]
                kh, kl = _f8_split(k_ref[pl.ds(r, tq), :].astype(F32))
                kl = kl * (1.0 / 16.0)
                khr = pltpu.roll(kh, A_QK, axis=1)
                klr = pltpu.roll(kl, A_QK, axis=1)
                lower = lax.broadcasted_iota(jnp.int32, kh.shape, 1) < A_QK
                blocks = [jnp.where(lower, kh, khr * 0.0625), jnp.where(lower, kl, klr * 0.0625),
                          jnp.where(lower, khr, kh * 0.0625), jnp.where(lower, klr, kl * 0.0625)]
                for b, blk in enumerate(blocks):
                    k8_sc[pl.ds(r, tq), b * LANES:(b + 1) * LANES] = blk.astype(F8)
            return carry
        lax.fori_loop(0, lp // tq, fill, 0)

    has_rows = pl.program_id(1) * tq < q_valid

    @pl.when(has_rows)
    def _():
        _attn_tile(q_ref, k_ref, o_ref, vx_sc, *scratch, tq=tq, **tile_args)

    @pl.when(jnp.logical_not(has_rows))
    def _():
        o_ref[...] = jnp.zeros(o_ref.shape, F32)


def _final_tile(h_ref, oa_ref, ob_ref, oc_ref, du_ref, dup_ref, dun_ref, sg_ref,
                wout_ref, lam_ref, subln_ref, wg_ref, dsc_ref, out_ref, ext_sc, *,
                lambda_init, seq_len, tl):
    i = pl.program_id(0)

    lp = lam_ref[...]
    lam = (jnp.exp(jnp.sum(lp[0:1, :] * lp[1:2, :], axis=-1, keepdims=True))
           - jnp.exp(jnp.sum(lp[2:3, :] * lp[3:4, :], axis=-1, keepdims=True)) + lambda_init)

    ys = []
    for hd in range(4):
        d = oa_ref[hd, 0] - lam * oa_ref[hd, 1]
        d = d * lax.rsqrt(_mean_sq(d, LANES) + EPS) * subln_ref[...]
        d = d * (1.0 - lambda_init)
        ys.append(d * sg_ref[:, hd * LANES:(hd + 1) * LANES].astype(F32))
    for g in range(2):
        for r in range(2):
            c = 4 + 2 * g + r
            ys.append(ob_ref[g, r] * sg_ref[:, c * LANES:(c + 1) * LANES].astype(F32))
    for hd in range(4):
        c = 8 + hd
        ys.append(oc_ref[hd, 0] * sg_ref[:, c * LANES:(c + 1) * LANES].astype(F32))

    row0 = i * tl
    def masked(x, first_row):
        r = first_row + lax.broadcasted_iota(jnp.int32, x.shape, 0)
        return jnp.where((r >= 0) & (r < seq_len), x, 0.0)
    ext_sc[0:POOL_HALO, :] = masked(dup_ref[...], row0 - POOL_HALO)
    ext_sc[POOL_HALO:POOL_HALO + tl, :] = masked(du_ref[...], row0)
    ext_sc[POOL_HALO + tl:2 * POOL_HALO + tl, :] = masked(dun_ref[...], row0 + tl)
    t = row0 + lax.broadcasted_iota(jnp.int32, (tl, LANES), 0)
    for g, w in enumerate(POOL_WINDOWS):
        cols = slice(g * LANES, (g + 1) * LANES)
        acc = ext_sc[POOL_HALO - w // 2:POOL_HALO - w // 2 + tl, cols]
        for j in range(-w // 2 + 1, w // 2):
            acc = acc + ext_sc[POOL_HALO + j:POOL_HALO + j + tl, cols]
        cnt = jnp.minimum(t + w // 2, seq_len) - jnp.maximum(t - w // 2, 0)
        cnt = jnp.maximum(cnt, 1).astype(F32)
        pooled = acc / cnt - ext_sc[POOL_HALO:POOL_HALO + tl, cols]
        mixed = jnp.dot(pooled.astype(BF16), wg_ref[g], preferred_element_type=F32)
        c = 12 + g
        ys.append(mixed * dsc_ref[:, cols] * sg_ref[:, c * LANES:(c + 1) * LANES].astype(F32))

    y = jnp.concatenate([v.astype(BF16) for v in ys], axis=1)
    out_ref[...] = h_ref[...] + jnp.dot(y, wout_ref[...], preferred_element_type=F32)


def _rope_lane_tables(n_tok, lp):
    l = N_META + n_tok
    rows = n_tok // GRID_W
    lane = jnp.arange(LANES, dtype=jnp.int32)

    def inv_freq(idx, dim, theta):
        return theta ** (-(2 * idx).astype(F32) / dim)

    def padded(v):
        return jnp.pad(v.astype(F32), (0, lp - l))[:, None]

    pos = padded(jnp.arange(l, dtype=jnp.int32))
    row_pos = padded(jnp.concatenate([jnp.full((N_META,), -1, jnp.int32),
                                      jnp.repeat(jnp.arange(rows, dtype=jnp.int32), GRID_W)]))
    col_pos = padded(jnp.concatenate([jnp.arange(N_META, dtype=jnp.int32),
                                      jnp.tile(jnp.arange(GRID_W, dtype=jnp.int32), rows)]))

    ja = lane % A_QK
    inv_a = jnp.where(ja < A_ROT, inv_freq(ja % (A_ROT // 2), A_ROT, ROPE_THETA), 0.0)
    sign_a = jnp.where(ja < A_ROT // 2, -1.0, 1.0).astype(F32)
    ang_a = pos * inv_a[None, :]
    jb = lane % (B_HD // 2)
    inv_b = inv_freq(jb % (B_HD // 4), B_HD // 2, AXIAL_THETA)
    inv_r = jnp.where(lane < B_HD // 2, inv_b, 0.0)
    inv_c = jnp.where(lane < B_HD // 2, 0.0, inv_b)
    sign_b = jnp.where(jb < B_HD // 4, -1.0, 1.0).astype(F32)
    ang_b = row_pos * inv_r[None, :] + col_pos * inv_c[None, :]
    inv_m = jnp.where(lane < C_ROPE, inv_freq(lane % (C_ROPE // 2), C_ROPE, ROPE_THETA), 0.0)
    sign_m = jnp.where(lane < C_ROPE // 2, -1.0, 1.0).astype(F32)
    ang_m = pos * inv_m[None, :]
    return (jnp.cos(ang_a), jnp.sin(ang_a) * sign_a[None, :],
            jnp.cos(ang_b), jnp.sin(ang_b) * sign_b[None, :],
            jnp.cos(ang_m), jnp.sin(ang_m) * sign_m[None, :])


def _row_tile_kernel(tile_fn, n_out, n_scratch, tl, seq_len):
    def kern(*refs):
        outs = refs[len(refs) - n_scratch - n_out:len(refs) - n_scratch]
        has_rows = pl.program_id(0) * tl < seq_len

        @pl.when(has_rows)
        def _():
            tile_fn(*refs)

        @pl.when(jnp.logical_not(has_rows))
        def _():
            for o in outs:
                o[...] = jnp.zeros(o.shape, o.dtype)
    return kern


def _row_spec(tl, width):
    return pl.BlockSpec((tl, width), lambda i: (i, 0))


def _const_spec(shape):
    nd = len(shape)
    return pl.BlockSpec(shape, lambda i: (0,) * nd, pipeline_mode=pl.Buffered(1))


def _proj_call(h, lw, tables, lp, tl, seq_len):
    n_tiles = lp // tl
    small = [lw["aqf"], lw["akf"], lw["aqn"], lw["akn"], lw["bqn"], lw["bkn"], lw["cql"], lw["ckvl"],
             lw["cqnn"], lw["cqnr"], lw["cknn"], lw["cknr"]]
    in_specs = ([_row_spec(tl, D_MODEL), _const_spec((1, D_MODEL)),
                 _const_spec((D_MODEL, P_HEAD_END)), _const_spec((D_MODEL, IN_COLS - P_TAIL)),
                 _const_spec(lw["wuq"].shape),
                 _const_spec(lw["wukv"].shape)]
                + [_const_spec(a.shape) for a in small]
                + [_row_spec(tl, LANES)] * 6)
    out_widths = [512, 512, 512, 512, 256, 256, 1024, 1024, 512, 2048, 512]
    out_dtypes = [BF16] * 10 + [F32]
    out_shape = [jax.ShapeDtypeStruct((lp, w), dt) for w, dt in zip(out_widths, out_dtypes)]
    out_specs = [_row_spec(tl, w) for w in out_widths]
    return pl.pallas_call(
        _row_tile_kernel(_proj_tile, len(out_widths), 0, tl, seq_len),
        grid=(n_tiles,),
        in_specs=in_specs,
        out_specs=out_specs,
        out_shape=out_shape,
        compiler_params=pltpu.CompilerParams(
            dimension_semantics=("arbitrary",), vmem_limit_bytes=VMEM_LIMIT),
        name="proj",
    )(h, lw["nw"], lw["win"], lw["wtail"], lw["wuq"], lw["wukv"], *small, *tables)


def _attn_call(q, k, v, mode, lp, n_tok, tq, ck):
    n_heads = {"A": 4, "B": 2, "C": 4}[mode]
    n_grp = {"A": 2, "B": 2, "C": 1}[mode]
    qw = {"A": LANES, "B": 2 * LANES, "C": 2 * LANES}[mode]
    kw = {"A": LANES, "B": LANES, "C": 2 * LANES}[mode]
    n_main = n_tok // ck
    tail_valid = N_META + n_tok - n_main * ck
    kern = functools.partial(_attn_kernel, mode=mode, tq=tq, ck=ck, n_main=n_main,
                             tail_valid=tail_valid, q_valid=N_META + n_tok)
    return pl.pallas_call(
        kern,
        grid=(n_heads, lp // tq),
        in_specs=[pl.BlockSpec((tq, qw), lambda hd, i: (i, hd)),
                  pl.BlockSpec((lp, kw), lambda hd, i: (0, hd), pipeline_mode=pl.Buffered(1)),
                  pl.BlockSpec((lp, LANES), lambda hd, i: (0, hd), pipeline_mode=pl.Buffered(1))],
        out_specs=pl.BlockSpec((None, n_grp, tq, LANES), lambda hd, i: (hd, 0, i, 0)),
        out_shape=jax.ShapeDtypeStruct((n_heads, n_grp, lp, LANES), F32),
        scratch_shapes=[pltpu.VMEM((lp, 2 * LANES), BF16),
                        pltpu.VMEM((4, n_grp * tq, ck), F32),
                        pltpu.VMEM((4, n_grp * tq, LANES), F32),
                        pltpu.VMEM((n_grp * tq, LANES), F32),
                        pltpu.VMEM((n_grp * tq, 2 * LANES), F32)]
                       + ([pltpu.VMEM((lp, 4 * LANES), F8)] if mode == "A" else []),
        compiler_params=pltpu.CompilerParams(
            dimension_semantics=("arbitrary", "arbitrary"), vmem_limit_bytes=VMEM_LIMIT),
        name="attn_" + mode,
    )(q, k, v)


def _final_call(h, oa, ob, oc, du, sg, lw, lambda_init, lp, n_tok, tl):
    n_tiles = lp // tl
    hb = tl // POOL_HALO
    n_hblk = lp // POOL_HALO
    seq_len = N_META + n_tok
    tile_fn = functools.partial(_final_tile, lambda_init=lambda_init, seq_len=seq_len, tl=tl)
    kern = _row_tile_kernel(tile_fn, 1, 1, tl, seq_len)
    in_specs = [
        _row_spec(tl, D_MODEL),
        pl.BlockSpec((4, 2, tl, LANES), lambda i: (0, 0, i, 0)),
        pl.BlockSpec((2, 2, tl, LANES), lambda i: (0, 0, i, 0)),
        pl.BlockSpec((4, 1, tl, LANES), lambda i: (0, 0, i, 0)),
        _row_spec(tl, BRANCH_W),
        pl.BlockSpec((POOL_HALO, BRANCH_W), lambda i: (jnp.maximum(i * hb - 1, 0), 0)),
        pl.BlockSpec((POOL_HALO, BRANCH_W), lambda i: (jnp.minimum((i + 1) * hb, n_hblk - 1), 0)),
        _row_spec(tl, D_MODEL),
        _const_spec((D_MODEL, D_MODEL)),
        _const_spec((4, A_QK)),
        _const_spec((1, LANES)),
        _const_spec((4, LANES, LANES)),
        _const_spec((1, BRANCH_W)),
    ]
    return pl.pallas_call(
        kern,
        grid=(n_tiles,),
        in_specs=in_specs,
        out_specs=_row_spec(tl, D_MODEL),
        out_shape=jax.ShapeDtypeStruct((lp, D_MODEL), F32),
        scratch_shapes=[pltpu.VMEM((tl + 2 * POOL_HALO, BRANCH_W), F32)],
        compiler_params=pltpu.CompilerParams(
            dimension_semantics=("arbitrary",), vmem_limit_bytes=VMEM_LIMIT),
        name="final",
    )(h, oa, ob, oc, du, du, du, sg, lw["wout"], lw["lam"], lw["subln"], lw["wg"], lw["dsc"])


def _layer_weights(i, norm_w, w_in, w_out, a_q_norm, a_k_norm, a_lambda, a_subln, b_q_norm,
                   b_k_norm, c_q_lat_norm, c_kv_lat_norm, c_w_uq, c_w_ukv, c_q_norm, c_k_norm,
                   d_w_group, d_scale):
    win = w_in[i][:, :P_HEAD_END].astype(BF16)
    wtail = w_in[i][:, P_TAIL:].astype(BF16)
    uq = c_w_uq[i].reshape(C_Q_LORA, 4, C_NOPE + C_ROPE)
    uq = jnp.pad(uq, ((0, 0), (0, 0), (0, 2 * LANES - C_NOPE - C_ROPE)))
    wuq = uq.reshape(C_Q_LORA, 4 * 2 * LANES).astype(BF16)
    row = lambda v: v.reshape(1, -1).astype(F32)
    a_scale = (A_QK ** -0.5) * LOG2E
    tiny = jnp.finfo(F32).tiny
    q_mag = jnp.maximum(jnp.max(jnp.abs(a_q_norm[i])).astype(F32) * a_scale, tiny)
    k_mag = jnp.maximum(jnp.max(jnp.abs(a_k_norm[i])).astype(F32), tiny)
    shift = jnp.clip(jnp.round(0.5 * (jnp.log2(k_mag) - jnp.log2(q_mag))), -60.0, 60.0)
    aqf = jnp.full((1, LANES), a_scale, F32) * jnp.exp2(shift)
    akf = jnp.full((1, LANES), 1.0, F32) * jnp.exp2(-shift)
    pad_row = lambda v: jnp.pad(v.astype(F32), (0, LANES - v.shape[0])).reshape(1, LANES)
    return dict(
        nw=row(norm_w[i]), win=win, wtail=wtail, wuq=wuq, aqf=aqf, akf=akf, wukv=c_w_ukv[i].astype(BF16),
        aqn=row(jnp.tile(a_q_norm[i], 2)), akn=row(jnp.tile(a_k_norm[i], 2)),
        bqn=row(b_q_norm[i]), bkn=row(b_k_norm[i]),
        cql=row(c_q_lat_norm[i]), ckvl=row(c_kv_lat_norm[i]),
        cqnn=row(c_q_norm[i][:C_NOPE]), cqnr=pad_row(c_q_norm[i][C_NOPE:]),
        cknn=row(c_k_norm[i][:C_NOPE]), cknr=pad_row(c_k_norm[i][C_NOPE:]),
        wout=w_out[i].astype(BF16), lam=a_lambda[i].astype(F32), subln=row(a_subln[i]),
        wg=d_w_group[i].astype(BF16), dsc=row(d_scale[i]),
    )


def kernel(x, meta_tokens, norm_w, w_in, w_out, a_q_norm, a_k_norm, a_lambda, a_subln, b_q_norm,
           b_k_norm, c_q_lat_norm, c_kv_lat_norm, c_w_uq, c_w_ukv, c_q_norm, c_k_norm,
           d_w_group, d_scale):
    b, n_tok, d = x.shape
    assert b == 1 and d == D_MODEL
    lp = n_tok + SEQ_PAD
    tl = ROW_TILE
    ck = KV_CHUNK
    assert lp % tl == 0 and lp % ATTN_ROWS == 0 and n_tok % (4 * ck) == 0 and n_tok % GRID_W == 0
    depth = norm_w.shape[0]

    h = jnp.concatenate([meta_tokens.astype(x.dtype), x[0],
                         jnp.zeros((lp - N_META - n_tok, d), x.dtype)], axis=0)
    tables = _rope_lane_tables(n_tok, lp)

    for i in range(depth):
        lambda_init = 0.8 - 0.6 * math.exp(-0.3 * i)
        lw = _layer_weights(i, norm_w, w_in, w_out, a_q_norm, a_k_norm, a_lambda, a_subln,
                            b_q_norm, b_k_norm, c_q_lat_norm, c_kv_lat_norm, c_w_uq, c_w_ukv,
                            c_q_norm, c_k_norm, d_w_group, d_scale)
        qa, ka, va, qb, kb, vb, qc, kc, vc, sg, du = _proj_call(h, lw, tables, lp, tl, N_META + n_tok)
        oa = _attn_call(qa, ka, va, "A", lp, n_tok, ATTN_ROWS // 2, ck)
        ob = _attn_call(qb, kb, vb, "B", lp, n_tok, ATTN_ROWS // 2, ck)
        oc = _attn_call(qc, kc, vc, "C", lp, n_tok, ATTN_ROWS, ck)
        h = _final_call(h, oa, ob, oc, du, sg, lw, lambda_init, lp, n_tok, tl)
    return h[N_META:N_META + n_tok][None]
```

```python
import functools
import math

import jax
import jax.numpy as jnp
from jax import lax
from jax.experimental import pallas as pl
from jax.experimental.pallas import tpu as pltpu

F32 = jnp.float32
BF16 = jnp.bfloat16
F8 = jnp.float8_e4m3fn

D_MODEL = 2048
N_META = 16
GRID_W = 64
EPS = 1e-6
ROPE_THETA = 500000.0
AXIAL_THETA = 10000.0
BRANCH_W = 512
LANES = 128
SEQ_PAD = 512
ROW_TILE = 384
ATTN_ROWS = 768
KV_CHUNK = 1024
PV_ROW_BLOCKS = 2
A_QK = 64
A_ROT = 16
B_HD = 128
C_NOPE = 128
C_ROPE = 64
C_Q_LORA = 384
C_KV_LORA = 256
POOL_WINDOWS = (2, 4, 8, 16)
POOL_HALO = 8
LOG2E = 1.4426950408889634
NEG_BIG = -1e30
VMEM_LIMIT = 56 * 1024 * 1024

P_AQ, P_AV, P_AG = 0, 1024, 1536
P_BQ, P_BV, P_BG = 2048, 2816, 3072
P_CQ = 3584
P_TAIL = 4288
IN_COLS = 5824
P_HEAD_END = P_CQ + C_Q_LORA + C_KV_LORA + LANES


def _mean_sq(x, n):
    return jnp.sum(x * x, axis=-1, keepdims=True) * (1.0 / n)


def _half_mean_sq(x):
    xx = x * x
    lower = lax.broadcasted_iota(jnp.int32, x.shape, 1) < (LANES // 2)
    lo = jnp.sum(jnp.where(lower, xx, 0.0), axis=-1, keepdims=True)
    hi = jnp.sum(jnp.where(lower, 0.0, xx), axis=-1, keepdims=True)
    return jnp.where(lower, lo, hi) * (2.0 / LANES)


def _rope(x, c, s, half):
    lane = lax.broadcasted_iota(jnp.int32, x.shape, 1)
    fwd = pltpu.roll(x, LANES - half, axis=1)
    bwd = pltpu.roll(x, half, axis=1)
    partner = jnp.where((lane % (2 * half)) < half, fwd, bwd)
    return x * c + partner * s


def _proj_tile(h_ref, nw_ref, win_ref, wtail_ref, wuq_ref, wukv_ref,
               aqf_ref, akf_ref, aqn_ref, akn_ref, bqn_ref, bkn_ref, cql_ref, ckvl_ref,
               cqnn_ref, cqnr_ref, cknn_ref, cknr_ref,
               ca_ref, sa_ref, cb_ref, sb_ref, cc_ref, sc_ref,
               qa_ref, ka_ref, va_ref, qb_ref, kb_ref, vb_ref,
               qc_ref, kc_ref, vc_ref, sg_ref, du_ref):
    h = h_ref[...]
    ms = jnp.mean(h * h, axis=-1, keepdims=True)
    hb = (h * lax.rsqrt(ms + EPS) * nw_ref[...]).astype(BF16)

    def mm(lo, hi):
        return jnp.dot(hb, win_ref[:, lo:hi], preferred_element_type=F32)

    def mm_tail(blk):
        return jnp.dot(hb, wtail_ref[:, blk * BRANCH_W:(blk + 1) * BRANCH_W],
                       preferred_element_type=F32)

    ca, sa = ca_ref[...], sa_ref[...]
    cb, sb = cb_ref[...], sb_ref[...]
    cc, sc = cc_ref[...], sc_ref[...]

    pa = mm(P_AQ, P_AV)
    for blk in range(8):
        x = pa[:, blk * LANES:(blk + 1) * LANES]
        w = aqn_ref[...] if blk < 4 else akn_ref[...]
        x = x * lax.rsqrt(_half_mean_sq(x) + EPS) * w
        x = _rope(x, ca, sa, A_ROT // 2)
        if blk < 4:
            qa_ref[:, blk * LANES:(blk + 1) * LANES] = (x * aqf_ref[...]).astype(BF16)
        else:
            ka_ref[:, (blk - 4) * LANES:(blk - 3) * LANES] = (x * akf_ref[...]).astype(BF16)
    va_ref[...] = mm(P_AV, P_AG).astype(BF16)

    pb = mm(P_BQ, P_BV)
    b_scale = (B_HD ** -0.5) * LOG2E
    for blk in range(6):
        x = pb[:, blk * LANES:(blk + 1) * LANES]
        w = bqn_ref[...] if blk < 4 else bkn_ref[...]
        x = x * lax.rsqrt(_mean_sq(x, B_HD) + EPS) * w
        x = _rope(x, cb, sb, B_HD // 4)
        if blk < 4:
            qb_ref[:, blk * LANES:(blk + 1) * LANES] = (x * b_scale).astype(BF16)
        else:
            kb_ref[:, (blk - 4) * LANES:(blk - 3) * LANES] = x.astype(BF16)
    vb_ref[...] = mm(P_BV, P_BG).astype(BF16)

    pc = mm(P_CQ, P_HEAD_END)
    cq = pc[:, 0:C_Q_LORA]
    cq = cq * lax.rsqrt(jnp.mean(cq * cq, axis=-1, keepdims=True) + EPS) * cql_ref[...]
    qfull = jnp.dot(cq.astype(BF16), wuq_ref[...], preferred_element_type=F32)
    ckv = pc[:, C_Q_LORA:C_Q_LORA + C_KV_LORA]
    ckv = ckv * lax.rsqrt(jnp.mean(ckv * ckv, axis=-1, keepdims=True) + EPS) * ckvl_ref[...]
    kvfull = jnp.dot(ckv.astype(BF16), wukv_ref[...], preferred_element_type=F32)
    kr = pc[:, C_Q_LORA + C_KV_LORA:]
    kr = jnp.where(lax.broadcasted_iota(jnp.int32, kr.shape, 1) < C_ROPE, kr, 0.0)
    kr = kr * lax.rsqrt(_mean_sq(kr, C_ROPE) + EPS) * cknr_ref[...]
    kr = _rope(kr, cc, sc, C_ROPE // 2).astype(BF16)
    c_scale = ((C_NOPE + C_ROPE) ** -0.5) * LOG2E
    for hd in range(4):
        base = hd * 2 * LANES
        qn = qfull[:, base:base + LANES]
        qn = qn * lax.rsqrt(_mean_sq(qn, C_NOPE) + EPS) * cqnn_ref[...]
        qr = qfull[:, base + LANES:base + 2 * LANES]
        qr = qr * lax.rsqrt(_mean_sq(qr, C_ROPE) + EPS) * cqnr_ref[...]
        qr = _rope(qr, cc, sc, C_ROPE // 2)
        qc_ref[:, base:base + LANES] = (qn * c_scale).astype(BF16)
        qc_ref[:, base + LANES:base + 2 * LANES] = (qr * c_scale).astype(BF16)
        kn = kvfull[:, base:base + LANES]
        kn = kn * lax.rsqrt(_mean_sq(kn, C_NOPE) + EPS) * cknn_ref[...]
        kc_ref[:, base:base + LANES] = kn.astype(BF16)
        kc_ref[:, base + LANES:base + 2 * LANES] = kr
        vc_ref[:, hd * LANES:(hd + 1) * LANES] = kvfull[:, base + LANES:base + 2 * LANES].astype(BF16)

    gates = [mm(P_AG, P_BQ), mm(P_BG, P_CQ), mm_tail(0), mm_tail(2)]
    for blk, g in enumerate(gates):
        sg_ref[:, blk * BRANCH_W:(blk + 1) * BRANCH_W] = jax.nn.silu(g).astype(BF16)
    du_ref[...] = mm_tail(1)


def _f8_split(x):
    hi = x.astype(F8).astype(F32)
    return hi, (x - hi) * 16.0


def _lane_repeat(x, n):
    return jnp.concatenate([x] * n, axis=1)


def _attn_tile(q_ref, k_ref, o_ref, vx_sc, s_sc, mc_sc, mr_sc, m_sc, acc_sc, k8_sc=None, *,
               mode, tq, ck, n_main, tail_valid):
    if mode == "A":
        qh, ql = _f8_split(q_ref[...].astype(F32))
        lower = lax.broadcasted_iota(jnp.int32, qh.shape, 1) < A_QK
        b0 = jnp.where(lower, qh, pltpu.roll(ql, A_QK, axis=1)).astype(F8)
        b1 = jnp.where(lower, pltpu.roll(qh, A_QK, axis=1), ql).astype(F8)
        q8 = [jnp.concatenate([b, b], axis=1) for b in (b0, b1)]
        rows = 2 * tq
    elif mode == "B":
        qs = jnp.concatenate([q_ref[:, 0:LANES], q_ref[:, LANES:2 * LANES]], axis=0)
        rows = qs.shape[0]
    else:
        qs = q_ref[...]
        rows = qs.shape[0]
    n_grp = rows // tq

    m_sc[...] = jnp.full(m_sc.shape, NEG_BIG, F32)
    acc_sc[...] = jnp.zeros(acc_sc.shape, F32)

    contract_last = (((1,), (1,)), ((), ()))

    def scores(start, width):
        if mode == "A":
            parts = [lax.dot_general(q8[m], k8_sc[pl.ds(start, width), m * 2 * LANES:(m + 1) * 2 * LANES],
                                     contract_last, preferred_element_type=F32) for m in range(2)]
            return jnp.concatenate(parts, axis=0)
        return lax.dot_general(qs, k_ref[pl.ds(start, width), :], contract_last,
                               preferred_element_type=F32)

    def row_max(s):
        return jnp.broadcast_to(jnp.max(s, axis=1, keepdims=True), (rows, LANES))

    def lane_max(s):
        parts = [s[:, t * LANES:(t + 1) * LANES] for t in range(s.shape[1] // LANES)]
        return functools.reduce(jnp.maximum, parts)

    def qk_stage(j, slot):
        s = scores(pl.multiple_of(j * ck, ck), ck)
        s_sc[slot] = s
        mc_sc[slot] = lane_max(s)

    def softmax_pv(s_parts, m_cur, vx):
        m_prev = m_sc[...]
        m_new = jnp.maximum(m_prev, m_cur)
        alpha = jnp.exp2(m_prev - m_new)
        m_sc[...] = m_new
        blk = rows // PV_ROW_BLOCKS
        for r0 in range(0, rows, blk):
            mr = m_new[r0:r0 + blk]
            p = jnp.concatenate(
                [jnp.exp2(s[r0:r0 + blk] - _lane_repeat(mr, s.shape[1] // LANES)).astype(BF16)
                 for s in s_parts], axis=1)
            pv = jnp.dot(p, vx, preferred_element_type=F32)
            acc_sc[r0:r0 + blk, :] = _lane_repeat(alpha[r0:r0 + blk], 2) * acc_sc[r0:r0 + blk, :] + pv

    def qk_pair(p, r):
        qk_stage(2 * p, 2 * r)
        qk_stage(2 * p + 1, 2 * r + 1)

    def reduce_pair(r):
        mr_sc[r] = row_max(jnp.maximum(mc_sc[2 * r], mc_sc[2 * r + 1]))

    def pv_pair(p, r):
        vx = vx_sc[pl.ds(pl.multiple_of(p * 2 * ck, 2 * ck), 2 * ck), :]
        softmax_pv([s_sc[2 * r], s_sc[2 * r + 1]], mr_sc[r], vx)

    n_pairs = n_main // 2
    qk_pair(0, 0)
    qk_pair(1, 1)
    reduce_pair(0)

    def body(i, carry):
        for r in range(3):
            p = 3 * i + r
            reduce_pair((r + 1) % 3)
            pv_pair(p, r)
            qk_pair(p + 2, (r + 2) % 3)
        return carry

    lax.fori_loop(0, (n_pairs - 2) // 3, body, 0)
    t0 = n_main * ck
    s_tail = scores(t0, LANES)
    col = lax.broadcasted_iota(jnp.int32, s_tail.shape, 1)
    s_tail = jnp.where(col < tail_valid, s_tail, NEG_BIG)
    r = (n_pairs - 2) % 3
    reduce_pair((r + 1) % 3)
    pv_pair(n_pairs - 2, r)
    pv_pair(n_pairs - 1, (r + 1) % 3)
    softmax_pv([s_tail], row_max(s_tail), vx_sc[t0:t0 + LANES, :])

    o = acc_sc[:, 0:LANES] / acc_sc[:, LANES:2 * LANES]
    for g in range(n_grp):
        o_ref[g] = o[g * tq:(g + 1) * tq]


def _attn_kernel(q_ref, k_ref, v_ref, o_ref, vx_sc, *scratch, tq, q_valid, **tile_args):
    lp = v_ref.shape[0]

    @pl.when(pl.program_id(1) == 0)
    def _():
        def fill(c, carry):
            r = pl.multiple_of(c * tq, tq)
            vx_sc[pl.ds(r, tq), 0:LANES] = v_ref[pl.ds(r, tq), :]
            vx_sc[pl.ds(r, tq), LANES:2 * LANES] = jnp.ones((tq, LANES), BF16)
            if tile_args["mode"] == "A":
                k8_sc = scratch[-1]
                kh, kl = _f8_split(k_ref[pl.ds(r, tq), :].astype(F32))
                kl = kl * (1.0 / 16.0)
                khr = pltpu.roll(kh, A_QK, axis=1)
                klr = pltpu.roll(kl, A_QK, axis=1)
                lower = lax.broadcasted_iota(jnp.int32, kh.shape, 1) < A_QK
                blocks = [jnp.where(lower, kh, khr * 0.0625), jnp.where(lower, kl, klr * 0.0625),
                          jnp.where(lower, khr, kh * 0.0625), jnp.where(lower, klr, kl * 0.0625)]
                for b, blk in enumerate(blocks):
                    k8_sc[pl.ds(r, tq), b * LANES:(b + 1) * LANES] = blk.astype(F8)
            return carry
        lax.fori_loop(0, lp // tq, fill, 0)

    has_rows = pl.program_id(1) * tq < q_valid

    @pl.when(has_rows)
    def _():
        _attn_tile(q_ref, k_ref, o_ref, vx_sc, *scratch, tq=tq, **tile_args)

    @pl.when(jnp.logical_not(has_rows))
    def _():
        o_ref[...] = jnp.zeros(o_ref.shape, F32)


def _final_tile(h_ref, oa_ref, ob_ref, oc_ref, du_ref, dup_ref, dun_ref, sg_ref,
                wout_ref, lam_ref, subln_ref, wg_ref, dsc_ref, out_ref, ext_sc, *,
                lambda_init, seq_len, tl):
    i = pl.program_id(0)

    lp = lam_ref[...]
    lam = (jnp.exp(jnp.sum(lp[0:1, :] * lp[1:2, :], axis=-1, keepdims=True))
           - jnp.exp(jnp.sum(lp[2:3, :] * lp[3:4, :], axis=-1, keepdims=True)) + lambda_init)

    ys = []
    for hd in range(4):
        d = oa_ref[hd, 0] - lam * oa_ref[hd, 1]
        d = d * lax.rsqrt(_mean_sq(d, LANES) + EPS) * subln_ref[...]
        d = d * (1.0 - lambda_init)
        ys.append(d * sg_ref[:, hd * LANES:(hd + 1) * LANES].astype(F32))
    for g in range(2):
        for r in range(2):
            c = 4 + 2 * g + r
            ys.append(ob_ref[g, r] * sg_ref[:, c * LANES:(c + 1) * LANES].astype(F32))
    for hd in range(4):
        c = 8 + hd
        ys.append(oc_ref[hd, 0] * sg_ref[:, c * LANES:(c + 1) * LANES].astype(F32))

    row0 = i * tl
    def masked(x, first_row):
        r = first_row + lax.broadcasted_iota(jnp.int32, x.shape, 0)
        return jnp.where((r >= 0) & (r < seq_len), x, 0.0)
    ext_sc[0:POOL_HALO, :] = masked(dup_ref[...], row0 - POOL_HALO)
    ext_sc[POOL_HALO:POOL_HALO + tl, :] = masked(du_ref[...], row0)
    ext_sc[POOL_HALO + tl:2 * POOL_HALO + tl, :] = masked(dun_ref[...], row0 + tl)
    t = row0 + lax.broadcasted_iota(jnp.int32, (tl, LANES), 0)
    for g, w in enumerate(POOL_WINDOWS):
        cols = slice(g * LANES, (g + 1) * LANES)
        acc = ext_sc[POOL_HALO - w // 2:POOL_HALO - w // 2 + tl, cols]
        for j in range(-w // 2 + 1, w // 2):
            acc = acc + ext_sc[POOL_HALO + j:POOL_HALO + j + tl, cols]
        cnt = jnp.minimum(t + w // 2, seq_len) - jnp.maximum(t - w // 2, 0)
        cnt = jnp.maximum(cnt, 1).astype(F32)
        pooled = acc / cnt - ext_sc[POOL_HALO:POOL_HALO + tl, cols]
        mixed = jnp.dot(pooled.astype(BF16), wg_ref[g], preferred_element_type=F32)
        c = 12 + g
        ys.append(mixed * dsc_ref[:, cols] * sg_ref[:, c * LANES:(c + 1) * LANES].astype(F32))

    y = jnp.concatenate([v.astype(BF16) for v in ys], axis=1)
    out_ref[...] = h_ref[...] + jnp.dot(y, wout_ref[...], preferred_element_type=F32)


def _rope_lane_tables(n_tok, lp):
    l = N_META + n_tok
    rows = n_tok // GRID_W
    lane = jnp.arange(LANES, dtype=jnp.int32)

    def inv_freq(idx, dim, theta):
        return theta ** (-(2 * idx).astype(F32) / dim)

    def padded(v):
        return jnp.pad(v.astype(F32), (0, lp - l))[:, None]

    pos = padded(jnp.arange(l, dtype=jnp.int32))
    row_pos = padded(jnp.concatenate([jnp.full((N_META,), -1, jnp.int32),
                                      jnp.repeat(jnp.arange(rows, dtype=jnp.int32), GRID_W)]))
    col_pos = padded(jnp.concatenate([jnp.arange(N_META, dtype=jnp.int32),
                                      jnp.tile(jnp.arange(GRID_W, dtype=jnp.int32), rows)]))

    ja = lane % A_QK
    inv_a = jnp.where(ja < A_ROT, inv_freq(ja % (A_ROT // 2), A_ROT, ROPE_THETA), 0.0)
    sign_a = jnp.where(ja < A_ROT // 2, -1.0, 1.0).astype(F32)
    ang_a = pos * inv_a[None, :]
    jb = lane % (B_HD // 2)
    inv_b = inv_freq(jb % (B_HD // 4), B_HD // 2, AXIAL_THETA)
    inv_r = jnp.where(lane < B_HD // 2, inv_b, 0.0)
    inv_c = jnp.where(lane < B_HD // 2, 0.0, inv_b)
    sign_b = jnp.where(jb < B_HD // 4, -1.0, 1.0).astype(F32)
    ang_b = row_pos * inv_r[None, :] + col_pos * inv_c[None, :]
    inv_m = jnp.where(lane < C_ROPE, inv_freq(lane % (C_ROPE // 2), C_ROPE, ROPE_THETA), 0.0)
    sign_m = jnp.where(lane < C_ROPE // 2, -1.0, 1.0).astype(F32)
    ang_m = pos * inv_m[None, :]
    return (jnp.cos(ang_a), jnp.sin(ang_a) * sign_a[None, :],
            jnp.cos(ang_b), jnp.sin(ang_b) * sign_b[None, :],
            jnp.cos(ang_m), jnp.sin(ang_m) * sign_m[None, :])


def _row_tile_kernel(tile_fn, n_out, n_scratch, tl, seq_len):
    def kern(*refs):
        outs = refs[len(refs) - n_scratch - n_out:len(refs) - n_scratch]
        has_rows = pl.program_id(0) * tl < seq_len

        @pl.when(has_rows)
        def _():
            tile_fn(*refs)

        @pl.when(jnp.logical_not(has_rows))
        def _():
            for o in outs:
                o[...] = jnp.zeros(o.shape, o.dtype)
    return kern


def _row_spec(tl, width):
    return pl.BlockSpec((tl, width), lambda i: (i, 0))


def _const_spec(shape):
    nd = len(shape)
    return pl.BlockSpec(shape, lambda i: (0,) * nd, pipeline_mode=pl.Buffered(1))


def _proj_call(h, lw, tables, lp, tl, seq_len):
    n_tiles = lp // tl
    small = [lw["aqf"], lw["akf"], lw["aqn"], lw["akn"], lw["bqn"], lw["bkn"], lw["cql"], lw["ckvl"],
             lw["cqnn"], lw["cqnr"], lw["cknn"], lw["cknr"]]
    in_specs = ([_row_spec(tl, D_MODEL), _const_spec((1, D_MODEL)),
                 _const_spec((D_MODEL, P_HEAD_END)), _const_spec((D_MODEL, IN_COLS - P_TAIL)),
                 _const_spec(lw["wuq"].shape),
                 _const_spec(lw["wukv"].shape)]
                + [_const_spec(a.shape) for a in small]
                + [_row_spec(tl, LANES)] * 6)
    out_widths = [512, 512, 512, 512, 256, 256, 1024, 1024, 512, 2048, 512]
    out_dtypes = [BF16] * 10 + [F32]
    out_shape = [jax.ShapeDtypeStruct((lp, w), dt) for w, dt in zip(out_widths, out_dtypes)]
    out_specs = [_row_spec(tl, w) for w in out_widths]
    return pl.pallas_call(
        _row_tile_kernel(_proj_tile, len(out_widths), 0, tl, seq_len),
        grid=(n_tiles,),
        in_specs=in_specs,
        out_specs=out_specs,
        out_shape=out_shape,
        compiler_params=pltpu.CompilerParams(
            dimension_semantics=("arbitrary",), vmem_limit_bytes=VMEM_LIMIT),
        name="proj",
    )(h, lw["nw"], lw["win"], lw["wtail"], lw["wuq"], lw["wukv"], *small, *tables)


def _attn_call(q, k, v, mode, lp, n_tok, tq, ck):
    n_heads = {"A": 4, "B": 2, "C": 4}[mode]
    n_grp = {"A": 2, "B": 2, "C": 1}[mode]
    qw = {"A": LANES, "B": 2 * LANES, "C": 2 * LANES}[mode]
    kw = {"A": LANES, "B": LANES, "C": 2 * LANES}[mode]
    n_main = n_tok // ck
    tail_valid = N_META + n_tok - n_main * ck
    kern = functools.partial(_attn_kernel, mode=mode, tq=tq, ck=ck, n_main=n_main,
                             tail_valid=tail_valid, q_valid=N_META + n_tok)
    return pl.pallas_call(
        kern,
        grid=(n_heads, lp // tq),
        in_specs=[pl.BlockSpec((tq, qw), lambda hd, i: (i, hd)),
                  pl.BlockSpec((lp, kw), lambda hd, i: (0, hd), pipeline_mode=pl.Buffered(1)),
                  pl.BlockSpec((lp, LANES), lambda hd, i: (0, hd), pipeline_mode=pl.Buffered(1))],
        out_specs=pl.BlockSpec((None, n_grp, tq, LANES), lambda hd, i: (hd, 0, i, 0)),
        out_shape=jax.ShapeDtypeStruct((n_heads, n_grp, lp, LANES), F32),
        scratch_shapes=[pltpu.VMEM((lp, 2 * LANES), BF16),
                        pltpu.VMEM((6, n_grp * tq, ck), F32),
                        pltpu.VMEM((6, n_grp * tq, LANES), F32),
                        pltpu.VMEM((3, n_grp * tq, LANES), F32),
                        pltpu.VMEM((n_grp * tq, LANES), F32),
                        pltpu.VMEM((n_grp * tq, 2 * LANES), F32)]
                       + ([pltpu.VMEM((lp, 4 * LANES), F8)] if mode == "A" else []),
        compiler_params=pltpu.CompilerParams(
            dimension_semantics=("arbitrary", "arbitrary"), vmem_limit_bytes=VMEM_LIMIT),
        name="attn_" + mode,
    )(q, k, v)


def _final_call(h, oa, ob, oc, du, sg, lw, lambda_init, lp, n_tok, tl):
    n_tiles = lp // tl
    hb = tl // POOL_HALO
    n_hblk = lp // POOL_HALO
    seq_len = N_META + n_tok
    tile_fn = functools.partial(_final_tile, lambda_init=lambda_init, seq_len=seq_len, tl=tl)
    kern = _row_tile_kernel(tile_fn, 1, 1, tl, seq_len)
    in_specs = [
        _row_spec(tl, D_MODEL),
        pl.BlockSpec((4, 2, tl, LANES), lambda i: (0, 0, i, 0)),
        pl.BlockSpec((2, 2, tl, LANES), lambda i: (0, 0, i, 0)),
        pl.BlockSpec((4, 1, tl, LANES), lambda i: (0, 0, i, 0)),
        _row_spec(tl, BRANCH_W),
        pl.BlockSpec((POOL_HALO, BRANCH_W), lambda i: (jnp.maximum(i * hb - 1, 0), 0)),
        pl.BlockSpec((POOL_HALO, BRANCH_W), lambda i: (jnp.minimum((i + 1) * hb, n_hblk - 1), 0)),
        _row_spec(tl, D_MODEL),
        _const_spec((D_MODEL, D_MODEL)),
        _const_spec((4, A_QK)),
        _const_spec((1, LANES)),
        _const_spec((4, LANES, LANES)),
        _const_spec((1, BRANCH_W)),
    ]
    return pl.pallas_call(
        kern,
        grid=(n_tiles,),
        in_specs=in_specs,
        out_specs=_row_spec(tl, D_MODEL),
        out_shape=jax.ShapeDtypeStruct((lp, D_MODEL), F32),
        scratch_shapes=[pltpu.VMEM((tl + 2 * POOL_HALO, BRANCH_W), F32)],
        compiler_params=pltpu.CompilerParams(
            dimension_semantics=("arbitrary",), vmem_limit_bytes=VMEM_LIMIT),
        name="final",
    )(h, oa, ob, oc, du, du, du, sg, lw["wout"], lw["lam"], lw["subln"], lw["wg"], lw["dsc"])


def _layer_weights(i, norm_w, w_in, w_out, a_q_norm, a_k_norm, a_lambda, a_subln, b_q_norm,
                   b_k_norm, c_q_lat_norm, c_kv_lat_norm, c_w_uq, c_w_ukv, c_q_norm, c_k_norm,
                   d_w_group, d_scale):
    win = w_in[i][:, :P_HEAD_END].astype(BF16)
    wtail = w_in[i][:, P_TAIL:].astype(BF16)
    uq = c_w_uq[i].reshape(C_Q_LORA, 4, C_NOPE + C_ROPE)
    uq = jnp.pad(uq, ((0, 0), (0, 0), (0, 2 * LANES - C_NOPE - C_ROPE)))
    wuq = uq.reshape(C_Q_LORA, 4 * 2 * LANES).astype(BF16)
    row = lambda v: v.reshape(1, -1).astype(F32)
    a_scale = (A_QK ** -0.5) * LOG2E
    tiny = jnp.finfo(F32).tiny
    q_mag = jnp.maximum(jnp.max(jnp.abs(a_q_norm[i])).astype(F32) * a_scale, tiny)
    k_mag = jnp.maximum(jnp.max(jnp.abs(a_k_norm[i])).astype(F32), tiny)
    shift = jnp.clip(jnp.round(0.5 * (jnp.log2(k_mag) - jnp.log2(q_mag))), -60.0, 60.0)
    aqf = jnp.full((1, LANES), a_scale, F32) * jnp.exp2(shift)
    akf = jnp.full((1, LANES), 1.0, F32) * jnp.exp2(-shift)
    pad_row = lambda v: jnp.pad(v.astype(F32), (0, LANES - v.shape[0])).reshape(1, LANES)
    return dict(
        nw=row(norm_w[i]), win=win, wtail=wtail, wuq=wuq, aqf=aqf, akf=akf, wukv=c_w_ukv[i].astype(BF16),
        aqn=row(jnp.tile(a_q_norm[i], 2)), akn=row(jnp.tile(a_k_norm[i], 2)),
        bqn=row(b_q_norm[i]), bkn=row(b_k_norm[i]),
        cql=row(c_q_lat_norm[i]), ckvl=row(c_kv_lat_norm[i]),
        cqnn=row(c_q_norm[i][:C_NOPE]), cqnr=pad_row(c_q_norm[i][C_NOPE:]),
        cknn=row(c_k_norm[i][:C_NOPE]), cknr=pad_row(c_k_norm[i][C_NOPE:]),
        wout=w_out[i].astype(BF16), lam=a_lambda[i].astype(F32), subln=row(a_subln[i]),
        wg=d_w_group[i].astype(BF16), dsc=row(d_scale[i]),
    )


def kernel(x, meta_tokens, norm_w, w_in, w_out, a_q_norm, a_k_norm, a_lambda, a_subln, b_q_norm,
           b_k_norm, c_q_lat_norm, c_kv_lat_norm, c_w_uq, c_w_ukv, c_q_norm, c_k_norm,
           d_w_group, d_scale):
    b, n_tok, d = x.shape
    assert b == 1 and d == D_MODEL
    lp = n_tok + SEQ_PAD
    tl = ROW_TILE
    ck = KV_CHUNK
    assert lp % tl == 0 and lp % ATTN_ROWS == 0 and n_tok % GRID_W == 0
    assert n_tok % (2 * ck) == 0 and (n_tok // (2 * ck) - 2) % 3 == 0
    depth = norm_w.shape[0]

    h = jnp.concatenate([meta_tokens.astype(x.dtype), x[0],
                         jnp.zeros((lp - N_META - n_tok, d), x.dtype)], axis=0)
    tables = _rope_lane_tables(n_tok, lp)

    for i in range(depth):
        lambda_init = 0.8 - 0.6 * math.exp(-0.3 * i)
        lw = _layer_weights(i, norm_w, w_in, w_out, a_q_norm, a_k_norm, a_lambda, a_subln,
                            b_q_norm, b_k_norm, c_q_lat_norm, c_kv_lat_norm, c_w_uq, c_w_ukv,
                            c_q_norm, c_k_norm, d_w_group, d_scale)
        qa, ka, va, qb, kb, vb, qc, kc, vc, sg, du = _proj_call(h, lw, tables, lp, tl, N_META + n_tok)
        oa = _attn_call(qa, ka, va, "A", lp, n_tok, ATTN_ROWS // 2, ck)
        ob = _attn_call(qb, kb, vb, "B", lp, n_tok, ATTN_ROWS // 2, ck)
        oc = _attn_call(qc, kc, vc, "C", lp, n_tok, ATTN_ROWS, ck)
        h = _final_call(h, oa, ob, oc, du, sg, lw, lambda_init, lp, n_tok, tl)
    return h[N_META:N_META + n_tok][None]
```

```python
import functools
import math

import jax
import jax.numpy as jnp
from jax import lax
from jax.experimental import pallas as pl
from jax.experimental.pallas import tpu as pltpu

F32 = jnp.float32
BF16 = jnp.bfloat16
F8 = jnp.float8_e4m3fn

D_MODEL = 2048
N_META = 16
GRID_W = 64
EPS = 1e-6
ROPE_THETA = 500000.0
AXIAL_THETA = 10000.0
BRANCH_W = 512
LANES = 128
SEQ_PAD = 512
ROW_TILE = 384
ATTN_ROWS = 768
KV_CHUNK = 1024
PV_ROW_BLOCKS = 2
A_QK = 64
A_ROT = 16
B_HD = 128
C_NOPE = 128
C_ROPE = 64
C_Q_LORA = 384
C_KV_LORA = 256
POOL_WINDOWS = (2, 4, 8, 16)
POOL_HALO = 8
LOG2E = 1.4426950408889634
NEG_BIG = -1e30
VMEM_LIMIT = 56 * 1024 * 1024

P_AQ, P_AV, P_AG = 0, 1024, 1536
P_BQ, P_BV, P_BG = 2048, 2816, 3072
P_CQ = 3584
P_TAIL = 4288
IN_COLS = 5824
P_HEAD_END = P_CQ + C_Q_LORA + C_KV_LORA + LANES


def _mean_sq(x, n):
    return jnp.sum(x * x, axis=-1, keepdims=True) * (1.0 / n)


def _half_mean_sq(x):
    xx = x * x
    lower = lax.broadcasted_iota(jnp.int32, x.shape, 1) < (LANES // 2)
    lo = jnp.sum(jnp.where(lower, xx, 0.0), axis=-1, keepdims=True)
    hi = jnp.sum(jnp.where(lower, 0.0, xx), axis=-1, keepdims=True)
    return jnp.where(lower, lo, hi) * (2.0 / LANES)


def _rope(x, c, s, half):
    lane = lax.broadcasted_iota(jnp.int32, x.shape, 1)
    fwd = pltpu.roll(x, LANES - half, axis=1)
    bwd = pltpu.roll(x, half, axis=1)
    partner = jnp.where((lane % (2 * half)) < half, fwd, bwd)
    return x * c + partner * s


def _proj_tile(h_ref, nw_ref, win_ref, wtail_ref, wuq_ref, wukv_ref,
               aqf_ref, akf_ref, aqn_ref, akn_ref, bqn_ref, bkn_ref, cql_ref, ckvl_ref,
               cqnn_ref, cqnr_ref, cknn_ref, cknr_ref,
               ca_ref, sa_ref, cb_ref, sb_ref, cc_ref, sc_ref,
               qa_ref, ka_ref, va_ref, qb_ref, kb_ref, vb_ref,
               qc_ref, kc_ref, vc_ref, sg_ref, du_ref):
    h = h_ref[...]
    ms = jnp.mean(h * h, axis=-1, keepdims=True)
    hb = (h * lax.rsqrt(ms + EPS) * nw_ref[...]).astype(BF16)

    def mm(lo, hi):
        return jnp.dot(hb, win_ref[:, lo:hi], preferred_element_type=F32)

    def mm_tail(blk):
        return jnp.dot(hb, wtail_ref[:, blk * BRANCH_W:(blk + 1) * BRANCH_W],
                       preferred_element_type=F32)

    ca, sa = ca_ref[...], sa_ref[...]
    cb, sb = cb_ref[...], sb_ref[...]
    cc, sc = cc_ref[...], sc_ref[...]

    pa = mm(P_AQ, P_AV)
    for blk in range(8):
        x = pa[:, blk * LANES:(blk + 1) * LANES]
        w = aqn_ref[...] if blk < 4 else akn_ref[...]
        x = x * lax.rsqrt(_half_mean_sq(x) + EPS) * w
        x = _rope(x, ca, sa, A_ROT // 2)
        if blk < 4:
            qa_ref[:, blk * LANES:(blk + 1) * LANES] = (x * aqf_ref[...]).astype(BF16)
        else:
            ka_ref[:, (blk - 4) * LANES:(blk - 3) * LANES] = (x * akf_ref[...]).astype(BF16)
    va_ref[...] = mm(P_AV, P_AG).astype(BF16)

    pb = mm(P_BQ, P_BV)
    b_scale = (B_HD ** -0.5) * LOG2E
    for blk in range(6):
        x = pb[:, blk * LANES:(blk + 1) * LANES]
        w = bqn_ref[...] if blk < 4 else bkn_ref[...]
        x = x * lax.rsqrt(_mean_sq(x, B_HD) + EPS) * w
        x = _rope(x, cb, sb, B_HD // 4)
        if blk < 4:
            qb_ref[:, blk * LANES:(blk + 1) * LANES] = (x * b_scale).astype(BF16)
        else:
            kb_ref[:, (blk - 4) * LANES:(blk - 3) * LANES] = x.astype(BF16)
    vb_ref[...] = mm(P_BV, P_BG).astype(BF16)

    pc = mm(P_CQ, P_HEAD_END)
    cq = pc[:, 0:C_Q_LORA]
    cq = cq * lax.rsqrt(jnp.mean(cq * cq, axis=-1, keepdims=True) + EPS) * cql_ref[...]
    qfull = jnp.dot(cq.astype(BF16), wuq_ref[...], preferred_element_type=F32)
    ckv = pc[:, C_Q_LORA:C_Q_LORA + C_KV_LORA]
    ckv = ckv * lax.rsqrt(jnp.mean(ckv * ckv, axis=-1, keepdims=True) + EPS) * ckvl_ref[...]
    kvfull = jnp.dot(ckv.astype(BF16), wukv_ref[...], preferred_element_type=F32)
    kr = pc[:, C_Q_LORA + C_KV_LORA:]
    kr = jnp.where(lax.broadcasted_iota(jnp.int32, kr.shape, 1) < C_ROPE, kr, 0.0)
    kr = kr * lax.rsqrt(_mean_sq(kr, C_ROPE) + EPS) * cknr_ref[...]
    kr = _rope(kr, cc, sc, C_ROPE // 2).astype(BF16)
    c_scale = ((C_NOPE + C_ROPE) ** -0.5) * LOG2E
    for hd in range(4):
        base = hd * 2 * LANES
        qn = qfull[:, base:base + LANES]
        qn = qn * lax.rsqrt(_mean_sq(qn, C_NOPE) + EPS) * cqnn_ref[...]
        qr = qfull[:, base + LANES:base + 2 * LANES]
        qr = qr * lax.rsqrt(_mean_sq(qr, C_ROPE) + EPS) * cqnr_ref[...]
        qr = _rope(qr, cc, sc, C_ROPE // 2)
        qc_ref[:, base:base + LANES] = (qn * c_scale).astype(BF16)
        qc_ref[:, base + LANES:base + 2 * LANES] = (qr * c_scale).astype(BF16)
        kn = kvfull[:, base:base + LANES]
        kn = kn * lax.rsqrt(_mean_sq(kn, C_NOPE) + EPS) * cknn_ref[...]
        kc_ref[:, base:base + LANES] = kn.astype(BF16)
        kc_ref[:, base + LANES:base + 2 * LANES] = kr
        vc_ref[:, hd * LANES:(hd + 1) * LANES] = kvfull[:, base + LANES:base + 2 * LANES].astype(BF16)

    gates = [mm(P_AG, P_BQ), mm(P_BG, P_CQ), mm_tail(0), mm_tail(2)]
    for blk, g in enumerate(gates):
        sg_ref[:, blk * BRANCH_W:(blk + 1) * BRANCH_W] = jax.nn.silu(g).astype(BF16)
    du_ref[...] = mm_tail(1)


def _f8_split(x):
    hi = x.astype(F8).astype(F32)
    return hi, (x - hi) * 16.0


def _lane_repeat(x, n):
    return jnp.concatenate([x] * n, axis=1)


def _attn_tile(q_ref, k_ref, o_ref, vx_sc, s_sc, mc_sc, mr_sc, m_sc, acc_sc, k8_sc=None, *,
               mode, tq, ck, n_main, tail_valid):
    if mode == "A":
        qh, ql = _f8_split(q_ref[...].astype(F32))
        lower = lax.broadcasted_iota(jnp.int32, qh.shape, 1) < A_QK
        b0 = jnp.where(lower, qh, pltpu.roll(ql, A_QK, axis=1)).astype(F8)
        b1 = jnp.where(lower, pltpu.roll(qh, A_QK, axis=1), ql).astype(F8)
        q8 = [jnp.concatenate([b, b], axis=1) for b in (b0, b1)]
        rows = 2 * tq
    elif mode == "B":
        qs = jnp.concatenate([q_ref[:, 0:LANES], q_ref[:, LANES:2 * LANES]], axis=0)
        rows = qs.shape[0]
    else:
        qs = q_ref[...]
        rows = qs.shape[0]
    n_grp = rows // tq

    m_sc[...] = jnp.full(m_sc.shape, NEG_BIG, F32)
    acc_sc[...] = jnp.zeros(acc_sc.shape, F32)

    contract_last = (((1,), (1,)), ((), ()))

    def scores(start, width):
        if mode == "A":
            parts = [lax.dot_general(q8[m], k8_sc[pl.ds(start, width), m * 2 * LANES:(m + 1) * 2 * LANES],
                                     contract_last, preferred_element_type=F32) for m in range(2)]
            return jnp.concatenate(parts, axis=0)
        return lax.dot_general(qs, k_ref[pl.ds(start, width), :], contract_last,
                               preferred_element_type=F32)

    def row_max(s):
        return jnp.broadcast_to(jnp.max(s, axis=1, keepdims=True), (rows, LANES))

    def lane_max(s):
        parts = [s[:, t * LANES:(t + 1) * LANES] for t in range(s.shape[1] // LANES)]
        return functools.reduce(jnp.maximum, parts)

    def qk_stage(j, slot):
        s = scores(pl.multiple_of(j * ck, ck), ck)
        s_sc[slot] = s
        mc_sc[slot] = lane_max(s)

    def softmax_pv(s_parts, m_cur, vx, after_block=()):
        m_prev = m_sc[...]
        m_new = jnp.maximum(m_prev, m_cur)
        alpha = jnp.exp2(m_prev - m_new)
        m_sc[...] = m_new
        blk = rows // PV_ROW_BLOCKS
        for r0 in range(0, rows, blk):
            mr = m_new[r0:r0 + blk]
            p = jnp.concatenate(
                [jnp.exp2(s[r0:r0 + blk] - _lane_repeat(mr, s.shape[1] // LANES)).astype(BF16)
                 for s in s_parts], axis=1)
            pv = jnp.dot(p, vx, preferred_element_type=F32)
            acc_sc[r0:r0 + blk, :] = _lane_repeat(alpha[r0:r0 + blk], 2) * acc_sc[r0:r0 + blk, :] + pv
            if r0 // blk < len(after_block):
                after_block[r0 // blk]()

    def qk_pair(p, r):
        qk_stage(2 * p, 2 * r)
        qk_stage(2 * p + 1, 2 * r + 1)

    def reduce_pair(r):
        mr_sc[r] = row_max(jnp.maximum(mc_sc[2 * r], mc_sc[2 * r + 1]))

    def pv_pair(p, r, after_block=()):
        vx = vx_sc[pl.ds(pl.multiple_of(p * 2 * ck, 2 * ck), 2 * ck), :]
        softmax_pv([s_sc[2 * r], s_sc[2 * r + 1]], mr_sc[r], vx, after_block)

    n_pairs = n_main // 2
    qk_pair(0, 0)
    qk_pair(1, 1)
    reduce_pair(0)

    def body(i, carry):
        for r in range(3):
            p = 3 * i + r
            reduce_pair((r + 1) % 3)
            nxt = (r + 2) % 3
            if mode == "A":
                pv_pair(p, r, [functools.partial(qk_stage, 2 * (p + 2), 2 * nxt),
                               functools.partial(qk_stage, 2 * (p + 2) + 1, 2 * nxt + 1)])
            else:
                pv_pair(p, r)
                qk_pair(p + 2, nxt)
        return carry

    lax.fori_loop(0, (n_pairs - 2) // 3, body, 0)
    t0 = n_main * ck
    s_tail = scores(t0, LANES)
    col = lax.broadcasted_iota(jnp.int32, s_tail.shape, 1)
    s_tail = jnp.where(col < tail_valid, s_tail, NEG_BIG)
    r = (n_pairs - 2) % 3
    reduce_pair((r + 1) % 3)
    pv_pair(n_pairs - 2, r)
    pv_pair(n_pairs - 1, (r + 1) % 3)
    softmax_pv([s_tail], row_max(s_tail), vx_sc[t0:t0 + LANES, :])

    o = acc_sc[:, 0:LANES] / acc_sc[:, LANES:2 * LANES]
    for g in range(n_grp):
        o_ref[g] = o[g * tq:(g + 1) * tq]


def _attn_kernel(q_ref, k_ref, v_ref, o_ref, vx_sc, *scratch, tq, q_valid, **tile_args):
    lp = v_ref.shape[0]

    @pl.when(pl.program_id(1) == 0)
    def _():
        def fill(c, carry):
            r = pl.multiple_of(c * tq, tq)
            vx_sc[pl.ds(r, tq), 0:LANES] = v_ref[pl.ds(r, tq), :]
            vx_sc[pl.ds(r, tq), LANES:2 * LANES] = jnp.ones((tq, LANES), BF16)
            if tile_args["mode"] == "A":
                k8_sc = scratch[-1]
                kh, kl = _f8_split(k_ref[pl.ds(r, tq), :].astype(F32))
                kl = kl * (1.0 / 16.0)
                khr = pltpu.roll(kh, A_QK, axis=1)
                klr = pltpu.roll(kl, A_QK, axis=1)
                lower = lax.broadcasted_iota(jnp.int32, kh.shape, 1) < A_QK
                blocks = [jnp.where(lower, kh, khr * 0.0625), jnp.where(lower, kl, klr * 0.0625),
                          jnp.where(lower, khr, kh * 0.0625), jnp.where(lower, klr, kl * 0.0625)]
                for b, blk in enumerate(blocks):
                    k8_sc[pl.ds(r, tq), b * LANES:(b + 1) * LANES] = blk.astype(F8)
            return carry
        lax.fori_loop(0, lp // tq, fill, 0)

    has_rows = pl.program_id(1) * tq < q_valid

    @pl.when(has_rows)
    def _():
        _attn_tile(q_ref, k_ref, o_ref, vx_sc, *scratch, tq=tq, **tile_args)

    @pl.when(jnp.logical_not(has_rows))
    def _():
        o_ref[...] = jnp.zeros(o_ref.shape, F32)


def _final_tile(h_ref, oa_ref, ob_ref, oc_ref, du_ref, dup_ref, dun_ref, sg_ref,
                wout_ref, lam_ref, subln_ref, wg_ref, dsc_ref, out_ref, ext_sc, *,
                lambda_init, seq_len, tl):
    i = pl.program_id(0)

    lp = lam_ref[...]
    lam = (jnp.exp(jnp.sum(lp[0:1, :] * lp[1:2, :], axis=-1, keepdims=True))
           - jnp.exp(jnp.sum(lp[2:3, :] * lp[3:4, :], axis=-1, keepdims=True)) + lambda_init)

    ys = []
    for hd in range(4):
        d = oa_ref[hd, 0] - lam * oa_ref[hd, 1]
        d = d * lax.rsqrt(_mean_sq(d, LANES) + EPS) * subln_ref[...]
        d = d * (1.0 - lambda_init)
        ys.append(d * sg_ref[:, hd * LANES:(hd + 1) * LANES].astype(F32))
    for g in range(2):
        for r in range(2):
            c = 4 + 2 * g + r
            ys.append(ob_ref[g, r] * sg_ref[:, c * LANES:(c + 1) * LANES].astype(F32))
    for hd in range(4):
        c = 8 + hd
        ys.append(oc_ref[hd, 0] * sg_ref[:, c * LANES:(c + 1) * LANES].astype(F32))

    row0 = i * tl
    def masked(x, first_row):
        r = first_row + lax.broadcasted_iota(jnp.int32, x.shape, 0)
        return jnp.where((r >= 0) & (r < seq_len), x, 0.0)
    ext_sc[0:POOL_HALO, :] = masked(dup_ref[...], row0 - POOL_HALO)
    ext_sc[POOL_HALO:POOL_HALO + tl, :] = masked(du_ref[...], row0)
    ext_sc[POOL_HALO + tl:2 * POOL_HALO + tl, :] = masked(dun_ref[...], row0 + tl)
    t = row0 + lax.broadcasted_iota(jnp.int32, (tl, LANES), 0)
    for g, w in enumerate(POOL_WINDOWS):
        cols = slice(g * LANES, (g + 1) * LANES)
        acc = ext_sc[POOL_HALO - w // 2:POOL_HALO - w // 2 + tl, cols]
        for j in range(-w // 2 + 1, w // 2):
            acc = acc + ext_sc[POOL_HALO + j:POOL_HALO + j + tl, cols]
        cnt = jnp.minimum(t + w // 2, seq_len) - jnp.maximum(t - w // 2, 0)
        cnt = jnp.maximum(cnt, 1).astype(F32)
        pooled = acc / cnt - ext_sc[POOL_HALO:POOL_HALO + tl, cols]
        mixed = jnp.dot(pooled.astype(BF16), wg_ref[g], preferred_element_type=F32)
        c = 12 + g
        ys.append(mixed * dsc_ref[:, cols] * sg_ref[:, c * LANES:(c + 1) * LANES].astype(F32))

    y = jnp.concatenate([v.astype(BF16) for v in ys], axis=1)
    out_ref[...] = h_ref[...] + jnp.dot(y, wout_ref[...], preferred_element_type=F32)


def _rope_lane_tables(n_tok, lp):
    l = N_META + n_tok
    rows = n_tok // GRID_W
    lane = jnp.arange(LANES, dtype=jnp.int32)

    def inv_freq(idx, dim, theta):
        return theta ** (-(2 * idx).astype(F32) / dim)

    def padded(v):
        return jnp.pad(v.astype(F32), (0, lp - l))[:, None]

    pos = padded(jnp.arange(l, dtype=jnp.int32))
    row_pos = padded(jnp.concatenate([jnp.full((N_META,), -1, jnp.int32),
                                      jnp.repeat(jnp.arange(rows, dtype=jnp.int32), GRID_W)]))
    col_pos = padded(jnp.concatenate([jnp.arange(N_META, dtype=jnp.int32),
                                      jnp.tile(jnp.arange(GRID_W, dtype=jnp.int32), rows)]))

    ja = lane % A_QK
    inv_a = jnp.where(ja < A_ROT, inv_freq(ja % (A_ROT // 2), A_ROT, ROPE_THETA), 0.0)
    sign_a = jnp.where(ja < A_ROT // 2, -1.0, 1.0).astype(F32)
    ang_a = pos * inv_a[None, :]
    jb = lane % (B_HD // 2)
    inv_b = inv_freq(jb % (B_HD // 4), B_HD // 2, AXIAL_THETA)
    inv_r = jnp.where(lane < B_HD // 2, inv_b, 0.0)
    inv_c = jnp.where(lane < B_HD // 2, 0.0, inv_b)
    sign_b = jnp.where(jb < B_HD // 4, -1.0, 1.0).astype(F32)
    ang_b = row_pos * inv_r[None, :] + col_pos * inv_c[None, :]
    inv_m = jnp.where(lane < C_ROPE, inv_freq(lane % (C_ROPE // 2), C_ROPE, ROPE_THETA), 0.0)
    sign_m = jnp.where(lane < C_ROPE // 2, -1.0, 1.0).astype(F32)
    ang_m = pos * inv_m[None, :]
    return (jnp.cos(ang_a), jnp.sin(ang_a) * sign_a[None, :],
            jnp.cos(ang_b), jnp.sin(ang_b) * sign_b[None, :],
            jnp.cos(ang_m), jnp.sin(ang_m) * sign_m[None, :])


def _row_tile_kernel(tile_fn, n_out, n_scratch, tl, seq_len):
    def kern(*refs):
        outs = refs[len(refs) - n_scratch - n_out:len(refs) - n_scratch]
        has_rows = pl.program_id(0) * tl < seq_len

        @pl.when(has_rows)
        def _():
            tile_fn(*refs)

        @pl.when(jnp.logical_not(has_rows))
        def _():
            for o in outs:
                o[...] = jnp.zeros(o.shape, o.dtype)
    return kern


def _row_spec(tl, width):
    return pl.BlockSpec((tl, width), lambda i: (i, 0))


def _const_spec(shape):
    nd = len(shape)
    return pl.BlockSpec(shape, lambda i: (0,) * nd, pipeline_mode=pl.Buffered(1))


def _proj_call(h, lw, tables, lp, tl, seq_len):
    n_tiles = lp // tl
    small = [lw["aqf"], lw["akf"], lw["aqn"], lw["akn"], lw["bqn"], lw["bkn"], lw["cql"], lw["ckvl"],
             lw["cqnn"], lw["cqnr"], lw["cknn"], lw["cknr"]]
    in_specs = ([_row_spec(tl, D_MODEL), _const_spec((1, D_MODEL)),
                 _const_spec((D_MODEL, P_HEAD_END)), _const_spec((D_MODEL, IN_COLS - P_TAIL)),
                 _const_spec(lw["wuq"].shape),
                 _const_spec(lw["wukv"].shape)]
                + [_const_spec(a.shape) for a in small]
                + [_row_spec(tl, LANES)] * 6)
    out_widths = [512, 512, 512, 512, 256, 256, 1024, 1024, 512, 2048, 512]
    out_dtypes = [BF16] * 10 + [F32]
    out_shape = [jax.ShapeDtypeStruct((lp, w), dt) for w, dt in zip(out_widths, out_dtypes)]
    out_specs = [_row_spec(tl, w) for w in out_widths]
    return pl.pallas_call(
        _row_tile_kernel(_proj_tile, len(out_widths), 0, tl, seq_len),
        grid=(n_tiles,),
        in_specs=in_specs,
        out_specs=out_specs,
        out_shape=out_shape,
        compiler_params=pltpu.CompilerParams(
            dimension_semantics=("arbitrary",), vmem_limit_bytes=VMEM_LIMIT),
        name="proj",
    )(h, lw["nw"], lw["win"], lw["wtail"], lw["wuq"], lw["wukv"], *small, *tables)


def _attn_call(q, k, v, mode, lp, n_tok, tq, ck):
    n_heads = {"A": 4, "B": 2, "C": 4}[mode]
    n_grp = {"A": 2, "B": 2, "C": 1}[mode]
    qw = {"A": LANES, "B": 2 * LANES, "C": 2 * LANES}[mode]
    kw = {"A": LANES, "B": LANES, "C": 2 * LANES}[mode]
    n_main = n_tok // ck
    tail_valid = N_META + n_tok - n_main * ck
    kern = functools.partial(_attn_kernel, mode=mode, tq=tq, ck=ck, n_main=n_main,
                             tail_valid=tail_valid, q_valid=N_META + n_tok)
    return pl.pallas_call(
        kern,
        grid=(n_heads, lp // tq),
        in_specs=[pl.BlockSpec((tq, qw), lambda hd, i: (i, hd)),
                  pl.BlockSpec((lp, kw), lambda hd, i: (0, hd), pipeline_mode=pl.Buffered(1)),
                  pl.BlockSpec((lp, LANES), lambda hd, i: (0, hd), pipeline_mode=pl.Buffered(1))],
        out_specs=pl.BlockSpec((None, n_grp, tq, LANES), lambda hd, i: (hd, 0, i, 0)),
        out_shape=jax.ShapeDtypeStruct((n_heads, n_grp, lp, LANES), F32),
        scratch_shapes=[pltpu.VMEM((lp, 2 * LANES), BF16),
                        pltpu.VMEM((6, n_grp * tq, ck), F32),
                        pltpu.VMEM((6, n_grp * tq, LANES), F32),
                        pltpu.VMEM((3, n_grp * tq, LANES), F32),
                        pltpu.VMEM((n_grp * tq, LANES), F32),
                        pltpu.VMEM((n_grp * tq, 2 * LANES), F32)]
                       + ([pltpu.VMEM((lp, 4 * LANES), F8)] if mode == "A" else []),
        compiler_params=pltpu.CompilerParams(
            dimension_semantics=("arbitrary", "arbitrary"), vmem_limit_bytes=VMEM_LIMIT),
        name="attn_" + mode,
    )(q, k, v)


def _final_call(h, oa, ob, oc, du, sg, lw, lambda_init, lp, n_tok, tl):
    n_tiles = lp // tl
    hb = tl // POOL_HALO
    n_hblk = lp // POOL_HALO
    seq_len = N_META + n_tok
    tile_fn = functools.partial(_final_tile, lambda_init=lambda_init, seq_len=seq_len, tl=tl)
    kern = _row_tile_kernel(tile_fn, 1, 1, tl, seq_len)
    in_specs = [
        _row_spec(tl, D_MODEL),
        pl.BlockSpec((4, 2, tl, LANES), lambda i: (0, 0, i, 0)),
        pl.BlockSpec((2, 2, tl, LANES), lambda i: (0, 0, i, 0)),
        pl.BlockSpec((4, 1, tl, LANES), lambda i: (0, 0, i, 0)),
        _row_spec(tl, BRANCH_W),
        pl.BlockSpec((POOL_HALO, BRANCH_W), lambda i: (jnp.maximum(i * hb - 1, 0), 0)),
        pl.BlockSpec((POOL_HALO, BRANCH_W), lambda i: (jnp.minimum((i + 1) * hb, n_hblk - 1), 0)),
        _row_spec(tl, D_MODEL),
        _const_spec((D_MODEL, D_MODEL)),
        _const_spec((4, A_QK)),
        _const_spec((1, LANES)),
        _const_spec((4, LANES, LANES)),
        _const_spec((1, BRANCH_W)),
    ]
    return pl.pallas_call(
        kern,
        grid=(n_tiles,),
        in_specs=in_specs,
        out_specs=_row_spec(tl, D_MODEL),
        out_shape=jax.ShapeDtypeStruct((lp, D_MODEL), F32),
        scratch_shapes=[pltpu.VMEM((tl + 2 * POOL_HALO, BRANCH_W), F32)],
        compiler_params=pltpu.CompilerParams(
            dimension_semantics=("arbitrary",), vmem_limit_bytes=VMEM_LIMIT),
        name="final",
    )(h, oa, ob, oc, du, du, du, sg, lw["wout"], lw["lam"], lw["subln"], lw["wg"], lw["dsc"])


def _layer_weights(i, norm_w, w_in, w_out, a_q_norm, a_k_norm, a_lambda, a_subln, b_q_norm,
                   b_k_norm, c_q_lat_norm, c_kv_lat_norm, c_w_uq, c_w_ukv, c_q_norm, c_k_norm,
                   d_w_group, d_scale):
    win = w_in[i][:, :P_HEAD_END].astype(BF16)
    wtail = w_in[i][:, P_TAIL:].astype(BF16)
    uq = c_w_uq[i].reshape(C_Q_LORA, 4, C_NOPE + C_ROPE)
    uq = jnp.pad(uq, ((0, 0), (0, 0), (0, 2 * LANES - C_NOPE - C_ROPE)))
    wuq = uq.reshape(C_Q_LORA, 4 * 2 * LANES).astype(BF16)
    row = lambda v: v.reshape(1, -1).astype(F32)
    a_scale = (A_QK ** -0.5) * LOG2E
    tiny = jnp.finfo(F32).tiny
    q_mag = jnp.maximum(jnp.max(jnp.abs(a_q_norm[i])).astype(F32) * a_scale, tiny)
    k_mag = jnp.maximum(jnp.max(jnp.abs(a_k_norm[i])).astype(F32), tiny)
    shift = jnp.clip(jnp.round(0.5 * (jnp.log2(k_mag) - jnp.log2(q_mag))), -60.0, 60.0)
    aqf = jnp.full((1, LANES), a_scale, F32) * jnp.exp2(shift)
    akf = jnp.full((1, LANES), 1.0, F32) * jnp.exp2(-shift)
    pad_row = lambda v: jnp.pad(v.astype(F32), (0, LANES - v.shape[0])).reshape(1, LANES)
    return dict(
        nw=row(norm_w[i]), win=win, wtail=wtail, wuq=wuq, aqf=aqf, akf=akf, wukv=c_w_ukv[i].astype(BF16),
        aqn=row(jnp.tile(a_q_norm[i], 2)), akn=row(jnp.tile(a_k_norm[i], 2)),
        bqn=row(b_q_norm[i]), bkn=row(b_k_norm[i]),
        cql=row(c_q_lat_norm[i]), ckvl=row(c_kv_lat_norm[i]),
        cqnn=row(c_q_norm[i][:C_NOPE]), cqnr=pad_row(c_q_norm[i][C_NOPE:]),
        cknn=row(c_k_norm[i][:C_NOPE]), cknr=pad_row(c_k_norm[i][C_NOPE:]),
        wout=w_out[i].astype(BF16), lam=a_lambda[i].astype(F32), subln=row(a_subln[i]),
        wg=d_w_group[i].astype(BF16), dsc=row(d_scale[i]),
    )


def kernel(x, meta_tokens, norm_w, w_in, w_out, a_q_norm, a_k_norm, a_lambda, a_subln, b_q_norm,
           b_k_norm, c_q_lat_norm, c_kv_lat_norm, c_w_uq, c_w_ukv, c_q_norm, c_k_norm,
           d_w_group, d_scale):
    b, n_tok, d = x.shape
    assert b == 1 and d == D_MODEL
    lp = n_tok + SEQ_PAD
    tl = ROW_TILE
    ck = KV_CHUNK
    assert lp % tl == 0 and lp % ATTN_ROWS == 0 and n_tok % GRID_W == 0
    assert n_tok % (2 * ck) == 0 and (n_tok // (2 * ck) - 2) % 3 == 0
    depth = norm_w.shape[0]

    h = jnp.concatenate([meta_tokens.astype(x.dtype), x[0],
                         jnp.zeros((lp - N_META - n_tok, d), x.dtype)], axis=0)
    tables = _rope_lane_tables(n_tok, lp)

    for i in range(depth):
        lambda_init = 0.8 - 0.6 * math.exp(-0.3 * i)
        lw = _layer_weights(i, norm_w, w_in, w_out, a_q_norm, a_k_norm, a_lambda, a_subln,
                            b_q_norm, b_k_norm, c_q_lat_norm, c_kv_lat_norm, c_w_uq, c_w_ukv,
                            c_q_norm, c_k_norm, d_w_group, d_scale)
        qa, ka, va, qb, kb, vb, qc, kc, vc, sg, du = _proj_call(h, lw, tables, lp, tl, N_META + n_tok)
        oa = _attn_call(qa, ka, va, "A", lp, n_tok, ATTN_ROWS // 2, ck)
        ob = _attn_call(qb, kb, vb, "B", lp, n_tok, ATTN_ROWS // 2, ck)
        oc = _attn_call(qc, kc, vc, "C", lp, n_tok, ATTN_ROWS, ck)
        h = _final_call(h, oa, ob, oc, du, sg, lw, lambda_init, lp, n_tok, tl)
    return h[N_META:N_META + n_tok][None]
```

```python
import functools
import math

import jax
import jax.numpy as jnp
from jax import lax
from jax.experimental import pallas as pl
from jax.experimental.pallas import tpu as pltpu

F32 = jnp.float32
BF16 = jnp.bfloat16
F8 = jnp.float8_e4m3fn

D_MODEL = 2048
N_META = 16
GRID_W = 64
EPS = 1e-6
ROPE_THETA = 500000.0
AXIAL_THETA = 10000.0
BRANCH_W = 512
LANES = 128
SEQ_PAD = 512
ROW_TILE = 384
ATTN_ROWS = 768
KV_CHUNK = 1024
PV_ROW_BLOCKS = 2
A_QK = 64
A_ROT = 16
B_HD = 128
C_NOPE = 128
C_ROPE = 64
C_Q_LORA = 384
C_KV_LORA = 256
POOL_WINDOWS = (2, 4, 8, 16)
POOL_HALO = 8
LOG2E = 1.4426950408889634
NEG_BIG = -1e30
VMEM_LIMIT = 56 * 1024 * 1024

P_AQ, P_AV, P_AG = 0, 1024, 1536
P_BQ, P_BV, P_BG = 2048, 2816, 3072
P_CQ = 3584
P_TAIL = 4288
IN_COLS = 5824
P_HEAD_END = P_CQ + C_Q_LORA + C_KV_LORA + LANES


def _mean_sq(x, n):
    return jnp.sum(x * x, axis=-1, keepdims=True) * (1.0 / n)


def _half_mean_sq(x):
    xx = x * x
    lower = lax.broadcasted_iota(jnp.int32, x.shape, 1) < (LANES // 2)
    lo = jnp.sum(jnp.where(lower, xx, 0.0), axis=-1, keepdims=True)
    hi = jnp.sum(jnp.where(lower, 0.0, xx), axis=-1, keepdims=True)
    return jnp.where(lower, lo, hi) * (2.0 / LANES)


def _rope(x, c, s, half):
    lane = lax.broadcasted_iota(jnp.int32, x.shape, 1)
    fwd = pltpu.roll(x, LANES - half, axis=1)
    bwd = pltpu.roll(x, half, axis=1)
    partner = jnp.where((lane % (2 * half)) < half, fwd, bwd)
    return x * c + partner * s


def _proj_tile(h_ref, nw_ref, win_ref, wtail_ref, wuq_ref, wukv_ref,
               aqf_ref, akf_ref, aqn_ref, akn_ref, bqn_ref, bkn_ref, cql_ref, ckvl_ref,
               cqnn_ref, cqnr_ref, cknn_ref, cknr_ref,
               ca_ref, sa_ref, cb_ref, sb_ref, cc_ref, sc_ref,
               qa_ref, ka_ref, va_ref, qb_ref, kb_ref, vb_ref,
               qc_ref, kc_ref, vc_ref, sg_ref, du_ref):
    h = h_ref[...]
    ms = jnp.mean(h * h, axis=-1, keepdims=True)
    hb = (h * lax.rsqrt(ms + EPS) * nw_ref[...]).astype(BF16)

    def mm(lo, hi):
        return jnp.dot(hb, win_ref[:, lo:hi], preferred_element_type=F32)

    def mm_tail(blk):
        return jnp.dot(hb, wtail_ref[:, blk * BRANCH_W:(blk + 1) * BRANCH_W],
                       preferred_element_type=F32)

    ca, sa = ca_ref[...], sa_ref[...]
    cb, sb = cb_ref[...], sb_ref[...]
    cc, sc = cc_ref[...], sc_ref[...]

    pa = mm(P_AQ, P_AV)
    for blk in range(8):
        x = pa[:, blk * LANES:(blk + 1) * LANES]
        w = aqn_ref[...] if blk < 4 else akn_ref[...]
        x = x * lax.rsqrt(_half_mean_sq(x) + EPS) * w
        x = _rope(x, ca, sa, A_ROT // 2)
        if blk < 4:
            qa_ref[:, blk * LANES:(blk + 1) * LANES] = (x * aqf_ref[...]).astype(BF16)
        else:
            ka_ref[:, (blk - 4) * LANES:(blk - 3) * LANES] = (x * akf_ref[...]).astype(BF16)
    va_ref[...] = mm(P_AV, P_AG).astype(BF16)

    pb = mm(P_BQ, P_BV)
    b_scale = (B_HD ** -0.5) * LOG2E
    for blk in range(6):
        x = pb[:, blk * LANES:(blk + 1) * LANES]
        w = bqn_ref[...] if blk < 4 else bkn_ref[...]
        x = x * lax.rsqrt(_mean_sq(x, B_HD) + EPS) * w
        x = _rope(x, cb, sb, B_HD // 4)
        if blk < 4:
            qb_ref[:, blk * LANES:(blk + 1) * LANES] = (x * b_scale).astype(BF16)
        else:
            kb_ref[:, (blk - 4) * LANES:(blk - 3) * LANES] = x.astype(BF16)
    vb_ref[...] = mm(P_BV, P_BG).astype(BF16)

    pc = mm(P_CQ, P_HEAD_END)
    cq = pc[:, 0:C_Q_LORA]
    cq = cq * lax.rsqrt(jnp.mean(cq * cq, axis=-1, keepdims=True) + EPS) * cql_ref[...]
    qfull = jnp.dot(cq.astype(BF16), wuq_ref[...], preferred_element_type=F32)
    ckv = pc[:, C_Q_LORA:C_Q_LORA + C_KV_LORA]
    ckv = ckv * lax.rsqrt(jnp.mean(ckv * ckv, axis=-1, keepdims=True) + EPS) * ckvl_ref[...]
    kvfull = jnp.dot(ckv.astype(BF16), wukv_ref[...], preferred_element_type=F32)
    kr = pc[:, C_Q_LORA + C_KV_LORA:]
    kr = jnp.where(lax.broadcasted_iota(jnp.int32, kr.shape, 1) < C_ROPE, kr, 0.0)
    kr = kr * lax.rsqrt(_mean_sq(kr, C_ROPE) + EPS) * cknr_ref[...]
    kr = _rope(kr, cc, sc, C_ROPE // 2).astype(BF16)
    c_scale = ((C_NOPE + C_ROPE) ** -0.5) * LOG2E
    for hd in range(4):
        base = hd * 2 * LANES
        qn = qfull[:, base:base + LANES]
        qn = qn * lax.rsqrt(_mean_sq(qn, C_NOPE) + EPS) * cqnn_ref[...]
        qr = qfull[:, base + LANES:base + 2 * LANES]
        qr = qr * lax.rsqrt(_mean_sq(qr, C_ROPE) + EPS) * cqnr_ref[...]
        qr = _rope(qr, cc, sc, C_ROPE // 2)
        qc_ref[:, base:base + LANES] = (qn * c_scale).astype(BF16)
        qc_ref[:, base + LANES:base + 2 * LANES] = (qr * c_scale).astype(BF16)
        kn = kvfull[:, base:base + LANES]
        kn = kn * lax.rsqrt(_mean_sq(kn, C_NOPE) + EPS) * cknn_ref[...]
        kc_ref[:, base:base + LANES] = kn.astype(BF16)
        kc_ref[:, base + LANES:base + 2 * LANES] = kr
        vc_ref[:, hd * LANES:(hd + 1) * LANES] = kvfull[:, base + LANES:base + 2 * LANES].astype(BF16)

    gates = [mm(P_AG, P_BQ), mm(P_BG, P_CQ), mm_tail(0), mm_tail(2)]
    for blk, g in enumerate(gates):
        sg_ref[:, blk * BRANCH_W:(blk + 1) * BRANCH_W] = jax.nn.silu(g).astype(BF16)
    du_ref[...] = mm_tail(1)


def _f8_split(x):
    hi = x.astype(F8).astype(F32)
    return hi, (x - hi) * 16.0


def _lane_repeat(x, n):
    return jnp.concatenate([x] * n, axis=1)


def _attn_tile(q_ref, k_ref, o_ref, vx_sc, s_sc, mc_sc, mr_sc, m_sc, acc_sc, k8_sc=None, *,
               mode, tq, ck, n_main, tail_valid):
    if mode == "A":
        qh, ql = _f8_split(q_ref[...].astype(F32))
        lower = lax.broadcasted_iota(jnp.int32, qh.shape, 1) < A_QK
        b0 = jnp.where(lower, qh, pltpu.roll(ql, A_QK, axis=1)).astype(F8)
        b1 = jnp.where(lower, pltpu.roll(qh, A_QK, axis=1), ql).astype(F8)
        q8 = [jnp.concatenate([b, b], axis=1) for b in (b0, b1)]
        rows = 2 * tq
    elif mode == "B":
        qs = jnp.concatenate([q_ref[:, 0:LANES], q_ref[:, LANES:2 * LANES]], axis=0)
        rows = qs.shape[0]
    else:
        qs = q_ref[...]
        rows = qs.shape[0]
    n_grp = rows // tq

    m_sc[...] = jnp.full(m_sc.shape, NEG_BIG, F32)
    acc_sc[...] = jnp.zeros(acc_sc.shape, F32)

    contract_last = (((1,), (1,)), ((), ()))

    def scores(start, width):
        if mode == "A":
            parts = [lax.dot_general(q8[m], k8_sc[pl.ds(start, width), m * 2 * LANES:(m + 1) * 2 * LANES],
                                     contract_last, preferred_element_type=F32) for m in range(2)]
            return jnp.concatenate(parts, axis=0)
        return lax.dot_general(qs, k_ref[pl.ds(start, width), :], contract_last,
                               preferred_element_type=F32)

    def row_max(s):
        return jnp.broadcast_to(jnp.max(s, axis=1, keepdims=True), (rows, LANES))

    def lane_max(s):
        parts = [s[:, t * LANES:(t + 1) * LANES] for t in range(s.shape[1] // LANES)]
        return functools.reduce(jnp.maximum, parts)

    def qk_stage(j, slot):
        s = scores(pl.multiple_of(j * ck, ck), ck)
        s_sc[slot] = s
        mc_sc[slot] = lane_max(s)

    def softmax_pv(s_parts, m_cur, vx, after_block=()):
        m_prev = m_sc[...]
        m_new = jnp.maximum(m_prev, m_cur)
        alpha = jnp.exp2(m_prev - m_new)
        m_sc[...] = m_new
        assert len(after_block) <= PV_ROW_BLOCKS
        blk = rows // PV_ROW_BLOCKS
        for r0 in range(0, rows, blk):
            mr = m_new[r0:r0 + blk]
            p = jnp.concatenate(
                [jnp.exp2(s[r0:r0 + blk] - _lane_repeat(mr, s.shape[1] // LANES)).astype(BF16)
                 for s in s_parts], axis=1)
            pv = jnp.dot(p, vx, preferred_element_type=F32)
            acc_sc[r0:r0 + blk, :] = _lane_repeat(alpha[r0:r0 + blk], 2) * acc_sc[r0:r0 + blk, :] + pv
            if r0 // blk < len(after_block):
                after_block[r0 // blk]()

    def qk_pair(p, r):
        qk_stage(2 * p, 2 * r)
        qk_stage(2 * p + 1, 2 * r + 1)

    def reduce_pair(r):
        mr_sc[r] = row_max(jnp.maximum(mc_sc[2 * r], mc_sc[2 * r + 1]))

    def pv_pair(p, r, after_block=()):
        vx = vx_sc[pl.ds(pl.multiple_of(p * 2 * ck, 2 * ck), 2 * ck), :]
        softmax_pv([s_sc[2 * r], s_sc[2 * r + 1]], mr_sc[r], vx, after_block)

    n_pairs = n_main // 2
    qk_pair(0, 0)
    qk_pair(1, 1)
    reduce_pair(0)

    def body(i, carry):
        for r in range(3):
            p = 3 * i + r
            reduce_pair((r + 1) % 3)
            nxt = (r + 2) % 3
            if mode == "A":
                pv_pair(p, r, [functools.partial(qk_stage, 2 * (p + 2), 2 * nxt),
                               functools.partial(qk_stage, 2 * (p + 2) + 1, 2 * nxt + 1)])
            else:
                pv_pair(p, r)
                qk_pair(p + 2, nxt)
        return carry

    lax.fori_loop(0, (n_pairs - 2) // 3, body, 0)
    r = (n_pairs - 2) % 3
    pv_pair(n_pairs - 2, r)
    reduce_pair((r + 1) % 3)
    t0 = n_main * ck
    s_tail = scores(t0, LANES)
    col = lax.broadcasted_iota(jnp.int32, s_tail.shape, 1)
    s_tail = jnp.where(col < tail_valid, s_tail, NEG_BIG)
    pv_pair(n_pairs - 1, (r + 1) % 3)
    softmax_pv([s_tail], row_max(s_tail), vx_sc[t0:t0 + LANES, :])

    o = acc_sc[:, 0:LANES] / acc_sc[:, LANES:2 * LANES]
    for g in range(n_grp):
        o_ref[g] = o[g * tq:(g + 1) * tq]


def _attn_kernel(q_ref, k_ref, v_ref, o_ref, vx_sc, *scratch, tq, q_valid, **tile_args):
    lp = v_ref.shape[0]

    @pl.when(pl.program_id(1) == 0)
    def _():
        def fill(c, carry):
            r = pl.multiple_of(c * tq, tq)
            vx_sc[pl.ds(r, tq), 0:LANES] = v_ref[pl.ds(r, tq), :]
            vx_sc[pl.ds(r, tq), LANES:2 * LANES] = jnp.ones((tq, LANES), BF16)
            if tile_args["mode"] == "A":
                k8_sc = scratch[-1]
                kh, kl = _f8_split(k_ref[pl.ds(r, tq), :].astype(F32))
                kl = kl * (1.0 / 16.0)
                khr = pltpu.roll(kh, A_QK, axis=1)
                klr = pltpu.roll(kl, A_QK, axis=1)
                lower = lax.broadcasted_iota(jnp.int32, kh.shape, 1) < A_QK
                blocks = [jnp.where(lower, kh, khr * 0.0625), jnp.where(lower, kl, klr * 0.0625),
                          jnp.where(lower, khr, kh * 0.0625), jnp.where(lower, klr, kl * 0.0625)]
                for b, blk in enumerate(blocks):
                    k8_sc[pl.ds(r, tq), b * LANES:(b + 1) * LANES] = blk.astype(F8)
            return carry
        lax.fori_loop(0, lp // tq, fill, 0)

    has_rows = pl.program_id(1) * tq < q_valid

    @pl.when(has_rows)
    def _():
        _attn_tile(q_ref, k_ref, o_ref, vx_sc, *scratch, tq=tq, **tile_args)

    @pl.when(jnp.logical_not(has_rows))
    def _():
        o_ref[...] = jnp.zeros(o_ref.shape, F32)


def _final_tile(h_ref, oa_ref, ob_ref, oc_ref, du_ref, dup_ref, dun_ref, sg_ref,
                wout_ref, lam_ref, subln_ref, wg_ref, dsc_ref, out_ref, ext_sc, *,
                lambda_init, seq_len, tl):
    i = pl.program_id(0)

    lp = lam_ref[...]
    lam = (jnp.exp(jnp.sum(lp[0:1, :] * lp[1:2, :], axis=-1, keepdims=True))
           - jnp.exp(jnp.sum(lp[2:3, :] * lp[3:4, :], axis=-1, keepdims=True)) + lambda_init)

    ys = []
    for hd in range(4):
        d = oa_ref[hd, 0] - lam * oa_ref[hd, 1]
        d = d * lax.rsqrt(_mean_sq(d, LANES) + EPS) * subln_ref[...]
        d = d * (1.0 - lambda_init)
        ys.append(d * sg_ref[:, hd * LANES:(hd + 1) * LANES].astype(F32))
    for g in range(2):
        for r in range(2):
            c = 4 + 2 * g + r
            ys.append(ob_ref[g, r] * sg_ref[:, c * LANES:(c + 1) * LANES].astype(F32))
    for hd in range(4):
        c = 8 + hd
        ys.append(oc_ref[hd, 0] * sg_ref[:, c * LANES:(c + 1) * LANES].astype(F32))

    row0 = i * tl
    def masked(x, first_row):
        r = first_row + lax.broadcasted_iota(jnp.int32, x.shape, 0)
        return jnp.where((r >= 0) & (r < seq_len), x, 0.0)
    ext_sc[0:POOL_HALO, :] = masked(dup_ref[...], row0 - POOL_HALO)
    ext_sc[POOL_HALO:POOL_HALO + tl, :] = masked(du_ref[...], row0)
    ext_sc[POOL_HALO + tl:2 * POOL_HALO + tl, :] = masked(dun_ref[...], row0 + tl)
    t = row0 + lax.broadcasted_iota(jnp.int32, (tl, LANES), 0)
    for g, w in enumerate(POOL_WINDOWS):
        cols = slice(g * LANES, (g + 1) * LANES)
        acc = ext_sc[POOL_HALO - w // 2:POOL_HALO - w // 2 + tl, cols]
        for j in range(-w // 2 + 1, w // 2):
            acc = acc + ext_sc[POOL_HALO + j:POOL_HALO + j + tl, cols]
        cnt = jnp.minimum(t + w // 2, seq_len) - jnp.maximum(t - w // 2, 0)
        cnt = jnp.maximum(cnt, 1).astype(F32)
        pooled = acc / cnt - ext_sc[POOL_HALO:POOL_HALO + tl, cols]
        mixed = jnp.dot(pooled.astype(BF16), wg_ref[g], preferred_element_type=F32)
        c = 12 + g
        ys.append(mixed * dsc_ref[:, cols] * sg_ref[:, c * LANES:(c + 1) * LANES].astype(F32))

    y = jnp.concatenate([v.astype(BF16) for v in ys], axis=1)
    out_ref[...] = h_ref[...] + jnp.dot(y, wout_ref[...], preferred_element_type=F32)


def _rope_lane_tables(n_tok, lp):
    l = N_META + n_tok
    rows = n_tok // GRID_W
    lane = jnp.arange(LANES, dtype=jnp.int32)

    def inv_freq(idx, dim, theta):
        return theta ** (-(2 * idx).astype(F32) / dim)

    def padded(v):
        return jnp.pad(v.astype(F32), (0, lp - l))[:, None]

    pos = padded(jnp.arange(l, dtype=jnp.int32))
    row_pos = padded(jnp.concatenate([jnp.full((N_META,), -1, jnp.int32),
                                      jnp.repeat(jnp.arange(rows, dtype=jnp.int32), GRID_W)]))
    col_pos = padded(jnp.concatenate([jnp.arange(N_META, dtype=jnp.int32),
                                      jnp.tile(jnp.arange(GRID_W, dtype=jnp.int32), rows)]))

    ja = lane % A_QK
    inv_a = jnp.where(ja < A_ROT, inv_freq(ja % (A_ROT // 2), A_ROT, ROPE_THETA), 0.0)
    sign_a = jnp.where(ja < A_ROT // 2, -1.0, 1.0).astype(F32)
    ang_a = pos * inv_a[None, :]
    jb = lane % (B_HD // 2)
    inv_b = inv_freq(jb % (B_HD // 4), B_HD // 2, AXIAL_THETA)
    inv_r = jnp.where(lane < B_HD // 2, inv_b, 0.0)
    inv_c = jnp.where(lane < B_HD // 2, 0.0, inv_b)
    sign_b = jnp.where(jb < B_HD // 4, -1.0, 1.0).astype(F32)
    ang_b = row_pos * inv_r[None, :] + col_pos * inv_c[None, :]
    inv_m = jnp.where(lane < C_ROPE, inv_freq(lane % (C_ROPE // 2), C_ROPE, ROPE_THETA), 0.0)
    sign_m = jnp.where(lane < C_ROPE // 2, -1.0, 1.0).astype(F32)
    ang_m = pos * inv_m[None, :]
    return (jnp.cos(ang_a), jnp.sin(ang_a) * sign_a[None, :],
            jnp.cos(ang_b), jnp.sin(ang_b) * sign_b[None, :],
            jnp.cos(ang_m), jnp.sin(ang_m) * sign_m[None, :])


def _row_tile_kernel(tile_fn, n_out, n_scratch, tl, seq_len):
    def kern(*refs):
        outs = refs[len(refs) - n_scratch - n_out:len(refs) - n_scratch]
        has_rows = pl.program_id(0) * tl < seq_len

        @pl.when(has_rows)
        def _():
            tile_fn(*refs)

        @pl.when(jnp.logical_not(has_rows))
        def _():
            for o in outs:
                o[...] = jnp.zeros(o.shape, o.dtype)
    return kern


class _TokenRows:
    def __init__(self, meta_ref, prev_ref, body_refs, seq_len, tl):
        self._parts = (meta_ref, prev_ref, body_refs)
        self._seq_len, self._tl = seq_len, tl

    def __getitem__(self, idx):
        meta_ref, prev_ref, body_refs = self._parts
        i = pl.program_id(0)
        head = jnp.where(i == 0, meta_ref[...], prev_ref[...])
        body = [r[...] for r in body_refs]
        body[-1] = body[-1][:body[-1].shape[0] - N_META]
        rows = jnp.concatenate([head] + body, axis=0)
        r_nat = i * self._tl + lax.broadcasted_iota(jnp.int32, rows.shape, 0)
        return jnp.where(r_nat < self._seq_len, rows, 0.0)


def _with_token_rows(tile_fn, seq_len, tl):
    def fn(meta_ref, prev_ref, b0_ref, b1_ref, b2_ref, *rest):
        tile_fn(_TokenRows(meta_ref, prev_ref, (b0_ref, b1_ref, b2_ref), seq_len, tl), *rest)
    return fn


def _token_row_specs(tl, n_tok):
    sub = tl // 3
    n_sub = n_tok // sub
    return [_const_spec((N_META, D_MODEL)),
            pl.BlockSpec((N_META, D_MODEL),
                         lambda i: (jnp.clip(i * (tl // N_META) - 1, 0, n_tok // N_META - 1), 0)),
            pl.BlockSpec((sub, D_MODEL), lambda i: (jnp.minimum(3 * i, n_sub - 1), 0)),
            pl.BlockSpec((sub, D_MODEL), lambda i: (jnp.minimum(3 * i + 1, n_sub - 1), 0)),
            pl.BlockSpec((sub, D_MODEL), lambda i: (jnp.minimum(3 * i + 2, n_sub - 1), 0))]


def _row_spec(tl, width):
    return pl.BlockSpec((tl, width), lambda i: (i, 0))


def _const_spec(shape):
    nd = len(shape)
    return pl.BlockSpec(shape, lambda i: (0,) * nd, pipeline_mode=pl.Buffered(1))


def _proj_call(h, lw, tables, lp, tl, seq_len):
    n_tiles = lp // tl
    small = [lw["aqf"], lw["akf"], lw["aqn"], lw["akn"], lw["bqn"], lw["bkn"], lw["cql"], lw["ckvl"],
             lw["cqnn"], lw["cqnr"], lw["cknn"], lw["cknr"]]
    first = isinstance(h, tuple)
    h_specs = _token_row_specs(tl, h[1].shape[0]) if first else [_row_spec(tl, D_MODEL)]
    h_args = (h[0], h[1], h[1], h[1], h[1]) if first else (h,)
    tile_fn = _with_token_rows(_proj_tile, seq_len, tl) if first else _proj_tile
    in_specs = (h_specs + [_const_spec((1, D_MODEL)),
                 pl.BlockSpec((None, D_MODEL, P_HEAD_END), lambda i: (lw["layer"], 0, 0),
                              pipeline_mode=pl.Buffered(1)),
                 _const_spec((D_MODEL, IN_COLS - P_TAIL)),
                 _const_spec(lw["wuq"].shape),
                 _const_spec(lw["wukv"].shape)]
                + [_const_spec(a.shape) for a in small]
                + [_row_spec(tl, LANES)] * 6)
    out_widths = [512, 512, 512, 512, 256, 256, 1024, 1024, 512, 2048, 512]
    out_dtypes = [BF16] * 10 + [F32]
    out_shape = [jax.ShapeDtypeStruct((lp, w), dt) for w, dt in zip(out_widths, out_dtypes)]
    out_specs = [_row_spec(tl, w) for w in out_widths]
    return pl.pallas_call(
        _row_tile_kernel(tile_fn, len(out_widths), 0, tl, seq_len),
        grid=(n_tiles,),
        in_specs=in_specs,
        out_specs=out_specs,
        out_shape=out_shape,
        compiler_params=pltpu.CompilerParams(
            dimension_semantics=("arbitrary",), vmem_limit_bytes=VMEM_LIMIT),
        name="proj",
    )(*h_args, lw["nw"], lw["win"], lw["wtail"], lw["wuq"], lw["wukv"], *small, *tables)


def _attn_call(q, k, v, mode, lp, n_tok, tq, ck):
    n_heads = {"A": 4, "B": 2, "C": 4}[mode]
    n_grp = {"A": 2, "B": 2, "C": 1}[mode]
    qw = {"A": LANES, "B": 2 * LANES, "C": 2 * LANES}[mode]
    kw = {"A": LANES, "B": LANES, "C": 2 * LANES}[mode]
    n_main = n_tok // ck
    tail_valid = N_META + n_tok - n_main * ck
    kern = functools.partial(_attn_kernel, mode=mode, tq=tq, ck=ck, n_main=n_main,
                             tail_valid=tail_valid, q_valid=N_META + n_tok)
    return pl.pallas_call(
        kern,
        grid=(n_heads, lp // tq),
        in_specs=[pl.BlockSpec((tq, qw), lambda hd, i: (i, hd)),
                  pl.BlockSpec((lp, kw), lambda hd, i: (0, hd), pipeline_mode=pl.Buffered(1)),
                  pl.BlockSpec((lp, LANES), lambda hd, i: (0, hd), pipeline_mode=pl.Buffered(1))],
        out_specs=pl.BlockSpec((None, n_grp, tq, LANES), lambda hd, i: (hd, 0, i, 0)),
        out_shape=jax.ShapeDtypeStruct((n_heads, n_grp, lp, LANES), F32),
        scratch_shapes=[pltpu.VMEM((lp, 2 * LANES), BF16),
                        pltpu.VMEM((6, n_grp * tq, ck), F32),
                        pltpu.VMEM((6, n_grp * tq, LANES), F32),
                        pltpu.VMEM((3, n_grp * tq, LANES), F32),
                        pltpu.VMEM((n_grp * tq, LANES), F32),
                        pltpu.VMEM((n_grp * tq, 2 * LANES), F32)]
                       + ([pltpu.VMEM((lp, 4 * LANES), F8)] if mode == "A" else []),
        compiler_params=pltpu.CompilerParams(
            dimension_semantics=("arbitrary", "arbitrary"), vmem_limit_bytes=VMEM_LIMIT),
        name="attn_" + mode,
    )(q, k, v)


def _final_call(h, oa, ob, oc, du, sg, lw, lambda_init, lp, n_tok, tl):
    n_tiles = lp // tl
    hb = tl // POOL_HALO
    n_hblk = lp // POOL_HALO
    seq_len = N_META + n_tok
    tile_fn = functools.partial(_final_tile, lambda_init=lambda_init, seq_len=seq_len, tl=tl)
    first = isinstance(h, tuple)
    if first:
        tile_fn = _with_token_rows(tile_fn, seq_len, tl)
    h_specs = _token_row_specs(tl, h[1].shape[0]) if first else [_row_spec(tl, D_MODEL)]
    h_args = (h[0], h[1], h[1], h[1], h[1]) if first else (h,)
    kern = _row_tile_kernel(tile_fn, 1, 1, tl, seq_len)
    in_specs = h_specs + [
        pl.BlockSpec((4, 2, tl, LANES), lambda i: (0, 0, i, 0)),
        pl.BlockSpec((2, 2, tl, LANES), lambda i: (0, 0, i, 0)),
        pl.BlockSpec((4, 1, tl, LANES), lambda i: (0, 0, i, 0)),
        _row_spec(tl, BRANCH_W),
        pl.BlockSpec((POOL_HALO, BRANCH_W), lambda i: (jnp.maximum(i * hb - 1, 0), 0)),
        pl.BlockSpec((POOL_HALO, BRANCH_W), lambda i: (jnp.minimum((i + 1) * hb, n_hblk - 1), 0)),
        _row_spec(tl, D_MODEL),
        _const_spec((D_MODEL, D_MODEL)),
        _const_spec((4, A_QK)),
        _const_spec((1, LANES)),
        _const_spec((4, LANES, LANES)),
        _const_spec((1, BRANCH_W)),
    ]
    return pl.pallas_call(
        kern,
        grid=(n_tiles,),
        in_specs=in_specs,
        out_specs=_row_spec(tl, D_MODEL),
        out_shape=jax.ShapeDtypeStruct((lp, D_MODEL), F32),
        scratch_shapes=[pltpu.VMEM((tl + 2 * POOL_HALO, BRANCH_W), F32)],
        compiler_params=pltpu.CompilerParams(
            dimension_semantics=("arbitrary",), vmem_limit_bytes=VMEM_LIMIT),
        name="final",
    )(*h_args, oa, ob, oc, du, du, du, sg, lw["wout"], lw["lam"], lw["subln"], lw["wg"], lw["dsc"])


def _layer_weights(i, norm_w, w_in_bf16, w_out, a_q_norm, a_k_norm, a_lambda, a_subln, b_q_norm,
                   b_k_norm, c_q_lat_norm, c_kv_lat_norm, c_w_uq, c_w_ukv, c_q_norm, c_k_norm,
                   d_w_group, d_scale):
    wtail = w_in_bf16[i][:, P_TAIL:]
    uq = c_w_uq[i].reshape(C_Q_LORA, 4, C_NOPE + C_ROPE)
    uq = jnp.pad(uq, ((0, 0), (0, 0), (0, 2 * LANES - C_NOPE - C_ROPE)))
    wuq = uq.reshape(C_Q_LORA, 4 * 2 * LANES).astype(BF16)
    row = lambda v: v.reshape(1, -1).astype(F32)
    a_scale = (A_QK ** -0.5) * LOG2E
    tiny = jnp.finfo(F32).tiny
    q_mag = jnp.maximum(jnp.max(jnp.abs(a_q_norm[i])).astype(F32) * a_scale, tiny)
    k_mag = jnp.maximum(jnp.max(jnp.abs(a_k_norm[i])).astype(F32), tiny)
    shift = jnp.clip(jnp.round(0.5 * (jnp.log2(k_mag) - jnp.log2(q_mag))), -60.0, 60.0)
    aqf = jnp.full((1, LANES), a_scale, F32) * jnp.exp2(shift)
    akf = jnp.full((1, LANES), 1.0, F32) * jnp.exp2(-shift)
    pad_row = lambda v: jnp.pad(v.astype(F32), (0, LANES - v.shape[0])).reshape(1, LANES)
    return dict(
        layer=i, nw=row(norm_w[i]), win=w_in_bf16, wtail=wtail, wuq=wuq, aqf=aqf, akf=akf, wukv=c_w_ukv[i].astype(BF16),
        aqn=row(jnp.tile(a_q_norm[i], 2)), akn=row(jnp.tile(a_k_norm[i], 2)),
        bqn=row(b_q_norm[i]), bkn=row(b_k_norm[i]),
        cql=row(c_q_lat_norm[i]), ckvl=row(c_kv_lat_norm[i]),
        cqnn=row(c_q_norm[i][:C_NOPE]), cqnr=pad_row(c_q_norm[i][C_NOPE:]),
        cknn=row(c_k_norm[i][:C_NOPE]), cknr=pad_row(c_k_norm[i][C_NOPE:]),
        wout=w_out[i].astype(BF16), lam=a_lambda[i].astype(F32), subln=row(a_subln[i]),
        wg=d_w_group[i].astype(BF16), dsc=row(d_scale[i]),
    )


def kernel(x, meta_tokens, norm_w, w_in, w_out, a_q_norm, a_k_norm, a_lambda, a_subln, b_q_norm,
           b_k_norm, c_q_lat_norm, c_kv_lat_norm, c_w_uq, c_w_ukv, c_q_norm, c_k_norm,
           d_w_group, d_scale):
    b, n_tok, d = x.shape
    assert b == 1 and d == D_MODEL
    lp = n_tok + SEQ_PAD
    tl = ROW_TILE
    ck = KV_CHUNK
    assert lp % tl == 0 and lp % ATTN_ROWS == 0 and n_tok % GRID_W == 0
    assert n_tok % (2 * ck) == 0 and (n_tok // (2 * ck) - 2) % 3 == 0
    depth = norm_w.shape[0]

    assert tl % 3 == 0 and n_tok % (tl // 3) == 0 and tl % N_META == 0
    h = (meta_tokens.astype(x.dtype), x[0])
    tables = _rope_lane_tables(n_tok, lp)

    w_in_bf16 = w_in.astype(BF16)
    for i in range(depth):
        lambda_init = 0.8 - 0.6 * math.exp(-0.3 * i)
        lw = _layer_weights(i, norm_w, w_in_bf16, w_out, a_q_norm, a_k_norm, a_lambda, a_subln,
                            b_q_norm, b_k_norm, c_q_lat_norm, c_kv_lat_norm, c_w_uq, c_w_ukv,
                            c_q_norm, c_k_norm, d_w_group, d_scale)
        qa, ka, va, qb, kb, vb, qc, kc, vc, sg, du = _proj_call(h, lw, tables, lp, tl, N_META + n_tok)
        oa = _attn_call(qa, ka, va, "A", lp, n_tok, ATTN_ROWS // 2, ck)
        ob = _attn_call(qb, kb, vb, "B", lp, n_tok, ATTN_ROWS // 2, ck)
        oc = _attn_call(qc, kc, vc, "C", lp, n_tok, ATTN_ROWS, ck)
        h = _final_call(h, oa, ob, oc, du, sg, lw, lambda_init, lp, n_tok, tl)
    return h[N_META:N_META + n_tok][None]
```

```python
import functools
import math

import jax
import jax.numpy as jnp
from jax import lax
from jax.experimental import pallas as pl
from jax.experimental.pallas import tpu as pltpu

F32 = jnp.float32
BF16 = jnp.bfloat16
F8 = jnp.float8_e4m3fn

D_MODEL = 2048
N_META = 16
GRID_W = 64
EPS = 1e-6
ROPE_THETA = 500000.0
AXIAL_THETA = 10000.0
BRANCH_W = 512
LANES = 128
SEQ_PAD = 512
ROW_TILE = 384
ATTN_ROWS = 768
KV_CHUNK = 1024
PV_ROW_BLOCKS = 2
A_QK = 64
A_ROT = 16
B_HD = 128
C_NOPE = 128
C_ROPE = 64
C_Q_LORA = 384
C_KV_LORA = 256
POOL_WINDOWS = (2, 4, 8, 16)
POOL_HALO = 8
LOG2E = 1.4426950408889634
NEG_BIG = -1e30
VMEM_LIMIT = 56 * 1024 * 1024

P_AQ, P_AV, P_AG = 0, 1024, 1536
P_BQ, P_BV, P_BG = 2048, 2816, 3072
P_CQ = 3584
P_TAIL = 4288
IN_COLS = 5824
P_HEAD_END = P_CQ + C_Q_LORA + C_KV_LORA + LANES


def _mean_sq(x, n):
    return jnp.sum(x * x, axis=-1, keepdims=True) * (1.0 / n)


def _half_mean_sq(x):
    xx = x * x
    lower = lax.broadcasted_iota(jnp.int32, x.shape, 1) < (LANES // 2)
    lo = jnp.sum(jnp.where(lower, xx, 0.0), axis=-1, keepdims=True)
    hi = jnp.sum(jnp.where(lower, 0.0, xx), axis=-1, keepdims=True)
    return jnp.where(lower, lo, hi) * (2.0 / LANES)


def _rope(x, c, s, half):
    lane = lax.broadcasted_iota(jnp.int32, x.shape, 1)
    fwd = pltpu.roll(x, LANES - half, axis=1)
    bwd = pltpu.roll(x, half, axis=1)
    partner = jnp.where((lane % (2 * half)) < half, fwd, bwd)
    return x * c + partner * s


def _proj_tile(h_ref, nw_ref, win_ref, wtail_ref, wuq_ref, wukv_ref,
               aqf_ref, akf_ref, aqn_ref, akn_ref, bqn_ref, bkn_ref, cql_ref, ckvl_ref,
               cqnn_ref, cqnr_ref, cknn_ref, cknr_ref,
               ca_ref, sa_ref, cb_ref, sb_ref, cc_ref, sc_ref,
               qa_ref, ka_ref, va_ref, qb_ref, kb_ref, vb_ref,
               qc_ref, kc_ref, vc_ref, sg_ref, du_ref):
    h = h_ref[...]
    ms = jnp.mean(h * h, axis=-1, keepdims=True)
    hb = (h * lax.rsqrt(ms + EPS) * nw_ref[...]).astype(BF16)

    def mm(lo, hi):
        return jnp.dot(hb, win_ref[:, lo:hi], preferred_element_type=F32)

    def mm_tail(blk):
        return jnp.dot(hb, wtail_ref[:, blk * BRANCH_W:(blk + 1) * BRANCH_W],
                       preferred_element_type=F32)

    ca, sa = ca_ref[...], sa_ref[...]
    cb, sb = cb_ref[...], sb_ref[...]
    cc, sc = cc_ref[...], sc_ref[...]

    pa = mm(P_AQ, P_AV)
    for blk in range(8):
        x = pa[:, blk * LANES:(blk + 1) * LANES]
        w = aqn_ref[...] if blk < 4 else akn_ref[...]
        x = x * lax.rsqrt(_half_mean_sq(x) + EPS) * w
        x = _rope(x, ca, sa, A_ROT // 2)
        if blk < 4:
            qa_ref[:, blk * LANES:(blk + 1) * LANES] = (x * aqf_ref[...]).astype(BF16)
        else:
            ka_ref[:, (blk - 4) * LANES:(blk - 3) * LANES] = (x * akf_ref[...]).astype(BF16)
    va_ref[...] = mm(P_AV, P_AG).astype(BF16)

    pb = mm(P_BQ, P_BV)
    b_scale = (B_HD ** -0.5) * LOG2E
    for blk in range(6):
        x = pb[:, blk * LANES:(blk + 1) * LANES]
        w = bqn_ref[...] if blk < 4 else bkn_ref[...]
        x = x * lax.rsqrt(_mean_sq(x, B_HD) + EPS) * w
        x = _rope(x, cb, sb, B_HD // 4)
        if blk < 4:
            qb_ref[:, blk * LANES:(blk + 1) * LANES] = (x * b_scale).astype(BF16)
        else:
            kb_ref[:, (blk - 4) * LANES:(blk - 3) * LANES] = x.astype(BF16)
    vb_ref[...] = mm(P_BV, P_BG).astype(BF16)

    pc = mm(P_CQ, P_HEAD_END)
    cq = pc[:, 0:C_Q_LORA]
    cq = cq * lax.rsqrt(jnp.mean(cq * cq, axis=-1, keepdims=True) + EPS) * cql_ref[...]
    qfull = jnp.dot(cq.astype(BF16), wuq_ref[...], preferred_element_type=F32)
    ckv = pc[:, C_Q_LORA:C_Q_LORA + C_KV_LORA]
    ckv = ckv * lax.rsqrt(jnp.mean(ckv * ckv, axis=-1, keepdims=True) + EPS) * ckvl_ref[...]
    kvfull = jnp.dot(ckv.astype(BF16), wukv_ref[...], preferred_element_type=F32)
    kr = pc[:, C_Q_LORA + C_KV_LORA:]
    kr = jnp.where(lax.broadcasted_iota(jnp.int32, kr.shape, 1) < C_ROPE, kr, 0.0)
    kr = kr * lax.rsqrt(_mean_sq(kr, C_ROPE) + EPS) * cknr_ref[...]
    kr = _rope(kr, cc, sc, C_ROPE // 2).astype(BF16)
    c_scale = ((C_NOPE + C_ROPE) ** -0.5) * LOG2E
    for hd in range(4):
        base = hd * 2 * LANES
        qn = qfull[:, base:base + LANES]
        qn = qn * lax.rsqrt(_mean_sq(qn, C_NOPE) + EPS) * cqnn_ref[...]
        qr = qfull[:, base + LANES:base + 2 * LANES]
        qr = qr * lax.rsqrt(_mean_sq(qr, C_ROPE) + EPS) * cqnr_ref[...]
        qr = _rope(qr, cc, sc, C_ROPE // 2)
        qc_ref[:, base:base + LANES] = (qn * c_scale).astype(BF16)
        qc_ref[:, base + LANES:base + 2 * LANES] = (qr * c_scale).astype(BF16)
        kn = kvfull[:, base:base + LANES]
        kn = kn * lax.rsqrt(_mean_sq(kn, C_NOPE) + EPS) * cknn_ref[...]
        kc_ref[:, base:base + LANES] = kn.astype(BF16)
        kc_ref[:, base + LANES:base + 2 * LANES] = kr
        vc_ref[:, hd * LANES:(hd + 1) * LANES] = kvfull[:, base + LANES:base + 2 * LANES].astype(BF16)

    gates = [mm(P_AG, P_BQ), mm(P_BG, P_CQ), mm_tail(0), mm_tail(2)]
    for blk, g in enumerate(gates):
        sg_ref[:, blk * BRANCH_W:(blk + 1) * BRANCH_W] = jax.nn.silu(g).astype(BF16)
    du_ref[...] = mm_tail(1)


def _f8_split(x):
    hi = x.astype(F8).astype(F32)
    return hi, (x - hi) * 16.0


def _lane_repeat(x, n):
    return jnp.concatenate([x] * n, axis=1)


def _attn_tile(q_ref, k_ref, o_ref, vx_sc, s_sc, mc_sc, mr_sc, m_sc, acc_sc, k8_sc=None, *,
               mode, tq, ck, n_main, tail_valid):
    if mode == "A":
        qh, ql = _f8_split(q_ref[...].astype(F32))
        lower = lax.broadcasted_iota(jnp.int32, qh.shape, 1) < A_QK
        b0 = jnp.where(lower, qh, pltpu.roll(ql, A_QK, axis=1)).astype(F8)
        b1 = jnp.where(lower, pltpu.roll(qh, A_QK, axis=1), ql).astype(F8)
        q8 = [jnp.concatenate([b, b], axis=1) for b in (b0, b1)]
        rows = 2 * tq
    elif mode == "B":
        qs = jnp.concatenate([q_ref[:, 0:LANES], q_ref[:, LANES:2 * LANES]], axis=0)
        rows = qs.shape[0]
    else:
        qs = q_ref[...]
        rows = qs.shape[0]
    n_grp = rows // tq

    m_sc[...] = jnp.full(m_sc.shape, NEG_BIG, F32)
    acc_sc[...] = jnp.zeros(acc_sc.shape, F32)

    contract_last = (((1,), (1,)), ((), ()))

    def scores(start, width):
        if mode == "A":
            parts = [lax.dot_general(q8[m], k8_sc[pl.ds(start, width), m * 2 * LANES:(m + 1) * 2 * LANES],
                                     contract_last, preferred_element_type=F32) for m in range(2)]
            return jnp.concatenate(parts, axis=0)
        return lax.dot_general(qs, k_ref[pl.ds(start, width), :], contract_last,
                               preferred_element_type=F32)

    def row_max(s):
        return jnp.broadcast_to(jnp.max(s, axis=1, keepdims=True), (rows, LANES))

    def lane_max(s):
        parts = [s[:, t * LANES:(t + 1) * LANES] for t in range(s.shape[1] // LANES)]
        return functools.reduce(jnp.maximum, parts)

    def qk_stage(j, slot):
        s = scores(pl.multiple_of(j * ck, ck), ck)
        s_sc[slot] = s
        mc_sc[slot] = lane_max(s)

    def softmax_pv(s_parts, m_cur, vx, after_block=()):
        m_prev = m_sc[...]
        m_new = jnp.maximum(m_prev, m_cur)
        alpha = jnp.exp2(m_prev - m_new)
        m_sc[...] = m_new
        assert len(after_block) <= PV_ROW_BLOCKS
        blk = rows // PV_ROW_BLOCKS
        for r0 in range(0, rows, blk):
            mr = m_new[r0:r0 + blk]
            p = jnp.concatenate(
                [jnp.exp2(s[r0:r0 + blk] - _lane_repeat(mr, s.shape[1] // LANES)).astype(BF16)
                 for s in s_parts], axis=1)
            pv = jnp.dot(p, vx, preferred_element_type=F32)
            acc_sc[r0:r0 + blk, :] = _lane_repeat(alpha[r0:r0 + blk], 2) * acc_sc[r0:r0 + blk, :] + pv
            if r0 // blk < len(after_block):
                after_block[r0 // blk]()

    def qk_pair(p, r):
        qk_stage(2 * p, 2 * r)
        qk_stage(2 * p + 1, 2 * r + 1)

    def reduce_pair(r):
        mr_sc[r] = row_max(jnp.maximum(mc_sc[2 * r], mc_sc[2 * r + 1]))

    def pv_pair(p, r, after_block=()):
        vx = vx_sc[pl.ds(pl.multiple_of(p * 2 * ck, 2 * ck), 2 * ck), :]
        softmax_pv([s_sc[2 * r], s_sc[2 * r + 1]], mr_sc[r], vx, after_block)

    n_pairs = n_main // 2
    qk_pair(0, 0)
    qk_pair(1, 1)
    reduce_pair(0)

    def body(i, carry):
        for r in range(3):
            p = 3 * i + r
            reduce_pair((r + 1) % 3)
            nxt = (r + 2) % 3
            if mode == "A":
                pv_pair(p, r, [functools.partial(qk_stage, 2 * (p + 2), 2 * nxt),
                               functools.partial(qk_stage, 2 * (p + 2) + 1, 2 * nxt + 1)])
            else:
                pv_pair(p, r)
                qk_pair(p + 2, nxt)
        return carry

    lax.fori_loop(0, (n_pairs - 2) // 3, body, 0)
    r = (n_pairs - 2) % 3
    pv_pair(n_pairs - 2, r)
    reduce_pair((r + 1) % 3)
    t0 = n_main * ck
    s_tail = scores(t0, LANES)
    col = lax.broadcasted_iota(jnp.int32, s_tail.shape, 1)
    s_tail = jnp.where(col < tail_valid, s_tail, NEG_BIG)
    pv_pair(n_pairs - 1, (r + 1) % 3)
    softmax_pv([s_tail], row_max(s_tail), vx_sc[t0:t0 + LANES, :])

    o = acc_sc[:, 0:LANES] / acc_sc[:, LANES:2 * LANES]
    for g in range(n_grp):
        o_ref[g] = o[g * tq:(g + 1) * tq]


def _attn_kernel(q_ref, k_ref, v_ref, o_ref, vx_sc, *scratch, tq, q_valid, **tile_args):
    lp = v_ref.shape[0]

    @pl.when(pl.program_id(1) == 0)
    def _():
        def fill(c, carry):
            r = pl.multiple_of(c * tq, tq)
            vx_sc[pl.ds(r, tq), 0:LANES] = v_ref[pl.ds(r, tq), :]
            vx_sc[pl.ds(r, tq), LANES:2 * LANES] = jnp.ones((tq, LANES), BF16)
            if tile_args["mode"] == "A":
                k8_sc = scratch[-1]
                kh, kl = _f8_split(k_ref[pl.ds(r, tq), :].astype(F32))
                kl = kl * (1.0 / 16.0)
                khr = pltpu.roll(kh, A_QK, axis=1)
                klr = pltpu.roll(kl, A_QK, axis=1)
                lower = lax.broadcasted_iota(jnp.int32, kh.shape, 1) < A_QK
                blocks = [jnp.where(lower, kh, khr * 0.0625), jnp.where(lower, kl, klr * 0.0625),
                          jnp.where(lower, khr, kh * 0.0625), jnp.where(lower, klr, kl * 0.0625)]
                for b, blk in enumerate(blocks):
                    k8_sc[pl.ds(r, tq), b * LANES:(b + 1) * LANES] = blk.astype(F8)
            return carry
        lax.fori_loop(0, lp // tq, fill, 0)

    has_rows = pl.program_id(1) * tq < q_valid

    @pl.when(has_rows)
    def _():
        _attn_tile(q_ref, k_ref, o_ref, vx_sc, *scratch, tq=tq, **tile_args)

    @pl.when(jnp.logical_not(has_rows))
    def _():
        o_ref[...] = jnp.zeros(o_ref.shape, F32)


def _final_tile(h_ref, oa_ref, ob_ref, oc_ref, du_ref, dup_ref, dun_ref, sg_ref,
                wout_ref, lam_ref, subln_ref, wg_ref, dsc_ref, out_ref, ext_sc, *,
                lambda_init, seq_len, tl):
    i = pl.program_id(0)

    lp = lam_ref[...]
    lam = (jnp.exp(jnp.sum(lp[0:1, :] * lp[1:2, :], axis=-1, keepdims=True))
           - jnp.exp(jnp.sum(lp[2:3, :] * lp[3:4, :], axis=-1, keepdims=True)) + lambda_init)

    ys = []
    for hd in range(4):
        d = oa_ref[hd, 0] - lam * oa_ref[hd, 1]
        d = d * lax.rsqrt(_mean_sq(d, LANES) + EPS) * subln_ref[...]
        d = d * (1.0 - lambda_init)
        ys.append(d * sg_ref[:, hd * LANES:(hd + 1) * LANES].astype(F32))
    for g in range(2):
        for r in range(2):
            c = 4 + 2 * g + r
            ys.append(ob_ref[g, r] * sg_ref[:, c * LANES:(c + 1) * LANES].astype(F32))
    for hd in range(4):
        c = 8 + hd
        ys.append(oc_ref[hd, 0] * sg_ref[:, c * LANES:(c + 1) * LANES].astype(F32))

    row0 = i * tl
    def masked(x, first_row):
        r = first_row + lax.broadcasted_iota(jnp.int32, x.shape, 0)
        return jnp.where((r >= 0) & (r < seq_len), x, 0.0)
    ext_sc[0:POOL_HALO, :] = masked(dup_ref[...], row0 - POOL_HALO)
    ext_sc[POOL_HALO:POOL_HALO + tl, :] = masked(du_ref[...], row0)
    ext_sc[POOL_HALO + tl:2 * POOL_HALO + tl, :] = masked(dun_ref[...], row0 + tl)
    t = row0 + lax.broadcasted_iota(jnp.int32, (tl, LANES), 0)
    for g, w in enumerate(POOL_WINDOWS):
        cols = slice(g * LANES, (g + 1) * LANES)
        acc = ext_sc[POOL_HALO - w // 2:POOL_HALO - w // 2 + tl, cols]
        for j in range(-w // 2 + 1, w // 2):
            acc = acc + ext_sc[POOL_HALO + j:POOL_HALO + j + tl, cols]
        cnt = jnp.minimum(t + w // 2, seq_len) - jnp.maximum(t - w // 2, 0)
        cnt = jnp.maximum(cnt, 1).astype(F32)
        pooled = acc / cnt - ext_sc[POOL_HALO:POOL_HALO + tl, cols]
        mixed = jnp.dot(pooled.astype(BF16), wg_ref[g], preferred_element_type=F32)
        c = 12 + g
        ys.append(mixed * dsc_ref[:, cols] * sg_ref[:, c * LANES:(c + 1) * LANES].astype(F32))

    y = jnp.concatenate([v.astype(BF16) for v in ys], axis=1)
    out_ref[...] = h_ref[...] + jnp.dot(y, wout_ref[...], preferred_element_type=F32)


def _rope_lane_tables(n_tok, lp):
    l = N_META + n_tok
    rows = n_tok // GRID_W
    lane = jnp.arange(LANES, dtype=jnp.int32)

    def inv_freq(idx, dim, theta):
        return theta ** (-(2 * idx).astype(F32) / dim)

    def padded(v):
        return jnp.pad(v.astype(F32), (0, lp - l))[:, None]

    pos = padded(jnp.arange(l, dtype=jnp.int32))
    row_pos = padded(jnp.concatenate([jnp.full((N_META,), -1, jnp.int32),
                                      jnp.repeat(jnp.arange(rows, dtype=jnp.int32), GRID_W)]))
    col_pos = padded(jnp.concatenate([jnp.arange(N_META, dtype=jnp.int32),
                                      jnp.tile(jnp.arange(GRID_W, dtype=jnp.int32), rows)]))

    ja = lane % A_QK
    inv_a = jnp.where(ja < A_ROT, inv_freq(ja % (A_ROT // 2), A_ROT, ROPE_THETA), 0.0)
    sign_a = jnp.where(ja < A_ROT // 2, -1.0, 1.0).astype(F32)
    ang_a = pos * inv_a[None, :]
    jb = lane % (B_HD // 2)
    inv_b = inv_freq(jb % (B_HD // 4), B_HD // 2, AXIAL_THETA)
    inv_r = jnp.where(lane < B_HD // 2, inv_b, 0.0)
    inv_c = jnp.where(lane < B_HD // 2, 0.0, inv_b)
    sign_b = jnp.where(jb < B_HD // 4, -1.0, 1.0).astype(F32)
    ang_b = row_pos * inv_r[None, :] + col_pos * inv_c[None, :]
    inv_m = jnp.where(lane < C_ROPE, inv_freq(lane % (C_ROPE // 2), C_ROPE, ROPE_THETA), 0.0)
    sign_m = jnp.where(lane < C_ROPE // 2, -1.0, 1.0).astype(F32)
    ang_m = pos * inv_m[None, :]
    return (jnp.cos(ang_a), jnp.sin(ang_a) * sign_a[None, :],
            jnp.cos(ang_b), jnp.sin(ang_b) * sign_b[None, :],
            jnp.cos(ang_m), jnp.sin(ang_m) * sign_m[None, :])


def _row_tile_kernel(tile_fn, n_out, n_scratch, tl, seq_len):
    def kern(*refs):
        outs = refs[len(refs) - n_scratch - n_out:len(refs) - n_scratch]
        has_rows = pl.program_id(0) * tl < seq_len

        @pl.when(has_rows)
        def _():
            tile_fn(*refs)

        @pl.when(jnp.logical_not(has_rows))
        def _():
            for o in outs:
                o[...] = jnp.zeros(o.shape, o.dtype)
    return kern


class _TokenRows:
    def __init__(self, meta_ref, prev_ref, body_refs, seq_len, tl):
        self._parts = (meta_ref, prev_ref, body_refs)
        self._seq_len, self._tl = seq_len, tl

    def __getitem__(self, idx):
        meta_ref, prev_ref, body_refs = self._parts
        i = pl.program_id(0)
        head = jnp.where(i == 0, meta_ref[...], prev_ref[...])
        body = [r[...] for r in body_refs]
        body[-1] = body[-1][:body[-1].shape[0] - N_META]
        rows = jnp.concatenate([head] + body, axis=0)
        r_nat = i * self._tl + lax.broadcasted_iota(jnp.int32, rows.shape, 0)
        return jnp.where(r_nat < self._seq_len, rows, 0.0)


def _with_token_rows(tile_fn, seq_len, tl):
    def fn(meta_ref, prev_ref, b0_ref, b1_ref, b2_ref, *rest):
        tile_fn(_TokenRows(meta_ref, prev_ref, (b0_ref, b1_ref, b2_ref), seq_len, tl), *rest)
    return fn


def _token_row_specs(tl, n_tok):
    sub = tl // 3
    n_sub = n_tok // sub
    return [_const_spec((N_META, D_MODEL)),
            pl.BlockSpec((N_META, D_MODEL),
                         lambda i: (jnp.clip(i * (tl // N_META) - 1, 0, n_tok // N_META - 1), 0)),
            pl.BlockSpec((sub, D_MODEL), lambda i: (jnp.minimum(3 * i, n_sub - 1), 0)),
            pl.BlockSpec((sub, D_MODEL), lambda i: (jnp.minimum(3 * i + 1, n_sub - 1), 0)),
            pl.BlockSpec((sub, D_MODEL), lambda i: (jnp.minimum(3 * i + 2, n_sub - 1), 0))]


def _row_spec(tl, width):
    return pl.BlockSpec((tl, width), lambda i: (i, 0))


def _const_spec(shape):
    nd = len(shape)
    return pl.BlockSpec(shape, lambda i: (0,) * nd, pipeline_mode=pl.Buffered(1))


def _proj_call(h, lw, tables, lp, tl, seq_len):
    n_tiles = lp // tl
    small = [lw["aqf"], lw["akf"], lw["aqn"], lw["akn"], lw["bqn"], lw["bkn"], lw["cql"], lw["ckvl"],
             lw["cqnn"], lw["cqnr"], lw["cknn"], lw["cknr"]]
    first = isinstance(h, tuple)
    h_specs = _token_row_specs(tl, h[1].shape[0]) if first else [_row_spec(tl, D_MODEL)]
    h_args = (h[0], h[1], h[1], h[1], h[1]) if first else (h,)
    tile_fn = _with_token_rows(_proj_tile, seq_len, tl) if first else _proj_tile
    in_specs = (h_specs + [_const_spec((1, D_MODEL)),
                 pl.BlockSpec((None, D_MODEL, P_HEAD_END), lambda i: (lw["layer"], 0, 0),
                              pipeline_mode=pl.Buffered(1)),
                 _const_spec((D_MODEL, IN_COLS - P_TAIL)),
                 _const_spec(lw["wuq"].shape),
                 _const_spec(lw["wukv"].shape)]
                + [_const_spec(a.shape) for a in small]
                + [_row_spec(tl, LANES)] * 6)
    out_widths = [512, 512, 512, 512, 256, 256, 1024, 1024, 512, 2048, 512]
    out_dtypes = [BF16] * 10 + [F32]
    out_shape = [jax.ShapeDtypeStruct((lp, w), dt) for w, dt in zip(out_widths, out_dtypes)]
    out_specs = [_row_spec(tl, w) for w in out_widths]
    return pl.pallas_call(
        _row_tile_kernel(tile_fn, len(out_widths), 0, tl, seq_len),
        grid=(n_tiles,),
        in_specs=in_specs,
        out_specs=out_specs,
        out_shape=out_shape,
        compiler_params=pltpu.CompilerParams(
            dimension_semantics=("arbitrary",), vmem_limit_bytes=VMEM_LIMIT),
        name="proj",
    )(*h_args, lw["nw"], lw["win"], lw["wtail"], lw["wuq"], lw["wukv"], *small, *tables)


def _attn_call(q, k, v, mode, lp, n_tok, tq, ck):
    n_heads = {"A": 4, "B": 2, "C": 4}[mode]
    n_grp = {"A": 2, "B": 2, "C": 1}[mode]
    qw = {"A": LANES, "B": 2 * LANES, "C": 2 * LANES}[mode]
    kw = {"A": LANES, "B": LANES, "C": 2 * LANES}[mode]
    n_main = n_tok // ck
    tail_valid = N_META + n_tok - n_main * ck
    kern = functools.partial(_attn_kernel, mode=mode, tq=tq, ck=ck, n_main=n_main,
                             tail_valid=tail_valid, q_valid=N_META + n_tok)
    return pl.pallas_call(
        kern,
        grid=(n_heads, lp // tq),
        in_specs=[pl.BlockSpec((tq, qw), lambda hd, i: (i, hd)),
                  pl.BlockSpec((lp, kw), lambda hd, i: (0, hd), pipeline_mode=pl.Buffered(1)),
                  pl.BlockSpec((lp, LANES), lambda hd, i: (0, hd), pipeline_mode=pl.Buffered(1))],
        out_specs=pl.BlockSpec((None, n_grp, tq, LANES), lambda hd, i: (hd, 0, i, 0)),
        out_shape=jax.ShapeDtypeStruct((n_heads, n_grp, lp, LANES), F32),
        scratch_shapes=[pltpu.VMEM((lp, 2 * LANES), BF16),
                        pltpu.VMEM((6, n_grp * tq, ck), F32),
                        pltpu.VMEM((6, n_grp * tq, LANES), F32),
                        pltpu.VMEM((3, n_grp * tq, LANES), F32),
                        pltpu.VMEM((n_grp * tq, LANES), F32),
                        pltpu.VMEM((n_grp * tq, 2 * LANES), F32)]
                       + ([pltpu.VMEM((lp, 4 * LANES), F8)] if mode == "A" else []),
        compiler_params=pltpu.CompilerParams(
            dimension_semantics=("arbitrary", "arbitrary"), vmem_limit_bytes=VMEM_LIMIT),
        name="attn_" + mode,
    )(q, k, v)


def _final_call(h, oa, ob, oc, du, sg, lw, lambda_init, lp, n_tok, tl):
    n_tiles = lp // tl
    hb = tl // POOL_HALO
    n_hblk = lp // POOL_HALO
    seq_len = N_META + n_tok
    tile_fn = functools.partial(_final_tile, lambda_init=lambda_init, seq_len=seq_len, tl=tl)
    first = isinstance(h, tuple)
    if first:
        tile_fn = _with_token_rows(tile_fn, seq_len, tl)
    h_specs = _token_row_specs(tl, h[1].shape[0]) if first else [_row_spec(tl, D_MODEL)]
    h_args = (h[0], h[1], h[1], h[1], h[1]) if first else (h,)
    kern = _row_tile_kernel(tile_fn, 1, 1, tl, seq_len)
    in_specs = h_specs + [
        pl.BlockSpec((4, 2, tl, LANES), lambda i: (0, 0, i, 0)),
        pl.BlockSpec((2, 2, tl, LANES), lambda i: (0, 0, i, 0)),
        pl.BlockSpec((4, 1, tl, LANES), lambda i: (0, 0, i, 0)),
        _row_spec(tl, BRANCH_W),
        pl.BlockSpec((POOL_HALO, BRANCH_W), lambda i: (jnp.maximum(i * hb - 1, 0), 0)),
        pl.BlockSpec((POOL_HALO, BRANCH_W), lambda i: (jnp.minimum((i + 1) * hb, n_hblk - 1), 0)),
        _row_spec(tl, D_MODEL),
        _const_spec((D_MODEL, D_MODEL)),
        _const_spec((4, A_QK)),
        _const_spec((1, LANES)),
        _const_spec((4, LANES, LANES)),
        _const_spec((1, BRANCH_W)),
    ]
    return pl.pallas_call(
        kern,
        grid=(n_tiles,),
        in_specs=in_specs,
        out_specs=_row_spec(tl, D_MODEL),
        out_shape=jax.ShapeDtypeStruct((lp, D_MODEL), F32),
        scratch_shapes=[pltpu.VMEM((tl + 2 * POOL_HALO, BRANCH_W), F32)],
        compiler_params=pltpu.CompilerParams(
            dimension_semantics=("arbitrary",), vmem_limit_bytes=VMEM_LIMIT),
        name="final",
    )(*h_args, oa, ob, oc, du, du, du, sg, lw["wout"], lw["lam"], lw["subln"], lw["wg"], lw["dsc"])


def _layer_weights(i, norm_w, w_in, w_in_bf16, w_out, a_q_norm, a_k_norm, a_lambda, a_subln, b_q_norm,
                   b_k_norm, c_q_lat_norm, c_kv_lat_norm, c_w_uq, c_w_ukv, c_q_norm, c_k_norm,
                   d_w_group, d_scale):
    wtail = w_in[i, :, P_TAIL:].astype(BF16)
    uq = c_w_uq[i].reshape(C_Q_LORA, 4, C_NOPE + C_ROPE)
    uq = jnp.pad(uq, ((0, 0), (0, 0), (0, 2 * LANES - C_NOPE - C_ROPE)))
    wuq = uq.reshape(C_Q_LORA, 4 * 2 * LANES).astype(BF16)
    row = lambda v: v.reshape(1, -1).astype(F32)
    a_scale = (A_QK ** -0.5) * LOG2E
    tiny = jnp.finfo(F32).tiny
    q_mag = jnp.maximum(jnp.max(jnp.abs(a_q_norm[i])).astype(F32) * a_scale, tiny)
    k_mag = jnp.maximum(jnp.max(jnp.abs(a_k_norm[i])).astype(F32), tiny)
    shift = jnp.clip(jnp.round(0.5 * (jnp.log2(k_mag) - jnp.log2(q_mag))), -60.0, 60.0)
    aqf = jnp.full((1, LANES), a_scale, F32) * jnp.exp2(shift)
    akf = jnp.full((1, LANES), 1.0, F32) * jnp.exp2(-shift)
    pad_row = lambda v: jnp.pad(v.astype(F32), (0, LANES - v.shape[0])).reshape(1, LANES)
    return dict(
        layer=i, nw=row(norm_w[i]), win=w_in_bf16, wtail=wtail, wuq=wuq, aqf=aqf, akf=akf, wukv=c_w_ukv[i].astype(BF16),
        aqn=row(jnp.tile(a_q_norm[i], 2)), akn=row(jnp.tile(a_k_norm[i], 2)),
        bqn=row(b_q_norm[i]), bkn=row(b_k_norm[i]),
        cql=row(c_q_lat_norm[i]), ckvl=row(c_kv_lat_norm[i]),
        cqnn=row(c_q_norm[i][:C_NOPE]), cqnr=pad_row(c_q_norm[i][C_NOPE:]),
        cknn=row(c_k_norm[i][:C_NOPE]), cknr=pad_row(c_k_norm[i][C_NOPE:]),
        wout=w_out[i].astype(BF16), lam=a_lambda[i].astype(F32), subln=row(a_subln[i]),
        wg=d_w_group[i].astype(BF16), dsc=row(d_scale[i]),
    )


def kernel(x, meta_tokens, norm_w, w_in, w_out, a_q_norm, a_k_norm, a_lambda, a_subln, b_q_norm,
           b_k_norm, c_q_lat_norm, c_kv_lat_norm, c_w_uq, c_w_ukv, c_q_norm, c_k_norm,
           d_w_group, d_scale):
    b, n_tok, d = x.shape
    assert b == 1 and d == D_MODEL
    lp = n_tok + SEQ_PAD
    tl = ROW_TILE
    ck = KV_CHUNK
    assert lp % tl == 0 and lp % ATTN_ROWS == 0 and n_tok % GRID_W == 0
    assert n_tok % (2 * ck) == 0 and (n_tok // (2 * ck) - 2) % 3 == 0
    depth = norm_w.shape[0]

    assert tl % 3 == 0 and n_tok % (tl // 3) == 0 and tl % N_META == 0
    h = (meta_tokens.astype(x.dtype), x[0])
    tables = _rope_lane_tables(n_tok, lp)

    w_in_bf16 = w_in.astype(BF16)
    for i in range(depth):
        lambda_init = 0.8 - 0.6 * math.exp(-0.3 * i)
        lw = _layer_weights(i, norm_w, w_in, w_in_bf16, w_out, a_q_norm, a_k_norm, a_lambda, a_subln,
                            b_q_norm, b_k_norm, c_q_lat_norm, c_kv_lat_norm, c_w_uq, c_w_ukv,
                            c_q_norm, c_k_norm, d_w_group, d_scale)
        qa, ka, va, qb, kb, vb, qc, kc, vc, sg, du = _proj_call(h, lw, tables, lp, tl, N_META + n_tok)
        oa = _attn_call(qa, ka, va, "A", lp, n_tok, ATTN_ROWS // 2, ck)
        ob = _attn_call(qb, kb, vb, "B", lp, n_tok, ATTN_ROWS // 2, ck)
        oc = _attn_call(qc, kc, vc, "C", lp, n_tok, ATTN_ROWS, ck)
        h = _final_call(h, oa, ob, oc, du, sg, lw, lambda_init, lp, n_tok, tl)
    return h[N_META:N_META + n_tok][None]
```

```python
import functools
import math

import jax
import jax.numpy as jnp
from jax import lax
from jax.experimental import pallas as pl
from jax.experimental.pallas import tpu as pltpu

F32 = jnp.float32
BF16 = jnp.bfloat16
F8 = jnp.float8_e4m3fn

D_MODEL = 2048
N_META = 16
GRID_W = 64
EPS = 1e-6
ROPE_THETA = 500000.0
AXIAL_THETA = 10000.0
BRANCH_W = 512
LANES = 128
SEQ_PAD = 512
ROW_TILE = 384
ATTN_ROWS = 768
KV_CHUNK = 1024
PV_ROW_BLOCKS = 2
A_QK = 64
A_ROT = 16
B_HD = 128
C_NOPE = 128
C_ROPE = 64
C_Q_LORA = 384
C_KV_LORA = 256
POOL_WINDOWS = (2, 4, 8, 16)
POOL_HALO = 8
LOG2E = 1.4426950408889634
NEG_BIG = -1e30
VMEM_LIMIT = 56 * 1024 * 1024

P_AQ, P_AV, P_AG = 0, 1024, 1536
P_BQ, P_BV, P_BG = 2048, 2816, 3072
P_CQ = 3584
P_TAIL = 4288
IN_COLS = 5824
P_HEAD_END = P_CQ + C_Q_LORA + C_KV_LORA + LANES


def _mean_sq(x, n):
    return jnp.sum(x * x, axis=-1, keepdims=True) * (1.0 / n)


def _half_mean_sq(x):
    xx = x * x
    lower = lax.broadcasted_iota(jnp.int32, x.shape, 1) < (LANES // 2)
    lo = jnp.sum(jnp.where(lower, xx, 0.0), axis=-1, keepdims=True)
    hi = jnp.sum(jnp.where(lower, 0.0, xx), axis=-1, keepdims=True)
    return jnp.where(lower, lo, hi) * (2.0 / LANES)


def _rope(x, c, s, half):
    lane = lax.broadcasted_iota(jnp.int32, x.shape, 1)
    fwd = pltpu.roll(x, LANES - half, axis=1)
    bwd = pltpu.roll(x, half, axis=1)
    partner = jnp.where((lane % (2 * half)) < half, fwd, bwd)
    return x * c + partner * s


def _proj_tile(h_ref, nw_ref, win_ref, wtail_ref, wuq_ref, wukv_ref,
               aqf_ref, akf_ref, aqn_ref, akn_ref, bqn_ref, bkn_ref, cql_ref, ckvl_ref,
               cqnn_ref, cqnr_ref, cknn_ref, cknr_ref,
               ca_ref, sa_ref, cb_ref, sb_ref, cc_ref, sc_ref,
               qa_ref, ka_ref, va_ref, qb_ref, kb_ref, vb_ref,
               qc_ref, kc_ref, vc_ref, sg_ref, du_ref):
    h = h_ref[...]
    ms = jnp.mean(h * h, axis=-1, keepdims=True)
    hb = (h * lax.rsqrt(ms + EPS) * nw_ref[...]).astype(BF16)

    def mm(lo, hi):
        return jnp.dot(hb, win_ref[:, lo:hi], preferred_element_type=F32)

    def mm_tail(blk):
        return jnp.dot(hb, wtail_ref[:, blk * BRANCH_W:(blk + 1) * BRANCH_W],
                       preferred_element_type=F32)

    ca, sa = ca_ref[...], sa_ref[...]
    cb, sb = cb_ref[...], sb_ref[...]
    cc, sc = cc_ref[...], sc_ref[...]

    pa = mm(P_AQ, P_AV)
    for blk in range(8):
        x = pa[:, blk * LANES:(blk + 1) * LANES]
        w = aqn_ref[...] if blk < 4 else akn_ref[...]
        x = x * lax.rsqrt(_half_mean_sq(x) + EPS) * w
        x = _rope(x, ca, sa, A_ROT // 2)
        if blk < 4:
            qa_ref[:, blk * LANES:(blk + 1) * LANES] = (x * aqf_ref[...]).astype(BF16)
        else:
            kh, kl = _f8_split((x * akf_ref[...]).astype(BF16).astype(F32))
            kl = kl * (1.0 / 16.0)
            khr = pltpu.roll(kh, A_QK, axis=1)
            klr = pltpu.roll(kl, A_QK, axis=1)
            lower = lax.broadcasted_iota(jnp.int32, kh.shape, 1) < A_QK
            parts = [jnp.where(lower, kh, khr * 0.0625), jnp.where(lower, kl, klr * 0.0625),
                     jnp.where(lower, khr, kh * 0.0625), jnp.where(lower, klr, kl * 0.0625)]
            for b, part in enumerate(parts):
                c0 = ((blk - 4) * 4 + b) * LANES
                ka_ref[:, c0:c0 + LANES] = part.astype(F8)
    va_ref[...] = mm(P_AV, P_AG).astype(BF16)

    pb = mm(P_BQ, P_BV)
    b_scale = (B_HD ** -0.5) * LOG2E
    for blk in range(6):
        x = pb[:, blk * LANES:(blk + 1) * LANES]
        w = bqn_ref[...] if blk < 4 else bkn_ref[...]
        x = x * lax.rsqrt(_mean_sq(x, B_HD) + EPS) * w
        x = _rope(x, cb, sb, B_HD // 4)
        if blk < 4:
            qb_ref[:, blk * LANES:(blk + 1) * LANES] = (x * b_scale).astype(BF16)
        else:
            kb_ref[:, (blk - 4) * LANES:(blk - 3) * LANES] = x.astype(BF16)
    vb_ref[...] = mm(P_BV, P_BG).astype(BF16)

    pc = mm(P_CQ, P_HEAD_END)
    cq = pc[:, 0:C_Q_LORA]
    cq = cq * lax.rsqrt(jnp.mean(cq * cq, axis=-1, keepdims=True) + EPS) * cql_ref[...]
    qfull = jnp.dot(cq.astype(BF16), wuq_ref[...], preferred_element_type=F32)
    ckv = pc[:, C_Q_LORA:C_Q_LORA + C_KV_LORA]
    ckv = ckv * lax.rsqrt(jnp.mean(ckv * ckv, axis=-1, keepdims=True) + EPS) * ckvl_ref[...]
    kvfull = jnp.dot(ckv.astype(BF16), wukv_ref[...], preferred_element_type=F32)
    kr = pc[:, C_Q_LORA + C_KV_LORA:]
    kr = jnp.where(lax.broadcasted_iota(jnp.int32, kr.shape, 1) < C_ROPE, kr, 0.0)
    kr = kr * lax.rsqrt(_mean_sq(kr, C_ROPE) + EPS) * cknr_ref[...]
    kr = _rope(kr, cc, sc, C_ROPE // 2).astype(BF16)
    c_scale = ((C_NOPE + C_ROPE) ** -0.5) * LOG2E
    for hd in range(4):
        base = hd * 2 * LANES
        qn = qfull[:, base:base + LANES]
        qn = qn * lax.rsqrt(_mean_sq(qn, C_NOPE) + EPS) * cqnn_ref[...]
        qr = qfull[:, base + LANES:base + 2 * LANES]
        qr = qr * lax.rsqrt(_mean_sq(qr, C_ROPE) + EPS) * cqnr_ref[...]
        qr = _rope(qr, cc, sc, C_ROPE // 2)
        qc_ref[:, base:base + LANES] = (qn * c_scale).astype(BF16)
        qc_ref[:, base + LANES:base + 2 * LANES] = (qr * c_scale).astype(BF16)
        kn = kvfull[:, base:base + LANES]
        kn = kn * lax.rsqrt(_mean_sq(kn, C_NOPE) + EPS) * cknn_ref[...]
        kc_ref[:, base:base + LANES] = kn.astype(BF16)
        kc_ref[:, base + LANES:base + 2 * LANES] = kr
        vc_ref[:, hd * LANES:(hd + 1) * LANES] = kvfull[:, base + LANES:base + 2 * LANES].astype(BF16)

    gates = [mm(P_AG, P_BQ), mm(P_BG, P_CQ), mm_tail(0), mm_tail(2)]
    for blk, g in enumerate(gates):
        sg_ref[:, blk * BRANCH_W:(blk + 1) * BRANCH_W] = jax.nn.silu(g).astype(BF16)
    du_ref[...] = mm_tail(1)


def _f8_split(x):
    hi = x.astype(F8).astype(F32)
    return hi, (x - hi) * 16.0


def _lane_repeat(x, n):
    return jnp.concatenate([x] * n, axis=1)


def _attn_tile(q_ref, k_ref, o_ref, vx_sc, s_sc, mc_sc, mr_sc, m_sc, acc_sc, *,
               mode, tq, ck, n_main, tail_valid):
    if mode == "A":
        qh, ql = _f8_split(q_ref[...].astype(F32))
        lower = lax.broadcasted_iota(jnp.int32, qh.shape, 1) < A_QK
        b0 = jnp.where(lower, qh, pltpu.roll(ql, A_QK, axis=1)).astype(F8)
        b1 = jnp.where(lower, pltpu.roll(qh, A_QK, axis=1), ql).astype(F8)
        q8 = [jnp.concatenate([b, b], axis=1) for b in (b0, b1)]
        rows = 2 * tq
    elif mode == "B":
        qs = jnp.concatenate([q_ref[:, 0:LANES], q_ref[:, LANES:2 * LANES]], axis=0)
        rows = qs.shape[0]
    else:
        qs = q_ref[...]
        rows = qs.shape[0]
    n_grp = rows // tq

    m_sc[...] = jnp.full(m_sc.shape, NEG_BIG, F32)
    acc_sc[...] = jnp.zeros(acc_sc.shape, F32)

    contract_last = (((1,), (1,)), ((), ()))

    def scores(start, width):
        if mode == "A":
            parts = [lax.dot_general(q8[m], k_ref[pl.ds(start, width), m * 2 * LANES:(m + 1) * 2 * LANES],
                                     contract_last, preferred_element_type=F32) for m in range(2)]
            return jnp.concatenate(parts, axis=0)
        return lax.dot_general(qs, k_ref[pl.ds(start, width), :], contract_last,
                               preferred_element_type=F32)

    def row_max(s):
        return jnp.broadcast_to(jnp.max(s, axis=1, keepdims=True), (rows, LANES))

    def lane_max(s):
        parts = [s[:, t * LANES:(t + 1) * LANES] for t in range(s.shape[1] // LANES)]
        return functools.reduce(jnp.maximum, parts)

    def qk_stage(j, slot):
        s = scores(pl.multiple_of(j * ck, ck), ck)
        s_sc[slot] = s
        mc_sc[slot] = lane_max(s)

    def softmax_pv(s_parts, m_cur, vx, after_block=()):
        m_prev = m_sc[...]
        m_new = jnp.maximum(m_prev, m_cur)
        alpha = jnp.exp2(m_prev - m_new)
        m_sc[...] = m_new
        assert len(after_block) <= PV_ROW_BLOCKS
        blk = rows // PV_ROW_BLOCKS
        for r0 in range(0, rows, blk):
            mr = m_new[r0:r0 + blk]
            p = jnp.concatenate(
                [jnp.exp2(s[r0:r0 + blk] - _lane_repeat(mr, s.shape[1] // LANES)).astype(BF16)
                 for s in s_parts], axis=1)
            pv = jnp.dot(p, vx, preferred_element_type=F32)
            acc_sc[r0:r0 + blk, :] = _lane_repeat(alpha[r0:r0 + blk], 2) * acc_sc[r0:r0 + blk, :] + pv
            if r0 // blk < len(after_block):
                after_block[r0 // blk]()

    def qk_pair(p, r):
        qk_stage(2 * p, 2 * r)
        qk_stage(2 * p + 1, 2 * r + 1)

    def reduce_pair(r):
        mr_sc[r] = row_max(jnp.maximum(mc_sc[2 * r], mc_sc[2 * r + 1]))

    def pv_pair(p, r, after_block=()):
        vx = vx_sc[pl.ds(pl.multiple_of(p * 2 * ck, 2 * ck), 2 * ck), :]
        softmax_pv([s_sc[2 * r], s_sc[2 * r + 1]], mr_sc[r], vx, after_block)

    n_pairs = n_main // 2
    qk_pair(0, 0)
    qk_pair(1, 1)
    reduce_pair(0)

    def body(i, carry):
        for r in range(3):
            p = 3 * i + r
            reduce_pair((r + 1) % 3)
            nxt = (r + 2) % 3
            if mode == "A":
                pv_pair(p, r, [functools.partial(qk_stage, 2 * (p + 2), 2 * nxt),
                               functools.partial(qk_stage, 2 * (p + 2) + 1, 2 * nxt + 1)])
            else:
                pv_pair(p, r)
                qk_pair(p + 2, nxt)
        return carry

    lax.fori_loop(0, (n_pairs - 2) // 3, body, 0)
    r = (n_pairs - 2) % 3
    pv_pair(n_pairs - 2, r)
    reduce_pair((r + 1) % 3)
    t0 = n_main * ck
    s_tail = scores(t0, LANES)
    col = lax.broadcasted_iota(jnp.int32, s_tail.shape, 1)
    s_tail = jnp.where(col < tail_valid, s_tail, NEG_BIG)
    pv_pair(n_pairs - 1, (r + 1) % 3)
    softmax_pv([s_tail], row_max(s_tail), vx_sc[t0:t0 + LANES, :])

    o = acc_sc[:, 0:LANES] / acc_sc[:, LANES:2 * LANES]
    for g in range(n_grp):
        o_ref[g] = o[g * tq:(g + 1) * tq]


def _attn_kernel(q_ref, k_ref, v_ref, o_ref, vx_sc, *scratch, tq, q_valid, **tile_args):
    lp = v_ref.shape[0]

    @pl.when(pl.program_id(1) == 0)
    def _():
        def fill(c, carry):
            r = pl.multiple_of(c * tq, tq)
            vx_sc[pl.ds(r, tq), 0:LANES] = v_ref[pl.ds(r, tq), :]
            vx_sc[pl.ds(r, tq), LANES:2 * LANES] = jnp.ones((tq, LANES), BF16)
            return carry
        lax.fori_loop(0, lp // tq, fill, 0)

    has_rows = pl.program_id(1) * tq < q_valid

    @pl.when(has_rows)
    def _():
        _attn_tile(q_ref, k_ref, o_ref, vx_sc, *scratch, tq=tq, **tile_args)

    @pl.when(jnp.logical_not(has_rows))
    def _():
        o_ref[...] = jnp.zeros(o_ref.shape, F32)


def _final_tile(h_ref, oa_ref, ob_ref, oc_ref, du_ref, dup_ref, dun_ref, sg_ref,
                wout_ref, lam_ref, subln_ref, wg_ref, dsc_ref, out_ref, ext_sc, *,
                lambda_init, seq_len, tl):
    i = pl.program_id(0)

    lp = lam_ref[...]
    lam = (jnp.exp(jnp.sum(lp[0:1, :] * lp[1:2, :], axis=-1, keepdims=True))
           - jnp.exp(jnp.sum(lp[2:3, :] * lp[3:4, :], axis=-1, keepdims=True)) + lambda_init)

    ys = []
    for hd in range(4):
        d = oa_ref[hd, 0] - lam * oa_ref[hd, 1]
        d = d * lax.rsqrt(_mean_sq(d, LANES) + EPS) * subln_ref[...]
        d = d * (1.0 - lambda_init)
        ys.append(d * sg_ref[:, hd * LANES:(hd + 1) * LANES].astype(F32))
    for g in range(2):
        for r in range(2):
            c = 4 + 2 * g + r
            ys.append(ob_ref[g, r] * sg_ref[:, c * LANES:(c + 1) * LANES].astype(F32))
    for hd in range(4):
        c = 8 + hd
        ys.append(oc_ref[hd, 0] * sg_ref[:, c * LANES:(c + 1) * LANES].astype(F32))

    row0 = i * tl
    def masked(x, first_row):
        r = first_row + lax.broadcasted_iota(jnp.int32, x.shape, 0)
        return jnp.where((r >= 0) & (r < seq_len), x, 0.0)
    ext_sc[0:POOL_HALO, :] = masked(dup_ref[...], row0 - POOL_HALO)
    ext_sc[POOL_HALO:POOL_HALO + tl, :] = masked(du_ref[...], row0)
    ext_sc[POOL_HALO + tl:2 * POOL_HALO + tl, :] = masked(dun_ref[...], row0 + tl)
    t = row0 + lax.broadcasted_iota(jnp.int32, (tl, LANES), 0)
    for g, w in enumerate(POOL_WINDOWS):
        cols = slice(g * LANES, (g + 1) * LANES)
        acc = ext_sc[POOL_HALO - w // 2:POOL_HALO - w // 2 + tl, cols]
        for j in range(-w // 2 + 1, w // 2):
            acc = acc + ext_sc[POOL_HALO + j:POOL_HALO + j + tl, cols]
        cnt = jnp.minimum(t + w // 2, seq_len) - jnp.maximum(t - w // 2, 0)
        cnt = jnp.maximum(cnt, 1).astype(F32)
        pooled = acc / cnt - ext_sc[POOL_HALO:POOL_HALO + tl, cols]
        mixed = jnp.dot(pooled.astype(BF16), wg_ref[g], preferred_element_type=F32)
        c = 12 + g
        ys.append(mixed * dsc_ref[:, cols] * sg_ref[:, c * LANES:(c + 1) * LANES].astype(F32))

    y = jnp.concatenate([v.astype(BF16) for v in ys], axis=1)
    out_ref[...] = h_ref[...] + jnp.dot(y, wout_ref[...], preferred_element_type=F32)


def _rope_lane_tables(n_tok, lp):
    l = N_META + n_tok
    rows = n_tok // GRID_W
    lane = jnp.arange(LANES, dtype=jnp.int32)

    def inv_freq(idx, dim, theta):
        return theta ** (-(2 * idx).astype(F32) / dim)

    def padded(v):
        return jnp.pad(v.astype(F32), (0, lp - l))[:, None]

    pos = padded(jnp.arange(l, dtype=jnp.int32))
    row_pos = padded(jnp.concatenate([jnp.full((N_META,), -1, jnp.int32),
                                      jnp.repeat(jnp.arange(rows, dtype=jnp.int32), GRID_W)]))
    col_pos = padded(jnp.concatenate([jnp.arange(N_META, dtype=jnp.int32),
                                      jnp.tile(jnp.arange(GRID_W, dtype=jnp.int32), rows)]))

    ja = lane % A_QK
    inv_a = jnp.where(ja < A_ROT, inv_freq(ja % (A_ROT // 2), A_ROT, ROPE_THETA), 0.0)
    sign_a = jnp.where(ja < A_ROT // 2, -1.0, 1.0).astype(F32)
    ang_a = pos * inv_a[None, :]
    jb = lane % (B_HD // 2)
    inv_b = inv_freq(jb % (B_HD // 4), B_HD // 2, AXIAL_THETA)
    inv_r = jnp.where(lane < B_HD // 2, inv_b, 0.0)
    inv_c = jnp.where(lane < B_HD // 2, 0.0, inv_b)
    sign_b = jnp.where(jb < B_HD // 4, -1.0, 1.0).astype(F32)
    ang_b = row_pos * inv_r[None, :] + col_pos * inv_c[None, :]
    inv_m = jnp.where(lane < C_ROPE, inv_freq(lane % (C_ROPE // 2), C_ROPE, ROPE_THETA), 0.0)
    sign_m = jnp.where(lane < C_ROPE // 2, -1.0, 1.0).astype(F32)
    ang_m = pos * inv_m[None, :]
    return (jnp.cos(ang_a), jnp.sin(ang_a) * sign_a[None, :],
            jnp.cos(ang_b), jnp.sin(ang_b) * sign_b[None, :],
            jnp.cos(ang_m), jnp.sin(ang_m) * sign_m[None, :])


def _row_tile_kernel(tile_fn, n_out, n_scratch, tl, seq_len):
    def kern(*refs):
        outs = refs[len(refs) - n_scratch - n_out:len(refs) - n_scratch]
        has_rows = pl.program_id(0) * tl < seq_len

        @pl.when(has_rows)
        def _():
            tile_fn(*refs)

        @pl.when(jnp.logical_not(has_rows))
        def _():
            for o in outs:
                o[...] = jnp.zeros(o.shape, o.dtype)
    return kern


class _TokenRows:
    def __init__(self, meta_ref, prev_ref, body_refs, seq_len, tl):
        self._parts = (meta_ref, prev_ref, body_refs)
        self._seq_len, self._tl = seq_len, tl

    def __getitem__(self, idx):
        meta_ref, prev_ref, body_refs = self._parts
        i = pl.program_id(0)
        head = jnp.where(i == 0, meta_ref[...], prev_ref[...])
        body = [r[...] for r in body_refs]
        body[-1] = body[-1][:body[-1].shape[0] - N_META]
        rows = jnp.concatenate([head] + body, axis=0)
        r_nat = i * self._tl + lax.broadcasted_iota(jnp.int32, rows.shape, 0)
        return jnp.where(r_nat < self._seq_len, rows, 0.0)


def _with_token_rows(tile_fn, seq_len, tl):
    def fn(meta_ref, prev_ref, b0_ref, b1_ref, b2_ref, *rest):
        tile_fn(_TokenRows(meta_ref, prev_ref, (b0_ref, b1_ref, b2_ref), seq_len, tl), *rest)
    return fn


def _token_row_specs(tl, n_tok):
    sub = tl // 3
    n_sub = n_tok // sub
    return [_const_spec((N_META, D_MODEL)),
            pl.BlockSpec((N_META, D_MODEL),
                         lambda i: (jnp.clip(i * (tl // N_META) - 1, 0, n_tok // N_META - 1), 0)),
            pl.BlockSpec((sub, D_MODEL), lambda i: (jnp.minimum(3 * i, n_sub - 1), 0)),
            pl.BlockSpec((sub, D_MODEL), lambda i: (jnp.minimum(3 * i + 1, n_sub - 1), 0)),
            pl.BlockSpec((sub, D_MODEL), lambda i: (jnp.minimum(3 * i + 2, n_sub - 1), 0))]


def _row_spec(tl, width):
    return pl.BlockSpec((tl, width), lambda i: (i, 0))


def _const_spec(shape):
    nd = len(shape)
    return pl.BlockSpec(shape, lambda i: (0,) * nd, pipeline_mode=pl.Buffered(1))


def _proj_call(h, lw, tables, lp, tl, seq_len):
    n_tiles = lp // tl
    small = [lw["aqf"], lw["akf"], lw["aqn"], lw["akn"], lw["bqn"], lw["bkn"], lw["cql"], lw["ckvl"],
             lw["cqnn"], lw["cqnr"], lw["cknn"], lw["cknr"]]
    first = isinstance(h, tuple)
    h_specs = _token_row_specs(tl, h[1].shape[0]) if first else [_row_spec(tl, D_MODEL)]
    h_args = (h[0], h[1], h[1], h[1], h[1]) if first else (h,)
    tile_fn = _with_token_rows(_proj_tile, seq_len, tl) if first else _proj_tile
    in_specs = (h_specs + [_const_spec((1, D_MODEL)),
                 pl.BlockSpec((None, D_MODEL, P_HEAD_END), lambda i: (lw["layer"], 0, 0),
                              pipeline_mode=pl.Buffered(1)),
                 _const_spec((D_MODEL, IN_COLS - P_TAIL)),
                 _const_spec(lw["wuq"].shape),
                 _const_spec(lw["wukv"].shape)]
                + [_const_spec(a.shape) for a in small]
                + [_row_spec(tl, LANES)] * 6)
    out_widths = [512, 2048, 512, 512, 256, 256, 1024, 1024, 512, 2048, 512]
    out_dtypes = [BF16, F8] + [BF16] * 8 + [F32]
    out_shape = [jax.ShapeDtypeStruct((lp, w), dt) for w, dt in zip(out_widths, out_dtypes)]
    out_specs = [_row_spec(tl, w) for w in out_widths]
    return pl.pallas_call(
        _row_tile_kernel(tile_fn, len(out_widths), 0, tl, seq_len),
        grid=(n_tiles,),
        in_specs=in_specs,
        out_specs=out_specs,
        out_shape=out_shape,
        compiler_params=pltpu.CompilerParams(
            dimension_semantics=("arbitrary",), vmem_limit_bytes=VMEM_LIMIT),
        name="proj",
    )(*h_args, lw["nw"], lw["win"], lw["wtail"], lw["wuq"], lw["wukv"], *small, *tables)


def _attn_call(q, k, v, mode, lp, n_tok, tq, ck):
    n_heads = {"A": 4, "B": 2, "C": 4}[mode]
    n_grp = {"A": 2, "B": 2, "C": 1}[mode]
    qw = {"A": LANES, "B": 2 * LANES, "C": 2 * LANES}[mode]
    kw = {"A": 4 * LANES, "B": LANES, "C": 2 * LANES}[mode]
    n_main = n_tok // ck
    tail_valid = N_META + n_tok - n_main * ck
    kern = functools.partial(_attn_kernel, mode=mode, tq=tq, ck=ck, n_main=n_main,
                             tail_valid=tail_valid, q_valid=N_META + n_tok)
    return pl.pallas_call(
        kern,
        grid=(n_heads, lp // tq),
        in_specs=[pl.BlockSpec((tq, qw), lambda hd, i: (i, hd)),
                  pl.BlockSpec((lp, kw), lambda hd, i: (0, hd), pipeline_mode=pl.Buffered(1)),
                  pl.BlockSpec((lp, LANES), lambda hd, i: (0, hd), pipeline_mode=pl.Buffered(1))],
        out_specs=pl.BlockSpec((None, n_grp, tq, LANES), lambda hd, i: (hd, 0, i, 0)),
        out_shape=jax.ShapeDtypeStruct((n_heads, n_grp, lp, LANES), F32),
        scratch_shapes=[pltpu.VMEM((lp, 2 * LANES), BF16),
                        pltpu.VMEM((6, n_grp * tq, ck), F32),
                        pltpu.VMEM((6, n_grp * tq, LANES), F32),
                        pltpu.VMEM((3, n_grp * tq, LANES), F32),
                        pltpu.VMEM((n_grp * tq, LANES), F32),
                        pltpu.VMEM((n_grp * tq, 2 * LANES), F32)],
        compiler_params=pltpu.CompilerParams(
            dimension_semantics=("arbitrary", "arbitrary"), vmem_limit_bytes=VMEM_LIMIT),
        name="attn_" + mode,
    )(q, k, v)


def _final_call(h, oa, ob, oc, du, sg, lw, lambda_init, lp, n_tok, tl):
    n_tiles = lp // tl
    hb = tl // POOL_HALO
    n_hblk = lp // POOL_HALO
    seq_len = N_META + n_tok
    tile_fn = functools.partial(_final_tile, lambda_init=lambda_init, seq_len=seq_len, tl=tl)
    first = isinstance(h, tuple)
    if first:
        tile_fn = _with_token_rows(tile_fn, seq_len, tl)
    h_specs = _token_row_specs(tl, h[1].shape[0]) if first else [_row_spec(tl, D_MODEL)]
    h_args = (h[0], h[1], h[1], h[1], h[1]) if first else (h,)
    kern = _row_tile_kernel(tile_fn, 1, 1, tl, seq_len)
    in_specs = h_specs + [
        pl.BlockSpec((4, 2, tl, LANES), lambda i: (0, 0, i, 0)),
        pl.BlockSpec((2, 2, tl, LANES), lambda i: (0, 0, i, 0)),
        pl.BlockSpec((4, 1, tl, LANES), lambda i: (0, 0, i, 0)),
        _row_spec(tl, BRANCH_W),
        pl.BlockSpec((POOL_HALO, BRANCH_W), lambda i: (jnp.maximum(i * hb - 1, 0), 0)),
        pl.BlockSpec((POOL_HALO, BRANCH_W), lambda i: (jnp.minimum((i + 1) * hb, n_hblk - 1), 0)),
        _row_spec(tl, D_MODEL),
        _const_spec((D_MODEL, D_MODEL)),
        _const_spec((4, A_QK)),
        _const_spec((1, LANES)),
        _const_spec((4, LANES, LANES)),
        _const_spec((1, BRANCH_W)),
    ]
    return pl.pallas_call(
        kern,
        grid=(n_tiles,),
        in_specs=in_specs,
        out_specs=_row_spec(tl, D_MODEL),
        out_shape=jax.ShapeDtypeStruct((lp, D_MODEL), F32),
        scratch_shapes=[pltpu.VMEM((tl + 2 * POOL_HALO, BRANCH_W), F32)],
        compiler_params=pltpu.CompilerParams(
            dimension_semantics=("arbitrary",), vmem_limit_bytes=VMEM_LIMIT),
        name="final",
    )(*h_args, oa, ob, oc, du, du, du, sg, lw["wout"], lw["lam"], lw["subln"], lw["wg"], lw["dsc"])


def _layer_weights(i, norm_w, w_in, w_in_bf16, w_out, a_q_norm, a_k_norm, a_lambda, a_subln, b_q_norm,
                   b_k_norm, c_q_lat_norm, c_kv_lat_norm, c_w_uq, c_w_ukv, c_q_norm, c_k_norm,
                   d_w_group, d_scale):
    wtail = w_in[i, :, P_TAIL:].astype(BF16)
    uq = c_w_uq[i].reshape(C_Q_LORA, 4, C_NOPE + C_ROPE)
    uq = jnp.pad(uq, ((0, 0), (0, 0), (0, 2 * LANES - C_NOPE - C_ROPE)))
    wuq = uq.reshape(C_Q_LORA, 4 * 2 * LANES).astype(BF16)
    row = lambda v: v.reshape(1, -1).astype(F32)
    a_scale = (A_QK ** -0.5) * LOG2E
    tiny = jnp.finfo(F32).tiny
    q_mag = jnp.maximum(jnp.max(jnp.abs(a_q_norm[i])).astype(F32) * a_scale, tiny)
    k_mag = jnp.maximum(jnp.max(jnp.abs(a_k_norm[i])).astype(F32), tiny)
    shift = jnp.clip(jnp.round(0.5 * (jnp.log2(k_mag) - jnp.log2(q_mag))), -60.0, 60.0)
    aqf = jnp.full((1, LANES), a_scale, F32) * jnp.exp2(shift)
    akf = jnp.full((1, LANES), 1.0, F32) * jnp.exp2(-shift)
    pad_row = lambda v: jnp.pad(v.astype(F32), (0, LANES - v.shape[0])).reshape(1, LANES)
    return dict(
        layer=i, nw=row(norm_w[i]), win=w_in_bf16, wtail=wtail, wuq=wuq, aqf=aqf, akf=akf, wukv=c_w_ukv[i].astype(BF16),
        aqn=row(jnp.tile(a_q_norm[i], 2)), akn=row(jnp.tile(a_k_norm[i], 2)),
        bqn=row(b_q_norm[i]), bkn=row(b_k_norm[i]),
        cql=row(c_q_lat_norm[i]), ckvl=row(c_kv_lat_norm[i]),
        cqnn=row(c_q_norm[i][:C_NOPE]), cqnr=pad_row(c_q_norm[i][C_NOPE:]),
        cknn=row(c_k_norm[i][:C_NOPE]), cknr=pad_row(c_k_norm[i][C_NOPE:]),
        wout=w_out[i].astype(BF16), lam=a_lambda[i].astype(F32), subln=row(a_subln[i]),
        wg=d_w_group[i].astype(BF16), dsc=row(d_scale[i]),
    )


def kernel(x, meta_tokens, norm_w, w_in, w_out, a_q_norm, a_k_norm, a_lambda, a_subln, b_q_norm,
           b_k_norm, c_q_lat_norm, c_kv_lat_norm, c_w_uq, c_w_ukv, c_q_norm, c_k_norm,
           d_w_group, d_scale):
    b, n_tok, d = x.shape
    assert b == 1 and d == D_MODEL
    lp = n_tok + SEQ_PAD
    tl = ROW_TILE
    ck = KV_CHUNK
    assert lp % tl == 0 and lp % ATTN_ROWS == 0 and n_tok % GRID_W == 0
    assert n_tok % (2 * ck) == 0 and (n_tok // (2 * ck) - 2) % 3 == 0
    depth = norm_w.shape[0]

    assert tl % 3 == 0 and n_tok % (tl // 3) == 0 and tl % N_META == 0
    h = (meta_tokens.astype(x.dtype), x[0])
    tables = _rope_lane_tables(n_tok, lp)

    w_in_bf16 = w_in[:, :, :P_HEAD_END].astype(BF16)
    for i in range(depth):
        lambda_init = 0.8 - 0.6 * math.exp(-0.3 * i)
        lw = _layer_weights(i, norm_w, w_in, w_in_bf16, w_out, a_q_norm, a_k_norm, a_lambda, a_subln,
                            b_q_norm, b_k_norm, c_q_lat_norm, c_kv_lat_norm, c_w_uq, c_w_ukv,
                            c_q_norm, c_k_norm, d_w_group, d_scale)
        qa, ka, va, qb, kb, vb, qc, kc, vc, sg, du = _proj_call(h, lw, tables, lp, tl, N_META + n_tok)
        oa = _attn_call(qa, ka, va, "A", lp, n_tok, ATTN_ROWS // 2, ck)
        ob = _attn_call(qb, kb, vb, "B", lp, n_tok, ATTN_ROWS // 2, ck)
        oc = _attn_call(qc, kc, vc, "C", lp, n_tok, ATTN_ROWS, ck)
        h = _final_call(h, oa, ob, oc, du, sg, lw, lambda_init, lp, n_tok, tl)
    return h[N_META:N_META + n_tok][None]
```

```python
import functools
import math

import jax
import jax.numpy as jnp
from jax import lax
from jax.experimental import pallas as pl
from jax.experimental.pallas import tpu as pltpu

F32 = jnp.float32
BF16 = jnp.bfloat16
F8 = jnp.float8_e4m3fn

D_MODEL = 2048
N_META = 16
GRID_W = 64
EPS = 1e-6
ROPE_THETA = 500000.0
AXIAL_THETA = 10000.0
BRANCH_W = 512
LANES = 128
SEQ_PAD = 512
ROW_TILE = 384
ATTN_ROWS = 768
KV_CHUNK = 1024
PV_ROW_BLOCKS = 2
A_QK = 64
A_ROT = 16
B_HD = 128
C_NOPE = 128
C_ROPE = 64
C_Q_LORA = 384
C_KV_LORA = 256
POOL_WINDOWS = (2, 4, 8, 16)
POOL_HALO = 8
LOG2E = 1.4426950408889634
NEG_BIG = -1e30
VMEM_LIMIT = 56 * 1024 * 1024

P_AQ, P_AV, P_AG = 0, 1024, 1536
P_BQ, P_BV, P_BG = 2048, 2816, 3072
P_CQ = 3584
P_TAIL = 4288
IN_COLS = 5824
P_HEAD_END = P_CQ + C_Q_LORA + C_KV_LORA + LANES


def _mean_sq(x, n):
    return jnp.sum(x * x, axis=-1, keepdims=True) * (1.0 / n)


def _half_mean_sq(x):
    xx = x * x
    lower = lax.broadcasted_iota(jnp.int32, x.shape, 1) < (LANES // 2)
    lo = jnp.sum(jnp.where(lower, xx, 0.0), axis=-1, keepdims=True)
    hi = jnp.sum(jnp.where(lower, 0.0, xx), axis=-1, keepdims=True)
    return jnp.where(lower, lo, hi) * (2.0 / LANES)


def _rope(x, c, s, half):
    lane = lax.broadcasted_iota(jnp.int32, x.shape, 1)
    fwd = pltpu.roll(x, LANES - half, axis=1)
    bwd = pltpu.roll(x, half, axis=1)
    partner = jnp.where((lane % (2 * half)) < half, fwd, bwd)
    return x * c + partner * s


def _proj_tile(h_ref, nw_ref, win_ref, wtail_ref, wuq_ref, wukv_ref,
               aqf_ref, akf_ref, aqn_ref, akn_ref, bqn_ref, bkn_ref, cql_ref, ckvl_ref,
               cqnn_ref, cqnr_ref, cknn_ref, cknr_ref,
               ca_ref, sa_ref, cb_ref, sb_ref, cc_ref, sc_ref,
               qa_ref, ka_ref, va_ref, qb_ref, kb_ref, vb_ref,
               qc_ref, kc_ref, vc_ref, sg_ref, du_ref):
    h = h_ref[...]
    ms = jnp.mean(h * h, axis=-1, keepdims=True)
    hb = (h * lax.rsqrt(ms + EPS) * nw_ref[...]).astype(BF16)

    def mm(lo, hi):
        return jnp.dot(hb, win_ref[:, lo:hi], preferred_element_type=F32)

    def mm_tail(blk):
        return jnp.dot(hb, wtail_ref[:, blk * BRANCH_W:(blk + 1) * BRANCH_W],
                       preferred_element_type=F32)

    ca, sa = ca_ref[...], sa_ref[...]
    cb, sb = cb_ref[...], sb_ref[...]
    cc, sc = cc_ref[...], sc_ref[...]

    pa = mm(P_AQ, P_AV)
    for blk in range(8):
        x = pa[:, blk * LANES:(blk + 1) * LANES]
        w = aqn_ref[...] if blk < 4 else akn_ref[...]
        x = x * lax.rsqrt(_half_mean_sq(x) + EPS) * w
        x = _rope(x, ca, sa, A_ROT // 2)
        if blk < 4:
            qa_ref[:, blk * LANES:(blk + 1) * LANES] = (x * aqf_ref[...]).astype(BF16)
        else:
            kh, kl = _f8_split((x * akf_ref[...]).astype(BF16).astype(F32))
            kl = kl * (1.0 / 16.0)
            khr = pltpu.roll(kh, A_QK, axis=1)
            klr = pltpu.roll(kl, A_QK, axis=1)
            lower = lax.broadcasted_iota(jnp.int32, kh.shape, 1) < A_QK
            parts = [jnp.where(lower, kh, khr * 0.0625), jnp.where(lower, kl, klr * 0.0625),
                     jnp.where(lower, khr, kh * 0.0625), jnp.where(lower, klr, kl * 0.0625)]
            for b, part in enumerate(parts):
                c0 = ((blk - 4) * 4 + b) * LANES
                ka_ref[:, c0:c0 + LANES] = part.astype(F8)
    va_ref[...] = mm(P_AV, P_AG).astype(BF16)

    pb = mm(P_BQ, P_BV)
    b_scale = (B_HD ** -0.5) * LOG2E
    for blk in range(6):
        x = pb[:, blk * LANES:(blk + 1) * LANES]
        w = bqn_ref[...] if blk < 4 else bkn_ref[...]
        x = x * lax.rsqrt(_mean_sq(x, B_HD) + EPS) * w
        x = _rope(x, cb, sb, B_HD // 4)
        if blk < 4:
            qb_ref[:, blk * LANES:(blk + 1) * LANES] = (x * b_scale).astype(BF16)
        else:
            kb_ref[:, (blk - 4) * LANES:(blk - 3) * LANES] = x.astype(BF16)
    vb_ref[...] = mm(P_BV, P_BG).astype(BF16)

    pc = mm(P_CQ, P_HEAD_END)
    cq = pc[:, 0:C_Q_LORA]
    cq = cq * lax.rsqrt(jnp.mean(cq * cq, axis=-1, keepdims=True) + EPS) * cql_ref[...]
    qfull = jnp.dot(cq.astype(BF16), wuq_ref[...], preferred_element_type=F32)
    ckv = pc[:, C_Q_LORA:C_Q_LORA + C_KV_LORA]
    ckv = ckv * lax.rsqrt(jnp.mean(ckv * ckv, axis=-1, keepdims=True) + EPS) * ckvl_ref[...]
    kvfull = jnp.dot(ckv.astype(BF16), wukv_ref[...], preferred_element_type=F32)
    kr = pc[:, C_Q_LORA + C_KV_LORA:]
    kr = jnp.where(lax.broadcasted_iota(jnp.int32, kr.shape, 1) < C_ROPE, kr, 0.0)
    kr = kr * lax.rsqrt(_mean_sq(kr, C_ROPE) + EPS) * cknr_ref[...]
    kr = _rope(kr, cc, sc, C_ROPE // 2).astype(BF16)
    c_scale = ((C_NOPE + C_ROPE) ** -0.5) * LOG2E
    for hd in range(4):
        base = hd * 2 * LANES
        qn = qfull[:, base:base + LANES]
        qn = qn * lax.rsqrt(_mean_sq(qn, C_NOPE) + EPS) * cqnn_ref[...]
        qr = qfull[:, base + LANES:base + 2 * LANES]
        qr = qr * lax.rsqrt(_mean_sq(qr, C_ROPE) + EPS) * cqnr_ref[...]
        qr = _rope(qr, cc, sc, C_ROPE // 2)
        qc_ref[:, base:base + LANES] = (qn * c_scale).astype(BF16)
        qc_ref[:, base + LANES:base + 2 * LANES] = (qr * c_scale).astype(BF16)
        kn = kvfull[:, base:base + LANES]
        kn = kn * lax.rsqrt(_mean_sq(kn, C_NOPE) + EPS) * cknn_ref[...]
        kc_ref[:, base:base + LANES] = kn.astype(BF16)
        kc_ref[:, base + LANES:base + 2 * LANES] = kr
        vc_ref[:, hd * LANES:(hd + 1) * LANES] = kvfull[:, base + LANES:base + 2 * LANES].astype(BF16)

    gates = [mm(P_AG, P_BQ), mm(P_BG, P_CQ), mm_tail(0), mm_tail(2)]
    for blk, g in enumerate(gates):
        sg_ref[:, blk * BRANCH_W:(blk + 1) * BRANCH_W] = jax.nn.silu(g).astype(BF16)
    du_ref[...] = mm_tail(1)


def _f8_split(x):
    hi = x.astype(F8).astype(F32)
    return hi, (x - hi) * 16.0


def _lane_repeat(x, n):
    return jnp.concatenate([x] * n, axis=1)


def _attn_tile(q_ref, k_ref, o_ref, vx_sc, s_sc, mc_sc, mr_sc, m_sc, acc_sc, *,
               mode, tq, ck, n_main, tail_valid):
    if mode == "A":
        qh, ql = _f8_split(q_ref[...].astype(F32))
        lower = lax.broadcasted_iota(jnp.int32, qh.shape, 1) < A_QK
        b0 = jnp.where(lower, qh, pltpu.roll(ql, A_QK, axis=1)).astype(F8)
        b1 = jnp.where(lower, pltpu.roll(qh, A_QK, axis=1), ql).astype(F8)
        q8 = [jnp.concatenate([b, b], axis=1) for b in (b0, b1)]
        rows = 2 * tq
    elif mode == "B":
        qs = jnp.concatenate([q_ref[:, 0:LANES], q_ref[:, LANES:2 * LANES]], axis=0)
        rows = qs.shape[0]
    else:
        qs = q_ref[...]
        rows = qs.shape[0]
    n_grp = rows // tq

    m_sc[...] = jnp.full(m_sc.shape, NEG_BIG, F32)
    acc_sc[...] = jnp.zeros(acc_sc.shape, F32)

    contract_last = (((1,), (1,)), ((), ()))

    def scores(start, width):
        if mode == "A":
            parts = [lax.dot_general(q8[m], k_ref[pl.ds(start, width), m * 2 * LANES:(m + 1) * 2 * LANES],
                                     contract_last, preferred_element_type=F32) for m in range(2)]
            return jnp.concatenate(parts, axis=0)
        return lax.dot_general(qs, k_ref[pl.ds(start, width), :], contract_last,
                               preferred_element_type=F32)

    def row_max(s):
        return jnp.broadcast_to(jnp.max(s, axis=1, keepdims=True), (rows, LANES))

    def lane_max(s):
        parts = [s[:, t * LANES:(t + 1) * LANES] for t in range(s.shape[1] // LANES)]
        return functools.reduce(jnp.maximum, parts)

    def qk_stage(j, slot):
        s = scores(pl.multiple_of(j * ck, ck), ck)
        s_sc[slot] = s
        mc_sc[slot] = lane_max(s)

    def softmax_pv(s_parts, m_cur, vx, after_block=()):
        m_prev = m_sc[...]
        m_new = jnp.maximum(m_prev, m_cur)
        alpha = jnp.exp2(m_prev - m_new)
        m_sc[...] = m_new
        assert len(after_block) <= PV_ROW_BLOCKS
        blk = rows // PV_ROW_BLOCKS
        for r0 in range(0, rows, blk):
            mr = m_new[r0:r0 + blk]
            p = jnp.concatenate(
                [jnp.exp2(s[r0:r0 + blk] - _lane_repeat(mr, s.shape[1] // LANES)).astype(BF16)
                 for s in s_parts], axis=1)
            pv = jnp.dot(p, vx, preferred_element_type=F32)
            acc_sc[r0:r0 + blk, :] = _lane_repeat(alpha[r0:r0 + blk], 2) * acc_sc[r0:r0 + blk, :] + pv
            if r0 // blk < len(after_block):
                after_block[r0 // blk]()

    def qk_pair(p, r):
        qk_stage(2 * p, 2 * r)
        qk_stage(2 * p + 1, 2 * r + 1)

    def reduce_pair(r):
        mr_sc[r] = row_max(jnp.maximum(mc_sc[2 * r], mc_sc[2 * r + 1]))

    def pv_pair(p, r, after_block=()):
        vx = vx_sc[pl.ds(pl.multiple_of(p * 2 * ck, 2 * ck), 2 * ck), :]
        softmax_pv([s_sc[2 * r], s_sc[2 * r + 1]], mr_sc[r], vx, after_block)

    n_pairs = n_main // 2
    qk_pair(0, 0)
    qk_pair(1, 1)
    reduce_pair(0)

    def body(i, carry):
        for r in range(3):
            p = 3 * i + r
            reduce_pair((r + 1) % 3)
            nxt = (r + 2) % 3
            if mode == "A":
                pv_pair(p, r, [functools.partial(qk_stage, 2 * (p + 2), 2 * nxt),
                               functools.partial(qk_stage, 2 * (p + 2) + 1, 2 * nxt + 1)])
            else:
                pv_pair(p, r)
                qk_pair(p + 2, nxt)
        return carry

    lax.fori_loop(0, (n_pairs - 2) // 3, body, 0)
    r = (n_pairs - 2) % 3
    pv_pair(n_pairs - 2, r)
    reduce_pair((r + 1) % 3)
    t0 = n_main * ck
    s_tail = scores(t0, LANES)
    col = lax.broadcasted_iota(jnp.int32, s_tail.shape, 1)
    s_tail = jnp.where(col < tail_valid, s_tail, NEG_BIG)
    pv_pair(n_pairs - 1, (r + 1) % 3)
    softmax_pv([s_tail], row_max(s_tail), vx_sc[t0:t0 + LANES, :])

    o = acc_sc[:, 0:LANES] / acc_sc[:, LANES:2 * LANES]
    for g in range(n_grp):
        o_ref[g] = o[g * tq:(g + 1) * tq]


def _attn_kernel(q_ref, k_ref, v_ref, o_ref, vx_sc, *scratch, tq, q_valid, **tile_args):
    lp = v_ref.shape[0]

    @pl.when(pl.program_id(1) == 0)
    def _():
        def fill(c, carry):
            r = pl.multiple_of(c * tq, tq)
            vx_sc[pl.ds(r, tq), 0:LANES] = v_ref[pl.ds(r, tq), :]
            vx_sc[pl.ds(r, tq), LANES:2 * LANES] = jnp.ones((tq, LANES), BF16)
            return carry
        lax.fori_loop(0, lp // tq, fill, 0)

    has_rows = pl.program_id(1) * tq < q_valid

    @pl.when(has_rows)
    def _():
        _attn_tile(q_ref, k_ref, o_ref, vx_sc, *scratch, tq=tq, **tile_args)

    @pl.when(jnp.logical_not(has_rows))
    def _():
        o_ref[...] = jnp.zeros(o_ref.shape, F32)


def _final_tile(h_ref, oa_ref, ob_ref, oc_ref, du_ref, dup_ref, dun_ref, sg_ref,
                wout_ref, lam_ref, subln_ref, wg_ref, dsc_ref, out_ref, ext_sc, *,
                lambda_init, seq_len, tl):
    i = pl.program_id(0)

    lp = lam_ref[...]
    lam = (jnp.exp(jnp.sum(lp[0:1, :] * lp[1:2, :], axis=-1, keepdims=True))
           - jnp.exp(jnp.sum(lp[2:3, :] * lp[3:4, :], axis=-1, keepdims=True)) + lambda_init)

    ys = []
    for hd in range(4):
        d = oa_ref[hd, 0] - lam * oa_ref[hd, 1]
        d = d * lax.rsqrt(_mean_sq(d, LANES) + EPS) * subln_ref[...]
        d = d * (1.0 - lambda_init)
        ys.append(d * sg_ref[:, hd * LANES:(hd + 1) * LANES].astype(F32))
    for g in range(2):
        for r in range(2):
            c = 4 + 2 * g + r
            ys.append(ob_ref[g, r] * sg_ref[:, c * LANES:(c + 1) * LANES].astype(F32))
    for hd in range(4):
        c = 8 + hd
        ys.append(oc_ref[hd, 0] * sg_ref[:, c * LANES:(c + 1) * LANES].astype(F32))

    row0 = i * tl
    def masked(x, first_row):
        r = first_row + lax.broadcasted_iota(jnp.int32, x.shape, 0)
        return jnp.where((r >= 0) & (r < seq_len), x, 0.0)
    ext_sc[0:POOL_HALO, :] = masked(dup_ref[...], row0 - POOL_HALO)
    ext_sc[POOL_HALO:POOL_HALO + tl, :] = masked(du_ref[...], row0)
    ext_sc[POOL_HALO + tl:2 * POOL_HALO + tl, :] = masked(dun_ref[...], row0 + tl)
    t = row0 + lax.broadcasted_iota(jnp.int32, (tl, LANES), 0)
    for g, w in enumerate(POOL_WINDOWS):
        cols = slice(g * LANES, (g + 1) * LANES)
        acc = ext_sc[POOL_HALO - w // 2:POOL_HALO - w // 2 + tl, cols]
        for j in range(-w // 2 + 1, w // 2):
            acc = acc + ext_sc[POOL_HALO + j:POOL_HALO + j + tl, cols]
        cnt = jnp.minimum(t + w // 2, seq_len) - jnp.maximum(t - w // 2, 0)
        cnt = jnp.maximum(cnt, 1).astype(F32)
        pooled = acc / cnt - ext_sc[POOL_HALO:POOL_HALO + tl, cols]
        mixed = jnp.dot(pooled.astype(BF16), wg_ref[g], preferred_element_type=F32)
        c = 12 + g
        ys.append(mixed * dsc_ref[:, cols] * sg_ref[:, c * LANES:(c + 1) * LANES].astype(F32))

    y = jnp.concatenate([v.astype(BF16) for v in ys], axis=1)
    out_ref[...] = h_ref[...] + jnp.dot(y, wout_ref[...], preferred_element_type=F32)


def _rope_lane_tables(n_tok, lp):
    l = N_META + n_tok
    rows = n_tok // GRID_W
    lane = jnp.arange(LANES, dtype=jnp.int32)

    def inv_freq(idx, dim, theta):
        return theta ** (-(2 * idx).astype(F32) / dim)

    def padded(v):
        return jnp.pad(v.astype(F32), (0, lp - l))[:, None]

    pos = padded(jnp.arange(l, dtype=jnp.int32))
    row_pos = padded(jnp.concatenate([jnp.full((N_META,), -1, jnp.int32),
                                      jnp.repeat(jnp.arange(rows, dtype=jnp.int32), GRID_W)]))
    col_pos = padded(jnp.concatenate([jnp.arange(N_META, dtype=jnp.int32),
                                      jnp.tile(jnp.arange(GRID_W, dtype=jnp.int32), rows)]))

    ja = lane % A_QK
    inv_a = jnp.where(ja < A_ROT, inv_freq(ja % (A_ROT // 2), A_ROT, ROPE_THETA), 0.0)
    sign_a = jnp.where(ja < A_ROT // 2, -1.0, 1.0).astype(F32)
    ang_a = pos * inv_a[None, :]
    jb = lane % (B_HD // 2)
    inv_b = inv_freq(jb % (B_HD // 4), B_HD // 2, AXIAL_THETA)
    inv_r = jnp.where(lane < B_HD // 2, inv_b, 0.0)
    inv_c = jnp.where(lane < B_HD // 2, 0.0, inv_b)
    sign_b = jnp.where(jb < B_HD // 4, -1.0, 1.0).astype(F32)
    ang_b = row_pos * inv_r[None, :] + col_pos * inv_c[None, :]
    inv_m = jnp.where(lane < C_ROPE, inv_freq(lane % (C_ROPE // 2), C_ROPE, ROPE_THETA), 0.0)
    sign_m = jnp.where(lane < C_ROPE // 2, -1.0, 1.0).astype(F32)
    ang_m = pos * inv_m[None, :]
    return (jnp.cos(ang_a), jnp.sin(ang_a) * sign_a[None, :],
            jnp.cos(ang_b), jnp.sin(ang_b) * sign_b[None, :],
            jnp.cos(ang_m), jnp.sin(ang_m) * sign_m[None, :])


def _row_tile_kernel(tile_fn, n_out, n_scratch, tl, seq_len):
    def kern(*refs):
        outs = refs[len(refs) - n_scratch - n_out:len(refs) - n_scratch]
        has_rows = pl.program_id(0) * tl < seq_len

        @pl.when(has_rows)
        def _():
            tile_fn(*refs)

        @pl.when(jnp.logical_not(has_rows))
        def _():
            for o in outs:
                o[...] = jnp.zeros(o.shape, o.dtype)
    return kern


class _TokenRows:
    def __init__(self, meta_ref, prev_ref, body_refs, seq_len, tl):
        self._parts = (meta_ref, prev_ref, body_refs)
        self._seq_len, self._tl = seq_len, tl

    def __getitem__(self, idx):
        meta_ref, prev_ref, body_refs = self._parts
        i = pl.program_id(0)
        head = jnp.where(i == 0, meta_ref[...], prev_ref[...])
        body = [r[...] for r in body_refs]
        body[-1] = body[-1][:body[-1].shape[0] - N_META]
        rows = jnp.concatenate([head] + body, axis=0)
        r_nat = i * self._tl + lax.broadcasted_iota(jnp.int32, rows.shape, 0)
        return jnp.where(r_nat < self._seq_len, rows, 0.0)


def _with_token_rows(tile_fn, seq_len, tl):
    def fn(meta_ref, prev_ref, b0_ref, b1_ref, b2_ref, *rest):
        tile_fn(_TokenRows(meta_ref, prev_ref, (b0_ref, b1_ref, b2_ref), seq_len, tl), *rest)
    return fn


def _token_row_specs(tl, n_tok):
    sub = tl // 3
    n_sub = n_tok // sub
    return [_const_spec((N_META, D_MODEL)),
            pl.BlockSpec((N_META, D_MODEL),
                         lambda i: (jnp.clip(i * (tl // N_META) - 1, 0, n_tok // N_META - 1), 0)),
            pl.BlockSpec((sub, D_MODEL), lambda i: (jnp.minimum(3 * i, n_sub - 1), 0)),
            pl.BlockSpec((sub, D_MODEL), lambda i: (jnp.minimum(3 * i + 1, n_sub - 1), 0)),
            pl.BlockSpec((sub, D_MODEL), lambda i: (jnp.minimum(3 * i + 2, n_sub - 1), 0))]


def _row_spec(tl, width):
    return pl.BlockSpec((tl, width), lambda i: (i, 0))


def _const_spec(shape):
    nd = len(shape)
    return pl.BlockSpec(shape, lambda i: (0,) * nd, pipeline_mode=pl.Buffered(1))


def _proj_call(h, lw, tables, lp, tl, seq_len):
    n_tiles = lp // tl
    small = [lw["aqf"], lw["akf"], lw["aqn"], lw["akn"], lw["bqn"], lw["bkn"], lw["cql"], lw["ckvl"],
             lw["cqnn"], lw["cqnr"], lw["cknn"], lw["cknr"]]
    first = isinstance(h, tuple)
    h_specs = _token_row_specs(tl, h[1].shape[0]) if first else [_row_spec(tl, D_MODEL)]
    h_args = (h[0], h[1], h[1], h[1], h[1]) if first else (h,)
    tile_fn = _with_token_rows(_proj_tile, seq_len, tl) if first else _proj_tile
    in_specs = (h_specs + [_const_spec((1, D_MODEL)),
                 pl.BlockSpec((None, D_MODEL, P_HEAD_END), lambda i: (lw["layer"], 0, 0),
                              pipeline_mode=pl.Buffered(1)),
                 _const_spec((D_MODEL, IN_COLS - P_TAIL)),
                 _const_spec(lw["wuq"].shape),
                 _const_spec(lw["wukv"].shape)]
                + [_const_spec(a.shape) for a in small]
                + [_row_spec(tl, LANES)] * 6)
    out_widths = [512, 2048, 512, 512, 256, 256, 1024, 1024, 512, 2048, 512]
    out_dtypes = [BF16, F8] + [BF16] * 8 + [F32]
    out_shape = [jax.ShapeDtypeStruct((lp, w), dt) for w, dt in zip(out_widths, out_dtypes)]
    out_specs = [_row_spec(tl, w) for w in out_widths]
    return pl.pallas_call(
        _row_tile_kernel(tile_fn, len(out_widths), 0, tl, seq_len),
        grid=(n_tiles,),
        in_specs=in_specs,
        out_specs=out_specs,
        out_shape=out_shape,
        compiler_params=pltpu.CompilerParams(
            dimension_semantics=("arbitrary",), vmem_limit_bytes=VMEM_LIMIT),
        name="proj",
    )(*h_args, lw["nw"], lw["win"], lw["wtail"], lw["wuq"], lw["wukv"], *small, *tables)


def _attn_call(q, k, v, mode, lp, n_tok, tq, ck):
    n_heads = {"A": 4, "B": 2, "C": 4}[mode]
    n_grp = {"A": 2, "B": 2, "C": 1}[mode]
    qw = {"A": LANES, "B": 2 * LANES, "C": 2 * LANES}[mode]
    kw = {"A": 4 * LANES, "B": LANES, "C": 2 * LANES}[mode]
    n_main = n_tok // ck
    tail_valid = N_META + n_tok - n_main * ck
    kern = functools.partial(_attn_kernel, mode=mode, tq=tq, ck=ck, n_main=n_main,
                             tail_valid=tail_valid, q_valid=N_META + n_tok)
    return pl.pallas_call(
        kern,
        grid=(n_heads, lp // tq),
        in_specs=[pl.BlockSpec((tq, qw), lambda hd, i: (i, hd)),
                  pl.BlockSpec((lp, kw), lambda hd, i: (0, hd), pipeline_mode=pl.Buffered(1)),
                  pl.BlockSpec((lp, LANES), lambda hd, i: (0, hd), pipeline_mode=pl.Buffered(1))],
        out_specs=pl.BlockSpec((None, n_grp, tq, LANES), lambda hd, i: (hd, 0, i, 0)),
        out_shape=jax.ShapeDtypeStruct((n_heads, n_grp, lp, LANES), F32),
        scratch_shapes=[pltpu.VMEM((lp, 2 * LANES), BF16),
                        pltpu.VMEM((6, n_grp * tq, ck), F32),
                        pltpu.VMEM((6, n_grp * tq, LANES), F32),
                        pltpu.VMEM((3, n_grp * tq, LANES), F32),
                        pltpu.VMEM((n_grp * tq, LANES), F32),
                        pltpu.VMEM((n_grp * tq, 2 * LANES), F32)],
        compiler_params=pltpu.CompilerParams(
            dimension_semantics=("arbitrary", "arbitrary"), vmem_limit_bytes=VMEM_LIMIT),
        name="attn_" + mode,
    )(q, k, v)


def _final_call(h, oa, ob, oc, du, sg, lw, lambda_init, lp, n_tok, tl):
    n_tiles = lp // tl
    hb = tl // POOL_HALO
    n_hblk = lp // POOL_HALO
    seq_len = N_META + n_tok
    tile_fn = functools.partial(_final_tile, lambda_init=lambda_init, seq_len=seq_len, tl=tl)
    first = isinstance(h, tuple)
    if first:
        tile_fn = _with_token_rows(tile_fn, seq_len, tl)
    h_specs = _token_row_specs(tl, h[1].shape[0]) if first else [_row_spec(tl, D_MODEL)]
    h_args = (h[0], h[1], h[1], h[1], h[1]) if first else (h,)
    kern = _row_tile_kernel(tile_fn, 1, 1, tl, seq_len)
    in_specs = h_specs + [
        pl.BlockSpec((4, 2, tl, LANES), lambda i: (0, 0, i, 0)),
        pl.BlockSpec((2, 2, tl, LANES), lambda i: (0, 0, i, 0)),
        pl.BlockSpec((4, 1, tl, LANES), lambda i: (0, 0, i, 0)),
        _row_spec(tl, BRANCH_W),
        pl.BlockSpec((POOL_HALO, BRANCH_W), lambda i: (jnp.maximum(i * hb - 1, 0), 0)),
        pl.BlockSpec((POOL_HALO, BRANCH_W), lambda i: (jnp.minimum((i + 1) * hb, n_hblk - 1), 0)),
        _row_spec(tl, D_MODEL),
        _const_spec((D_MODEL, D_MODEL)),
        _const_spec((4, A_QK)),
        _const_spec((1, LANES)),
        _const_spec((4, LANES, LANES)),
        _const_spec((1, BRANCH_W)),
    ]
    return pl.pallas_call(
        kern,
        grid=(n_tiles,),
        in_specs=in_specs,
        out_specs=_row_spec(tl, D_MODEL),
        out_shape=jax.ShapeDtypeStruct((lp, D_MODEL), F32),
        scratch_shapes=[pltpu.VMEM((tl + 2 * POOL_HALO, BRANCH_W), F32)],
        compiler_params=pltpu.CompilerParams(
            dimension_semantics=("arbitrary",), vmem_limit_bytes=VMEM_LIMIT),
        name="final",
    )(*h_args, oa, ob, oc, du, du, du, sg, lw["wout"], lw["lam"], lw["subln"], lw["wg"], lw["dsc"])


def _cast_kernel(x_ref, o_ref):
    o_ref[...] = x_ref[...].astype(o_ref.dtype)


def _cast_leading_columns(w, n_cols, block_cols=2 * LANES):
    depth, rows, _ = w.shape
    spec = pl.BlockSpec((None, rows, block_cols), lambda l, j: (l, 0, j))
    return pl.pallas_call(
        _cast_kernel,
        grid=(depth, n_cols // block_cols),
        in_specs=[spec],
        out_specs=spec,
        out_shape=jax.ShapeDtypeStruct((depth, rows, n_cols), BF16),
        compiler_params=pltpu.CompilerParams(dimension_semantics=("arbitrary", "arbitrary")),
        name="cast_w_in",
    )(w)


def _layer_weights(i, norm_w, w_in, w_in_bf16, w_out, a_q_norm, a_k_norm, a_lambda, a_subln, b_q_norm,
                   b_k_norm, c_q_lat_norm, c_kv_lat_norm, c_w_uq, c_w_ukv, c_q_norm, c_k_norm,
                   d_w_group, d_scale):
    wtail = w_in[i, :, P_TAIL:].astype(BF16)
    uq = c_w_uq[i].reshape(C_Q_LORA, 4, C_NOPE + C_ROPE)
    uq = jnp.pad(uq, ((0, 0), (0, 0), (0, 2 * LANES - C_NOPE - C_ROPE)))
    wuq = uq.reshape(C_Q_LORA, 4 * 2 * LANES).astype(BF16)
    row = lambda v: v.reshape(1, -1).astype(F32)
    a_scale = (A_QK ** -0.5) * LOG2E
    tiny = jnp.finfo(F32).tiny
    q_mag = jnp.maximum(jnp.max(jnp.abs(a_q_norm[i])).astype(F32) * a_scale, tiny)
    k_mag = jnp.maximum(jnp.max(jnp.abs(a_k_norm[i])).astype(F32), tiny)
    shift = jnp.clip(jnp.round(0.5 * (jnp.log2(k_mag) - jnp.log2(q_mag))), -60.0, 60.0)
    aqf = jnp.full((1, LANES), a_scale, F32) * jnp.exp2(shift)
    akf = jnp.full((1, LANES), 1.0, F32) * jnp.exp2(-shift)
    pad_row = lambda v: jnp.pad(v.astype(F32), (0, LANES - v.shape[0])).reshape(1, LANES)
    return dict(
        layer=i, nw=row(norm_w[i]), win=w_in_bf16, wtail=wtail, wuq=wuq, aqf=aqf, akf=akf, wukv=c_w_ukv[i].astype(BF16),
        aqn=row(jnp.tile(a_q_norm[i], 2)), akn=row(jnp.tile(a_k_norm[i], 2)),
        bqn=row(b_q_norm[i]), bkn=row(b_k_norm[i]),
        cql=row(c_q_lat_norm[i]), ckvl=row(c_kv_lat_norm[i]),
        cqnn=row(c_q_norm[i][:C_NOPE]), cqnr=pad_row(c_q_norm[i][C_NOPE:]),
        cknn=row(c_k_norm[i][:C_NOPE]), cknr=pad_row(c_k_norm[i][C_NOPE:]),
        wout=w_out[i].astype(BF16), lam=a_lambda[i].astype(F32), subln=row(a_subln[i]),
        wg=d_w_group[i].astype(BF16), dsc=row(d_scale[i]),
    )


def kernel(x, meta_tokens, norm_w, w_in, w_out, a_q_norm, a_k_norm, a_lambda, a_subln, b_q_norm,
           b_k_norm, c_q_lat_norm, c_kv_lat_norm, c_w_uq, c_w_ukv, c_q_norm, c_k_norm,
           d_w_group, d_scale):
    b, n_tok, d = x.shape
    assert b == 1 and d == D_MODEL
    lp = n_tok + SEQ_PAD
    tl = ROW_TILE
    ck = KV_CHUNK
    assert lp % tl == 0 and lp % ATTN_ROWS == 0 and n_tok % GRID_W == 0
    assert n_tok % (2 * ck) == 0 and (n_tok // (2 * ck) - 2) % 3 == 0
    depth = norm_w.shape[0]

    assert tl % 3 == 0 and n_tok % (tl // 3) == 0 and tl % N_META == 0
    h = (meta_tokens.astype(x.dtype), x[0])
    tables = _rope_lane_tables(n_tok, lp)

    w_in_bf16 = _cast_leading_columns(w_in, P_HEAD_END)
    for i in range(depth):
        lambda_init = 0.8 - 0.6 * math.exp(-0.3 * i)
        lw = _layer_weights(i, norm_w, w_in, w_in_bf16, w_out, a_q_norm, a_k_norm, a_lambda, a_subln,
                            b_q_norm, b_k_norm, c_q_lat_norm, c_kv_lat_norm, c_w_uq, c_w_ukv,
                            c_q_norm, c_k_norm, d_w_group, d_scale)
        qa, ka, va, qb, kb, vb, qc, kc, vc, sg, du = _proj_call(h, lw, tables, lp, tl, N_META + n_tok)
        oa = _attn_call(qa, ka, va, "A", lp, n_tok, ATTN_ROWS // 2, ck)
        ob = _attn_call(qb, kb, vb, "B", lp, n_tok, ATTN_ROWS // 2, ck)
        oc = _attn_call(qc, kc, vc, "C", lp, n_tok, ATTN_ROWS, ck)
        h = _final_call(h, oa, ob, oc, du, sg, lw, lambda_init, lp, n_tok, tl)
    return h[N_META:N_META + n_tok][None]
```

```python
import functools
import math

import jax
import jax.numpy as jnp
from jax import lax
from jax.experimental import pallas as pl
from jax.experimental.pallas import tpu as pltpu

F32 = jnp.float32
BF16 = jnp.bfloat16
F8 = jnp.float8_e4m3fn

D_MODEL = 2048
N_META = 16
GRID_W = 64
EPS = 1e-6
ROPE_THETA = 500000.0
AXIAL_THETA = 10000.0
BRANCH_W = 512
LANES = 128
SEQ_PAD = 512
ROW_TILE = 384
ATTN_ROWS = 768
KV_CHUNK = 1024
PV_ROW_BLOCKS = 2
A_QK = 64
A_ROT = 16
B_HD = 128
C_NOPE = 128
C_ROPE = 64
C_Q_LORA = 384
C_KV_LORA = 256
POOL_WINDOWS = (2, 4, 8, 16)
POOL_HALO = 8
LOG2E = 1.4426950408889634
NEG_BIG = -1e30
VMEM_LIMIT = 56 * 1024 * 1024

P_AQ, P_AV, P_AG = 0, 1024, 1536
P_BQ, P_BV, P_BG = 2048, 2816, 3072
P_CQ = 3584
P_TAIL = 4288
IN_COLS = 5824
P_HEAD_END = P_CQ + C_Q_LORA + C_KV_LORA + LANES


def _mean_sq(x, n):
    return jnp.sum(x * x, axis=-1, keepdims=True) * (1.0 / n)


def _half_mean_sq(x):
    xx = x * x
    lower = lax.broadcasted_iota(jnp.int32, x.shape, 1) < (LANES // 2)
    lo = jnp.sum(jnp.where(lower, xx, 0.0), axis=-1, keepdims=True)
    hi = jnp.sum(jnp.where(lower, 0.0, xx), axis=-1, keepdims=True)
    return jnp.where(lower, lo, hi) * (2.0 / LANES)


def _rope(x, c, s, half):
    lane = lax.broadcasted_iota(jnp.int32, x.shape, 1)
    fwd = pltpu.roll(x, LANES - half, axis=1)
    bwd = pltpu.roll(x, half, axis=1)
    partner = jnp.where((lane % (2 * half)) < half, fwd, bwd)
    return x * c + partner * s


def _proj_tile(h_ref, nw_ref, win_ref, wtail_ref, wuq_ref, wukv_ref,
               aqf_ref, akf_ref, aqn_ref, akn_ref, bqn_ref, bkn_ref, cql_ref, ckvl_ref,
               cqnn_ref, cqnr_ref, cknn_ref, cknr_ref,
               ca_ref, sa_ref, cb_ref, sb_ref, cc_ref, sc_ref,
               qa_ref, ka_ref, va_ref, qb_ref, kb_ref, vb_ref,
               qc_ref, kc_ref, vc_ref, sg_ref, du_ref):
    h = h_ref[...]
    ms = jnp.mean(h * h, axis=-1, keepdims=True)
    hb = (h * lax.rsqrt(ms + EPS) * nw_ref[...]).astype(BF16)

    def mm(lo, hi):
        return jnp.dot(hb, win_ref[:, lo:hi], preferred_element_type=F32)

    def mm_tail(blk):
        return jnp.dot(hb, wtail_ref[:, blk * BRANCH_W:(blk + 1) * BRANCH_W],
                       preferred_element_type=F32)

    ca, sa = ca_ref[...], sa_ref[...]
    cb, sb = cb_ref[...], sb_ref[...]
    cc, sc = cc_ref[...], sc_ref[...]

    pa = mm(P_AQ, P_AV)
    for blk in range(8):
        x = pa[:, blk * LANES:(blk + 1) * LANES]
        w = aqn_ref[...] if blk < 4 else akn_ref[...]
        x = x * lax.rsqrt(_half_mean_sq(x) + EPS) * w
        x = _rope(x, ca, sa, A_ROT // 2)
        if blk < 4:
            qa_ref[:, blk * LANES:(blk + 1) * LANES] = (x * aqf_ref[...]).astype(BF16)
        else:
            kh, kl = _f8_split((x * akf_ref[...]).astype(BF16).astype(F32))
            kl = kl * (1.0 / 16.0)
            khr = pltpu.roll(kh, A_QK, axis=1)
            klr = pltpu.roll(kl, A_QK, axis=1)
            lower = lax.broadcasted_iota(jnp.int32, kh.shape, 1) < A_QK
            parts = [jnp.where(lower, kh, khr * 0.0625), jnp.where(lower, kl, klr * 0.0625),
                     jnp.where(lower, khr, kh * 0.0625), jnp.where(lower, klr, kl * 0.0625)]
            for b, part in enumerate(parts):
                c0 = ((blk - 4) * 4 + b) * LANES
                ka_ref[:, c0:c0 + LANES] = part.astype(F8)
    va_ref[...] = mm(P_AV, P_AG).astype(BF16)

    pb = mm(P_BQ, P_BV)
    b_scale = (B_HD ** -0.5) * LOG2E
    for blk in range(6):
        x = pb[:, blk * LANES:(blk + 1) * LANES]
        w = bqn_ref[...] if blk < 4 else bkn_ref[...]
        x = x * lax.rsqrt(_mean_sq(x, B_HD) + EPS) * w
        x = _rope(x, cb, sb, B_HD // 4)
        if blk < 4:
            qb_ref[:, blk * LANES:(blk + 1) * LANES] = (x * b_scale).astype(BF16)
        else:
            kb_ref[:, (blk - 4) * LANES:(blk - 3) * LANES] = x.astype(BF16)
    vb_ref[...] = mm(P_BV, P_BG).astype(BF16)

    pc = mm(P_CQ, P_HEAD_END)
    cq = pc[:, 0:C_Q_LORA]
    cq = cq * lax.rsqrt(jnp.mean(cq * cq, axis=-1, keepdims=True) + EPS) * cql_ref[...]
    qfull = jnp.dot(cq.astype(BF16), wuq_ref[...], preferred_element_type=F32)
    ckv = pc[:, C_Q_LORA:C_Q_LORA + C_KV_LORA]
    ckv = ckv * lax.rsqrt(jnp.mean(ckv * ckv, axis=-1, keepdims=True) + EPS) * ckvl_ref[...]
    kvfull = jnp.dot(ckv.astype(BF16), wukv_ref[...], preferred_element_type=F32)
    kr = pc[:, C_Q_LORA + C_KV_LORA:]
    kr = jnp.where(lax.broadcasted_iota(jnp.int32, kr.shape, 1) < C_ROPE, kr, 0.0)
    kr = kr * lax.rsqrt(_mean_sq(kr, C_ROPE) + EPS) * cknr_ref[...]
    kr = _rope(kr, cc, sc, C_ROPE // 2).astype(BF16)
    c_scale = ((C_NOPE + C_ROPE) ** -0.5) * LOG2E
    for hd in range(4):
        base = hd * 2 * LANES
        qn = qfull[:, base:base + LANES]
        qn = qn * lax.rsqrt(_mean_sq(qn, C_NOPE) + EPS) * cqnn_ref[...]
        qr = qfull[:, base + LANES:base + 2 * LANES]
        qr = qr * lax.rsqrt(_mean_sq(qr, C_ROPE) + EPS) * cqnr_ref[...]
        qr = _rope(qr, cc, sc, C_ROPE // 2)
        qc_ref[:, base:base + LANES] = (qn * c_scale).astype(BF16)
        qc_ref[:, base + LANES:base + 2 * LANES] = (qr * c_scale).astype(BF16)
        kn = kvfull[:, base:base + LANES]
        kn = kn * lax.rsqrt(_mean_sq(kn, C_NOPE) + EPS) * cknn_ref[...]
        kc_ref[:, base:base + LANES] = kn.astype(BF16)
        kc_ref[:, base + LANES:base + 2 * LANES] = kr
        vc_ref[:, hd * LANES:(hd + 1) * LANES] = kvfull[:, base + LANES:base + 2 * LANES].astype(BF16)

    gates = [mm(P_AG, P_BQ), mm(P_BG, P_CQ), mm_tail(0), mm_tail(2)]
    for blk, g in enumerate(gates):
        sg_ref[:, blk * BRANCH_W:(blk + 1) * BRANCH_W] = jax.nn.silu(g).astype(BF16)
    du_ref[...] = mm_tail(1)


def _f8_split(x):
    hi = x.astype(F8).astype(F32)
    return hi, (x - hi) * 16.0


def _lane_repeat(x, n):
    return jnp.concatenate([x] * n, axis=1)


def _attn_tile(q_ref, k_ref, o_ref, vx_sc, s_sc, mc_sc, mr_sc, m_sc, acc_sc, *,
               mode, tq, ck, n_main, tail_valid):
    if mode == "A":
        qh, ql = _f8_split(q_ref[...].astype(F32))
        lower = lax.broadcasted_iota(jnp.int32, qh.shape, 1) < A_QK
        b0 = jnp.where(lower, qh, pltpu.roll(ql, A_QK, axis=1)).astype(F8)
        b1 = jnp.where(lower, pltpu.roll(qh, A_QK, axis=1), ql).astype(F8)
        q8 = [jnp.concatenate([b, b], axis=1) for b in (b0, b1)]
        rows = 2 * tq
    elif mode == "B":
        qs = jnp.concatenate([q_ref[:, 0:LANES], q_ref[:, LANES:2 * LANES]], axis=0)
        rows = qs.shape[0]
    else:
        qs = q_ref[...]
        rows = qs.shape[0]
    n_grp = rows // tq

    m_sc[...] = jnp.full(m_sc.shape, NEG_BIG, F32)
    acc_sc[...] = jnp.zeros(acc_sc.shape, F32)

    contract_last = (((1,), (1,)), ((), ()))

    def scores(start, width):
        if mode == "A":
            parts = [lax.dot_general(q8[m], k_ref[pl.ds(start, width), m * 2 * LANES:(m + 1) * 2 * LANES],
                                     contract_last, preferred_element_type=F32) for m in range(2)]
            return jnp.concatenate(parts, axis=0)
        return lax.dot_general(qs, k_ref[pl.ds(start, width), :], contract_last,
                               preferred_element_type=F32)

    def row_max(s):
        return jnp.broadcast_to(jnp.max(s, axis=1, keepdims=True), (rows, LANES))

    def lane_max(s):
        parts = [s[:, t * LANES:(t + 1) * LANES] for t in range(s.shape[1] // LANES)]
        return functools.reduce(jnp.maximum, parts)

    def qk_stage(j, slot):
        s = scores(pl.multiple_of(j * ck, ck), ck)
        s_sc[slot] = s
        mc_sc[slot] = lane_max(s)

    def softmax_pv(s_parts, m_cur, vx, after_block=()):
        m_prev = m_sc[...]
        m_new = jnp.maximum(m_prev, m_cur)
        alpha = jnp.exp2(m_prev - m_new)
        m_sc[...] = m_new
        assert len(after_block) <= PV_ROW_BLOCKS
        blk = rows // PV_ROW_BLOCKS
        for r0 in range(0, rows, blk):
            mr = m_new[r0:r0 + blk]
            p = jnp.concatenate(
                [jnp.exp2(s[r0:r0 + blk] - _lane_repeat(mr, s.shape[1] // LANES)).astype(BF16)
                 for s in s_parts], axis=1)
            pv = jnp.dot(p, vx, preferred_element_type=F32)
            acc_sc[r0:r0 + blk, :] = _lane_repeat(alpha[r0:r0 + blk], 2) * acc_sc[r0:r0 + blk, :] + pv
            if r0 // blk < len(after_block):
                after_block[r0 // blk]()

    def qk_pair(p, r):
        qk_stage(2 * p, 2 * r)
        qk_stage(2 * p + 1, 2 * r + 1)

    def reduce_pair(r):
        mr_sc[r] = row_max(jnp.maximum(mc_sc[2 * r], mc_sc[2 * r + 1]))

    def pv_pair(p, r, after_block=()):
        vx = vx_sc[pl.ds(pl.multiple_of(p * 2 * ck, 2 * ck), 2 * ck), :]
        softmax_pv([s_sc[2 * r], s_sc[2 * r + 1]], mr_sc[r], vx, after_block)

    n_pairs = n_main // 2
    qk_pair(0, 0)
    qk_pair(1, 1)
    reduce_pair(0)

    def body(i, carry):
        for r in range(3):
            p = 3 * i + r
            reduce_pair((r + 1) % 3)
            nxt = (r + 2) % 3
            if mode == "A":
                pv_pair(p, r, [functools.partial(qk_stage, 2 * (p + 2), 2 * nxt),
                               functools.partial(qk_stage, 2 * (p + 2) + 1, 2 * nxt + 1)])
            else:
                pv_pair(p, r)
                qk_pair(p + 2, nxt)
        return carry

    lax.fori_loop(0, (n_pairs - 2) // 3, body, 0)
    r = (n_pairs - 2) % 3
    pv_pair(n_pairs - 2, r)
    reduce_pair((r + 1) % 3)
    t0 = n_main * ck
    s_tail = scores(t0, LANES)
    col = lax.broadcasted_iota(jnp.int32, s_tail.shape, 1)
    s_tail = jnp.where(col < tail_valid, s_tail, NEG_BIG)
    pv_pair(n_pairs - 1, (r + 1) % 3)
    softmax_pv([s_tail], row_max(s_tail), vx_sc[t0:t0 + LANES, :])

    o = acc_sc[:, 0:LANES] / acc_sc[:, LANES:2 * LANES]
    for g in range(n_grp):
        o_ref[g] = o[g * tq:(g + 1) * tq]


def _attn_kernel(q_ref, k_ref, v_ref, o_ref, vx_sc, *scratch, tq, q_valid, **tile_args):
    lp = v_ref.shape[0]

    @pl.when(pl.program_id(1) == 0)
    def _():
        def fill(c, carry):
            r = pl.multiple_of(c * tq, tq)
            vx_sc[pl.ds(r, tq), 0:LANES] = v_ref[pl.ds(r, tq), :]
            vx_sc[pl.ds(r, tq), LANES:2 * LANES] = jnp.ones((tq, LANES), BF16)
            return carry
        lax.fori_loop(0, lp // tq, fill, 0)

    has_rows = pl.program_id(1) * tq < q_valid

    @pl.when(has_rows)
    def _():
        _attn_tile(q_ref, k_ref, o_ref, vx_sc, *scratch, tq=tq, **tile_args)

    @pl.when(jnp.logical_not(has_rows))
    def _():
        o_ref[...] = jnp.zeros(o_ref.shape, F32)


def _final_tile(h_ref, oa_ref, ob_ref, oc_ref, du_ref, dup_ref, dun_ref, sg_ref,
                wout_ref, lam_ref, subln_ref, wg_ref, dsc_ref, out_ref, ext_sc, *,
                lambda_init, seq_len, tl):
    i = pl.program_id(0)

    lp = lam_ref[...]
    lam = (jnp.exp(jnp.sum(lp[0:1, :] * lp[1:2, :], axis=-1, keepdims=True))
           - jnp.exp(jnp.sum(lp[2:3, :] * lp[3:4, :], axis=-1, keepdims=True)) + lambda_init)

    ys = []
    for hd in range(4):
        d = oa_ref[hd, 0] - lam * oa_ref[hd, 1]
        d = d * lax.rsqrt(_mean_sq(d, LANES) + EPS) * subln_ref[...]
        d = d * (1.0 - lambda_init)
        ys.append(d * sg_ref[:, hd * LANES:(hd + 1) * LANES].astype(F32))
    for g in range(2):
        for r in range(2):
            c = 4 + 2 * g + r
            ys.append(ob_ref[g, r] * sg_ref[:, c * LANES:(c + 1) * LANES].astype(F32))
    for hd in range(4):
        c = 8 + hd
        ys.append(oc_ref[hd, 0] * sg_ref[:, c * LANES:(c + 1) * LANES].astype(F32))

    row0 = i * tl
    def masked(x, first_row):
        r = first_row + lax.broadcasted_iota(jnp.int32, x.shape, 0)
        return jnp.where((r >= 0) & (r < seq_len), x, 0.0)
    ext_sc[0:POOL_HALO, :] = masked(dup_ref[...], row0 - POOL_HALO)
    ext_sc[POOL_HALO:POOL_HALO + tl, :] = masked(du_ref[...], row0)
    ext_sc[POOL_HALO + tl:2 * POOL_HALO + tl, :] = masked(dun_ref[...], row0 + tl)
    t = row0 + lax.broadcasted_iota(jnp.int32, (tl, LANES), 0)
    for g, w in enumerate(POOL_WINDOWS):
        cols = slice(g * LANES, (g + 1) * LANES)
        acc = ext_sc[POOL_HALO - w // 2:POOL_HALO - w // 2 + tl, cols]
        for j in range(-w // 2 + 1, w // 2):
            acc = acc + ext_sc[POOL_HALO + j:POOL_HALO + j + tl, cols]
        cnt = jnp.minimum(t + w // 2, seq_len) - jnp.maximum(t - w // 2, 0)
        cnt = jnp.maximum(cnt, 1).astype(F32)
        pooled = acc / cnt - ext_sc[POOL_HALO:POOL_HALO + tl, cols]
        mixed = jnp.dot(pooled.astype(BF16), wg_ref[g], preferred_element_type=F32)
        c = 12 + g
        ys.append(mixed * dsc_ref[:, cols] * sg_ref[:, c * LANES:(c + 1) * LANES].astype(F32))

    y = jnp.concatenate([v.astype(BF16) for v in ys], axis=1)
    out_ref[...] = h_ref[...] + jnp.dot(y, wout_ref[...], preferred_element_type=F32)


def _rope_lane_tables(n_tok, lp):
    l = N_META + n_tok
    rows = n_tok // GRID_W
    lane = jnp.arange(LANES, dtype=jnp.int32)

    def inv_freq(idx, dim, theta):
        return theta ** (-(2 * idx).astype(F32) / dim)

    def padded(v):
        return jnp.pad(v.astype(F32), (0, lp - l))[:, None]

    pos = padded(jnp.arange(l, dtype=jnp.int32))
    row_pos = padded(jnp.concatenate([jnp.full((N_META,), -1, jnp.int32),
                                      jnp.repeat(jnp.arange(rows, dtype=jnp.int32), GRID_W)]))
    col_pos = padded(jnp.concatenate([jnp.arange(N_META, dtype=jnp.int32),
                                      jnp.tile(jnp.arange(GRID_W, dtype=jnp.int32), rows)]))

    ja = lane % A_QK
    inv_a = jnp.where(ja < A_ROT, inv_freq(ja % (A_ROT // 2), A_ROT, ROPE_THETA), 0.0)
    sign_a = jnp.where(ja < A_ROT // 2, -1.0, 1.0).astype(F32)
    ang_a = pos * inv_a[None, :]
    jb = lane % (B_HD // 2)
    inv_b = inv_freq(jb % (B_HD // 4), B_HD // 2, AXIAL_THETA)
    inv_r = jnp.where(lane < B_HD // 2, inv_b, 0.0)
    inv_c = jnp.where(lane < B_HD // 2, 0.0, inv_b)
    sign_b = jnp.where(jb < B_HD // 4, -1.0, 1.0).astype(F32)
    ang_b = row_pos * inv_r[None, :] + col_pos * inv_c[None, :]
    inv_m = jnp.where(lane < C_ROPE, inv_freq(lane % (C_ROPE // 2), C_ROPE, ROPE_THETA), 0.0)
    sign_m = jnp.where(lane < C_ROPE // 2, -1.0, 1.0).astype(F32)
    ang_m = pos * inv_m[None, :]
    return (jnp.cos(ang_a), jnp.sin(ang_a) * sign_a[None, :],
            jnp.cos(ang_b), jnp.sin(ang_b) * sign_b[None, :],
            jnp.cos(ang_m), jnp.sin(ang_m) * sign_m[None, :])


def _row_tile_kernel(tile_fn, n_out, n_scratch, tl, seq_len):
    def kern(*refs):
        outs = refs[len(refs) - n_scratch - n_out:len(refs) - n_scratch]
        has_rows = pl.program_id(0) * tl < seq_len

        @pl.when(has_rows)
        def _():
            tile_fn(*refs)

        @pl.when(jnp.logical_not(has_rows))
        def _():
            for o in outs:
                o[...] = jnp.zeros(o.shape, o.dtype)
    return kern


class _TokenRows:
    def __init__(self, meta_ref, prev_ref, body_refs, seq_len, tl):
        self._parts = (meta_ref, prev_ref, body_refs)
        self._seq_len, self._tl = seq_len, tl

    def __getitem__(self, idx):
        meta_ref, prev_ref, body_refs = self._parts
        i = pl.program_id(0)
        head = jnp.where(i == 0, meta_ref[...], prev_ref[...])
        body = [r[...] for r in body_refs]
        body[-1] = body[-1][:body[-1].shape[0] - N_META]
        rows = jnp.concatenate([head] + body, axis=0)
        r_nat = i * self._tl + lax.broadcasted_iota(jnp.int32, rows.shape, 0)
        return jnp.where(r_nat < self._seq_len, rows, 0.0)


def _with_token_rows(tile_fn, seq_len, tl):
    def fn(meta_ref, prev_ref, b0_ref, b1_ref, b2_ref, *rest):
        tile_fn(_TokenRows(meta_ref, prev_ref, (b0_ref, b1_ref, b2_ref), seq_len, tl), *rest)
    return fn


def _token_row_specs(tl, n_tok):
    sub = tl // 3
    n_sub = n_tok // sub
    return [_const_spec((N_META, D_MODEL)),
            pl.BlockSpec((N_META, D_MODEL),
                         lambda i: (jnp.clip(i * (tl // N_META) - 1, 0, n_tok // N_META - 1), 0)),
            pl.BlockSpec((sub, D_MODEL), lambda i: (jnp.minimum(3 * i, n_sub - 1), 0)),
            pl.BlockSpec((sub, D_MODEL), lambda i: (jnp.minimum(3 * i + 1, n_sub - 1), 0)),
            pl.BlockSpec((sub, D_MODEL), lambda i: (jnp.minimum(3 * i + 2, n_sub - 1), 0))]


def _row_spec(tl, width):
    return pl.BlockSpec((tl, width), lambda i: (i, 0))


def _const_spec(shape):
    nd = len(shape)
    return pl.BlockSpec(shape, lambda i: (0,) * nd, pipeline_mode=pl.Buffered(1))


def _proj_call(h, lw, tables, lp, tl, seq_len):
    n_tiles = lp // tl
    small = [lw["aqf"], lw["akf"], lw["aqn"], lw["akn"], lw["bqn"], lw["bkn"], lw["cql"], lw["ckvl"],
             lw["cqnn"], lw["cqnr"], lw["cknn"], lw["cknr"]]
    first = isinstance(h, tuple)
    h_specs = _token_row_specs(tl, h[1].shape[0]) if first else [_row_spec(tl, D_MODEL)]
    h_args = (h[0], h[1], h[1], h[1], h[1]) if first else (h,)
    tile_fn = _with_token_rows(_proj_tile, seq_len, tl) if first else _proj_tile
    in_specs = (h_specs + [_const_spec((1, D_MODEL)),
                 pl.BlockSpec((None, D_MODEL, P_HEAD_END), lambda i: (lw["layer"], 0, 0),
                              pipeline_mode=pl.Buffered(1)),
                 _const_spec((D_MODEL, IN_COLS - P_TAIL)),
                 _const_spec(lw["wuq"].shape),
                 _const_spec(lw["wukv"].shape)]
                + [_const_spec(a.shape) for a in small]
                + [_row_spec(tl, LANES)] * 6)
    out_widths = [512, 2048, 512, 512, 256, 256, 1024, 1024, 512, 2048, 512]
    out_dtypes = [BF16, F8] + [BF16] * 8 + [F32]
    out_shape = [jax.ShapeDtypeStruct((lp, w), dt) for w, dt in zip(out_widths, out_dtypes)]
    out_specs = [_row_spec(tl, w) for w in out_widths]
    return pl.pallas_call(
        _row_tile_kernel(tile_fn, len(out_widths), 0, tl, seq_len),
        grid=(n_tiles,),
        in_specs=in_specs,
        out_specs=out_specs,
        out_shape=out_shape,
        compiler_params=pltpu.CompilerParams(
            dimension_semantics=("arbitrary",), vmem_limit_bytes=VMEM_LIMIT),
        name="proj",
    )(*h_args, lw["nw"], lw["win"], lw["wtail"], lw["wuq"], lw["wukv"], *small, *tables)


def _attn_call(q, k, v, mode, lp, n_tok, tq, ck):
    n_heads = {"A": 4, "B": 2, "C": 4}[mode]
    n_grp = {"A": 2, "B": 2, "C": 1}[mode]
    qw = {"A": LANES, "B": 2 * LANES, "C": 2 * LANES}[mode]
    kw = {"A": 4 * LANES, "B": LANES, "C": 2 * LANES}[mode]
    n_main = n_tok // ck
    tail_valid = N_META + n_tok - n_main * ck
    kern = functools.partial(_attn_kernel, mode=mode, tq=tq, ck=ck, n_main=n_main,
                             tail_valid=tail_valid, q_valid=N_META + n_tok)
    return pl.pallas_call(
        kern,
        grid=(n_heads, lp // tq),
        in_specs=[pl.BlockSpec((tq, qw), lambda hd, i: (i, hd)),
                  pl.BlockSpec((lp, kw), lambda hd, i: (0, hd), pipeline_mode=pl.Buffered(1)),
                  pl.BlockSpec((lp, LANES), lambda hd, i: (0, hd), pipeline_mode=pl.Buffered(1))],
        out_specs=pl.BlockSpec((None, n_grp, tq, LANES), lambda hd, i: (hd, 0, i, 0)),
        out_shape=jax.ShapeDtypeStruct((n_heads, n_grp, lp, LANES), F32),
        scratch_shapes=[pltpu.VMEM((lp, 2 * LANES), BF16),
                        pltpu.VMEM((6, n_grp * tq, ck), F32),
                        pltpu.VMEM((6, n_grp * tq, LANES), F32),
                        pltpu.VMEM((3, n_grp * tq, LANES), F32),
                        pltpu.VMEM((n_grp * tq, LANES), F32),
                        pltpu.VMEM((n_grp * tq, 2 * LANES), F32)],
        compiler_params=pltpu.CompilerParams(
            dimension_semantics=("arbitrary", "arbitrary"), vmem_limit_bytes=VMEM_LIMIT),
        name="attn_" + mode,
    )(q, k, v)


def _final_call(h, oa, ob, oc, du, sg, lw, lambda_init, lp, n_tok, tl):
    n_tiles = lp // tl
    hb = tl // POOL_HALO
    n_hblk = lp // POOL_HALO
    seq_len = N_META + n_tok
    tile_fn = functools.partial(_final_tile, lambda_init=lambda_init, seq_len=seq_len, tl=tl)
    first = isinstance(h, tuple)
    if first:
        tile_fn = _with_token_rows(tile_fn, seq_len, tl)
    h_specs = _token_row_specs(tl, h[1].shape[0]) if first else [_row_spec(tl, D_MODEL)]
    h_args = (h[0], h[1], h[1], h[1], h[1]) if first else (h,)
    kern = _row_tile_kernel(tile_fn, 1, 1, tl, seq_len)
    in_specs = h_specs + [
        pl.BlockSpec((4, 2, tl, LANES), lambda i: (0, 0, i, 0)),
        pl.BlockSpec((2, 2, tl, LANES), lambda i: (0, 0, i, 0)),
        pl.BlockSpec((4, 1, tl, LANES), lambda i: (0, 0, i, 0)),
        _row_spec(tl, BRANCH_W),
        pl.BlockSpec((POOL_HALO, BRANCH_W), lambda i: (jnp.maximum(i * hb - 1, 0), 0)),
        pl.BlockSpec((POOL_HALO, BRANCH_W), lambda i: (jnp.minimum((i + 1) * hb, n_hblk - 1), 0)),
        _row_spec(tl, D_MODEL),
        _const_spec((D_MODEL, D_MODEL)),
        _const_spec((4, A_QK)),
        _const_spec((1, LANES)),
        _const_spec((4, LANES, LANES)),
        _const_spec((1, BRANCH_W)),
    ]
    return pl.pallas_call(
        kern,
        grid=(n_tiles,),
        in_specs=in_specs,
        out_specs=_row_spec(tl, D_MODEL),
        out_shape=jax.ShapeDtypeStruct((lp, D_MODEL), F32),
        scratch_shapes=[pltpu.VMEM((tl + 2 * POOL_HALO, BRANCH_W), F32)],
        compiler_params=pltpu.CompilerParams(
            dimension_semantics=("arbitrary",), vmem_limit_bytes=VMEM_LIMIT),
        name="final",
    )(*h_args, oa, ob, oc, du, du, du, sg, lw["wout"], lw["lam"], lw["subln"], lw["wg"], lw["dsc"])


def _layer_weights(i, norm_w, w_in, w_in_bf16, w_out, a_q_norm, a_k_norm, a_lambda, a_subln, b_q_norm,
                   b_k_norm, c_q_lat_norm, c_kv_lat_norm, c_w_uq, c_w_ukv, c_q_norm, c_k_norm,
                   d_w_group, d_scale):
    wtail = w_in[i, :, P_TAIL:].astype(BF16)
    uq = c_w_uq[i].reshape(C_Q_LORA, 4, C_NOPE + C_ROPE)
    uq = jnp.pad(uq, ((0, 0), (0, 0), (0, 2 * LANES - C_NOPE - C_ROPE)))
    wuq = uq.reshape(C_Q_LORA, 4 * 2 * LANES).astype(BF16)
    row = lambda v: v.reshape(1, -1).astype(F32)
    a_scale = (A_QK ** -0.5) * LOG2E
    tiny = jnp.finfo(F32).tiny
    q_mag = jnp.maximum(jnp.max(jnp.abs(a_q_norm[i])).astype(F32) * a_scale, tiny)
    k_mag = jnp.maximum(jnp.max(jnp.abs(a_k_norm[i])).astype(F32), tiny)
    shift = jnp.clip(jnp.round(0.5 * (jnp.log2(k_mag) - jnp.log2(q_mag))), -60.0, 60.0)
    aqf = jnp.full((1, LANES), a_scale, F32) * jnp.exp2(shift)
    akf = jnp.full((1, LANES), 1.0, F32) * jnp.exp2(-shift)
    pad_row = lambda v: jnp.pad(v.astype(F32), (0, LANES - v.shape[0])).reshape(1, LANES)
    return dict(
        layer=i, nw=row(norm_w[i]), win=w_in_bf16, wtail=wtail, wuq=wuq, aqf=aqf, akf=akf, wukv=c_w_ukv[i].astype(BF16),
        aqn=row(jnp.tile(a_q_norm[i], 2)), akn=row(jnp.tile(a_k_norm[i], 2)),
        bqn=row(b_q_norm[i]), bkn=row(b_k_norm[i]),
        cql=row(c_q_lat_norm[i]), ckvl=row(c_kv_lat_norm[i]),
        cqnn=row(c_q_norm[i][:C_NOPE]), cqnr=pad_row(c_q_norm[i][C_NOPE:]),
        cknn=row(c_k_norm[i][:C_NOPE]), cknr=pad_row(c_k_norm[i][C_NOPE:]),
        wout=w_out[i].astype(BF16), lam=a_lambda[i].astype(F32), subln=row(a_subln[i]),
        wg=d_w_group[i].astype(BF16), dsc=row(d_scale[i]),
    )


def kernel(x, meta_tokens, norm_w, w_in, w_out, a_q_norm, a_k_norm, a_lambda, a_subln, b_q_norm,
           b_k_norm, c_q_lat_norm, c_kv_lat_norm, c_w_uq, c_w_ukv, c_q_norm, c_k_norm,
           d_w_group, d_scale):
    b, n_tok, d = x.shape
    assert b == 1 and d == D_MODEL
    lp = n_tok + SEQ_PAD
    tl = ROW_TILE
    ck = KV_CHUNK
    assert lp % tl == 0 and lp % ATTN_ROWS == 0 and n_tok % GRID_W == 0
    assert n_tok % (2 * ck) == 0 and (n_tok // (2 * ck) - 2) % 3 == 0
    depth = norm_w.shape[0]

    assert tl % 3 == 0 and n_tok % (tl // 3) == 0 and tl % N_META == 0
    h = (meta_tokens.astype(x.dtype), x[0])
    tables = _rope_lane_tables(n_tok, lp)

    w_in_bf16 = w_in.astype(BF16)
    for i in range(depth):
        lambda_init = 0.8 - 0.6 * math.exp(-0.3 * i)
        lw = _layer_weights(i, norm_w, w_in, w_in_bf16, w_out, a_q_norm, a_k_norm, a_lambda, a_subln,
                            b_q_norm, b_k_norm, c_q_lat_norm, c_kv_lat_norm, c_w_uq, c_w_ukv,
                            c_q_norm, c_k_norm, d_w_group, d_scale)
        qa, ka, va, qb, kb, vb, qc, kc, vc, sg, du = _proj_call(h, lw, tables, lp, tl, N_META + n_tok)
        oa = _attn_call(qa, ka, va, "A", lp, n_tok, ATTN_ROWS // 2, ck)
        ob = _attn_call(qb, kb, vb, "B", lp, n_tok, ATTN_ROWS // 2, ck)
        oc = _attn_call(qc, kc, vc, "C", lp, n_tok, ATTN_ROWS, ck)
        h = _final_call(h, oa, ob, oc, du, sg, lw, lambda_init, lp, n_tok, tl)
    return h[N_META:N_META + n_tok][None]
```

```python
import functools
import math

import jax
import jax.numpy as jnp
from jax import lax
from jax.experimental import pallas as pl
from jax.experimental.pallas import tpu as pltpu

F32 = jnp.float32
BF16 = jnp.bfloat16
F8 = jnp.float8_e4m3fn

D_MODEL = 2048
N_META = 16
GRID_W = 64
EPS = 1e-6
ROPE_THETA = 500000.0
AXIAL_THETA = 10000.0
BRANCH_W = 512
LANES = 128
SEQ_PAD = 512
ROW_TILE = 384
ATTN_ROWS = 768
KV_CHUNK = 1024
PV_ROW_BLOCKS = 2
A_QK = 64
A_ROT = 16
B_HD = 128
C_NOPE = 128
C_ROPE = 64
C_Q_LORA = 384
C_KV_LORA = 256
POOL_WINDOWS = (2, 4, 8, 16)
POOL_HALO = 8
LOG2E = 1.4426950408889634
NEG_BIG = -1e30
VMEM_LIMIT = 56 * 1024 * 1024

P_AQ, P_AV, P_AG = 0, 1024, 1536
P_BQ, P_BV, P_BG = 2048, 2816, 3072
P_CQ = 3584
P_TAIL = 4288
IN_COLS = 5824
P_HEAD_END = P_CQ + C_Q_LORA + C_KV_LORA + LANES


def _mean_sq(x, n):
    return jnp.sum(x * x, axis=-1, keepdims=True) * (1.0 / n)


def _half_mean_sq(x):
    xx = x * x
    lower = lax.broadcasted_iota(jnp.int32, x.shape, 1) < (LANES // 2)
    lo = jnp.sum(jnp.where(lower, xx, 0.0), axis=-1, keepdims=True)
    hi = jnp.sum(jnp.where(lower, 0.0, xx), axis=-1, keepdims=True)
    return jnp.where(lower, lo, hi) * (2.0 / LANES)


def _rope(x, c, s, half):
    lane = lax.broadcasted_iota(jnp.int32, x.shape, 1)
    fwd = pltpu.roll(x, LANES - half, axis=1)
    bwd = pltpu.roll(x, half, axis=1)
    partner = jnp.where((lane % (2 * half)) < half, fwd, bwd)
    return x * c + partner * s


def _proj_tile(h_ref, nw_ref, win_ref, wtail_ref, wuq_ref, wukv_ref,
               aqf_ref, akf_ref, aqn_ref, akn_ref, bqn_ref, bkn_ref, cql_ref, ckvl_ref,
               cqnn_ref, cqnr_ref, cknn_ref, cknr_ref,
               ca_ref, sa_ref, cb_ref, sb_ref, cc_ref, sc_ref,
               qa_ref, ka_ref, va_ref, qb_ref, kb_ref, vb_ref,
               qc_ref, kc_ref, vc_ref, sg_ref, du_ref):
    h = h_ref[...]
    ms = jnp.mean(h * h, axis=-1, keepdims=True)
    hb = (h * lax.rsqrt(ms + EPS) * nw_ref[...]).astype(BF16)

    def mm(lo, hi):
        return jnp.dot(hb, win_ref[:, lo:hi], preferred_element_type=F32)

    def mm_tail(blk):
        return jnp.dot(hb, wtail_ref[:, blk * BRANCH_W:(blk + 1) * BRANCH_W],
                       preferred_element_type=F32)

    ca, sa = ca_ref[...], sa_ref[...]
    cb, sb = cb_ref[...], sb_ref[...]
    cc, sc = cc_ref[...], sc_ref[...]

    pa = mm(P_AQ, P_AV)
    for blk in range(8):
        x = pa[:, blk * LANES:(blk + 1) * LANES]
        w = aqn_ref[...] if blk < 4 else akn_ref[...]
        x = x * lax.rsqrt(_half_mean_sq(x) + EPS) * w
        x = _rope(x, ca, sa, A_ROT // 2)
        if blk < 4:
            qa_ref[:, blk * LANES:(blk + 1) * LANES] = (x * aqf_ref[...]).astype(BF16)
        else:
            kh, kl = _f8_split((x * akf_ref[...]).astype(BF16).astype(F32))
            kl = kl * (1.0 / 16.0)
            khr = pltpu.roll(kh, A_QK, axis=1)
            klr = pltpu.roll(kl, A_QK, axis=1)
            lower = lax.broadcasted_iota(jnp.int32, kh.shape, 1) < A_QK
            parts = [jnp.where(lower, kh, khr * 0.0625), jnp.where(lower, kl, klr * 0.0625),
                     jnp.where(lower, khr, kh * 0.0625), jnp.where(lower, klr, kl * 0.0625)]
            for b, part in enumerate(parts):
                c0 = ((blk - 4) * 4 + b) * LANES
                ka_ref[:, c0:c0 + LANES] = part.astype(F8)
    va_ref[...] = mm(P_AV, P_AG).astype(BF16)

    pb = mm(P_BQ, P_BV)
    b_scale = (B_HD ** -0.5) * LOG2E
    for blk in range(6):
        x = pb[:, blk * LANES:(blk + 1) * LANES]
        w = bqn_ref[...] if blk < 4 else bkn_ref[...]
        x = x * lax.rsqrt(_mean_sq(x, B_HD) + EPS) * w
        x = _rope(x, cb, sb, B_HD // 4)
        if blk < 4:
            qb_ref[:, blk * LANES:(blk + 1) * LANES] = (x * b_scale).astype(BF16)
        else:
            kb_ref[:, (blk - 4) * LANES:(blk - 3) * LANES] = x.astype(BF16)
    vb_ref[...] = mm(P_BV, P_BG).astype(BF16)

    pc = mm(P_CQ, P_HEAD_END)
    cq = pc[:, 0:C_Q_LORA]
    cq = cq * lax.rsqrt(jnp.mean(cq * cq, axis=-1, keepdims=True) + EPS) * cql_ref[...]
    qfull = jnp.dot(cq.astype(BF16), wuq_ref[...], preferred_element_type=F32)
    ckv = pc[:, C_Q_LORA:C_Q_LORA + C_KV_LORA]
    ckv = ckv * lax.rsqrt(jnp.mean(ckv * ckv, axis=-1, keepdims=True) + EPS) * ckvl_ref[...]
    kvfull = jnp.dot(ckv.astype(BF16), wukv_ref[...], preferred_element_type=F32)
    kr = pc[:, C_Q_LORA + C_KV_LORA:]
    kr = jnp.where(lax.broadcasted_iota(jnp.int32, kr.shape, 1) < C_ROPE, kr, 0.0)
    kr = kr * lax.rsqrt(_mean_sq(kr, C_ROPE) + EPS) * cknr_ref[...]
    kr = _rope(kr, cc, sc, C_ROPE // 2).astype(BF16)
    c_scale = ((C_NOPE + C_ROPE) ** -0.5) * LOG2E
    for hd in range(4):
        base = hd * 2 * LANES
        qn = qfull[:, base:base + LANES]
        qn = qn * lax.rsqrt(_mean_sq(qn, C_NOPE) + EPS) * cqnn_ref[...]
        qr = qfull[:, base + LANES:base + 2 * LANES]
        qr = qr * lax.rsqrt(_mean_sq(qr, C_ROPE) + EPS) * cqnr_ref[...]
        qr = _rope(qr, cc, sc, C_ROPE // 2)
        qc_ref[:, base:base + LANES] = (qn * c_scale).astype(BF16)
        qc_ref[:, base + LANES:base + 2 * LANES] = (qr * c_scale).astype(BF16)
        kn = kvfull[:, base:base + LANES]
        kn = kn * lax.rsqrt(_mean_sq(kn, C_NOPE) + EPS) * cknn_ref[...]
        kc_ref[:, base:base + LANES] = kn.astype(BF16)
        kc_ref[:, base + LANES:base + 2 * LANES] = kr
        vc_ref[:, hd * LANES:(hd + 1) * LANES] = kvfull[:, base + LANES:base + 2 * LANES].astype(BF16)

    gates = [mm(P_AG, P_BQ), mm(P_BG, P_CQ), mm_tail(0), mm_tail(2)]
    for blk, g in enumerate(gates):
        sg_ref[:, blk * BRANCH_W:(blk + 1) * BRANCH_W] = jax.nn.silu(g).astype(BF16)
    du_ref[...] = mm_tail(1)


def _f8_split(x):
    hi = x.astype(F8).astype(F32)
    return hi, (x - hi) * 16.0


def _lane_repeat(x, n):
    return jnp.concatenate([x] * n, axis=1)


def _attn_tile(q_ref, k_ref, o_ref, vx_sc, s_sc, mc_sc, mr_sc, m_sc, acc_sc, *,
               mode, tq, ck, n_main, tail_valid):
    if mode == "A":
        qh, ql = _f8_split(q_ref[...].astype(F32))
        lower = lax.broadcasted_iota(jnp.int32, qh.shape, 1) < A_QK
        b0 = jnp.where(lower, qh, pltpu.roll(ql, A_QK, axis=1)).astype(F8)
        b1 = jnp.where(lower, pltpu.roll(qh, A_QK, axis=1), ql).astype(F8)
        q8 = [jnp.concatenate([b, b], axis=1) for b in (b0, b1)]
        rows = 2 * tq
    elif mode == "B":
        qs = jnp.concatenate([q_ref[:, 0:LANES], q_ref[:, LANES:2 * LANES]], axis=0)
        rows = qs.shape[0]
    else:
        qs = q_ref[...]
        rows = qs.shape[0]
    n_grp = rows // tq

    m_sc[...] = jnp.full(m_sc.shape, NEG_BIG, F32)
    acc_sc[...] = jnp.zeros(acc_sc.shape, F32)

    contract_last = (((1,), (1,)), ((), ()))

    def scores(start, width):
        if mode == "A":
            parts = [lax.dot_general(q8[m], k_ref[pl.ds(start, width), m * 2 * LANES:(m + 1) * 2 * LANES],
                                     contract_last, preferred_element_type=F32) for m in range(2)]
            return jnp.concatenate(parts, axis=0)
        return lax.dot_general(qs, k_ref[pl.ds(start, width), :], contract_last,
                               preferred_element_type=F32)

    def row_max(s):
        return jnp.broadcast_to(jnp.max(s, axis=1, keepdims=True), (rows, LANES))

    def lane_max(s):
        parts = [s[:, t * LANES:(t + 1) * LANES] for t in range(s.shape[1] // LANES)]
        return functools.reduce(jnp.maximum, parts)

    def qk_stage(j, slot):
        s = scores(pl.multiple_of(j * ck, ck), ck)
        s_sc[slot] = s
        mc_sc[slot] = lane_max(s)

    def softmax_pv(s_parts, m_cur, vx, after_block=()):
        m_prev = m_sc[...]
        m_new = jnp.maximum(m_prev, m_cur)
        alpha = jnp.exp2(m_prev - m_new)
        m_sc[...] = m_new
        assert len(after_block) <= PV_ROW_BLOCKS
        blk = rows // PV_ROW_BLOCKS
        for r0 in range(0, rows, blk):
            mr = m_new[r0:r0 + blk]
            p = jnp.concatenate(
                [jnp.exp2(s[r0:r0 + blk] - _lane_repeat(mr, s.shape[1] // LANES)).astype(BF16)
                 for s in s_parts], axis=1)
            pv = jnp.dot(p, vx, preferred_element_type=F32)
            acc_sc[r0:r0 + blk, :] = _lane_repeat(alpha[r0:r0 + blk], 2) * acc_sc[r0:r0 + blk, :] + pv
            if r0 // blk < len(after_block):
                after_block[r0 // blk]()

    def qk_pair(p, r):
        qk_stage(2 * p, 2 * r)
        qk_stage(2 * p + 1, 2 * r + 1)

    def reduce_pair(r):
        mr_sc[r] = row_max(jnp.maximum(mc_sc[2 * r], mc_sc[2 * r + 1]))

    def pv_pair(p, r, after_block=()):
        vx = vx_sc[pl.ds(pl.multiple_of(p * 2 * ck, 2 * ck), 2 * ck), :]
        softmax_pv([s_sc[2 * r], s_sc[2 * r + 1]], mr_sc[r], vx, after_block)

    n_pairs = n_main // 2
    qk_pair(0, 0)
    qk_pair(1, 1)
    reduce_pair(0)

    def body(i, carry):
        for r in range(3):
            p = 3 * i + r
            reduce_pair((r + 1) % 3)
            nxt = (r + 2) % 3
            if mode == "A":
                pv_pair(p, r, [functools.partial(qk_stage, 2 * (p + 2), 2 * nxt),
                               functools.partial(qk_stage, 2 * (p + 2) + 1, 2 * nxt + 1)])
            else:
                pv_pair(p, r)
                qk_pair(p + 2, nxt)
        return carry

    lax.fori_loop(0, (n_pairs - 2) // 3, body, 0)
    r = (n_pairs - 2) % 3
    pv_pair(n_pairs - 2, r)
    reduce_pair((r + 1) % 3)
    t0 = n_main * ck
    s_tail = scores(t0, LANES)
    col = lax.broadcasted_iota(jnp.int32, s_tail.shape, 1)
    s_tail = jnp.where(col < tail_valid, s_tail, NEG_BIG)
    pv_pair(n_pairs - 1, (r + 1) % 3)
    softmax_pv([s_tail], row_max(s_tail), vx_sc[t0:t0 + LANES, :])

    o = acc_sc[:, 0:LANES] / acc_sc[:, LANES:2 * LANES]
    for g in range(n_grp):
        o_ref[g] = o[g * tq:(g + 1) * tq]


def _attn_kernel(q_ref, k_ref, v_ref, o_ref, vx_sc, *scratch, tq, q_valid, **tile_args):
    lp = v_ref.shape[0]

    @pl.when(pl.program_id(1) == 0)
    def _():
        def fill(c, carry):
            r = pl.multiple_of(c * tq, tq)
            vx_sc[pl.ds(r, tq), 0:LANES] = v_ref[pl.ds(r, tq), :]
            vx_sc[pl.ds(r, tq), LANES:2 * LANES] = jnp.ones((tq, LANES), BF16)
            return carry
        lax.fori_loop(0, lp // tq, fill, 0)

    has_rows = pl.program_id(1) * tq < q_valid

    @pl.when(has_rows)
    def _():
        _attn_tile(q_ref, k_ref, o_ref, vx_sc, *scratch, tq=tq, **tile_args)

    @pl.when(jnp.logical_not(has_rows))
    def _():
        o_ref[...] = jnp.zeros(o_ref.shape, F32)


def _final_tile(h_ref, oa_ref, ob_ref, oc_ref, du_ref, dup_ref, dun_ref, sg_ref,
                wout_ref, lam_ref, subln_ref, wg_ref, dsc_ref, out_ref, ext_sc, *,
                lambda_init, seq_len, tl):
    i = pl.program_id(0)

    lp = lam_ref[...]
    lam = (jnp.exp(jnp.sum(lp[0:1, :] * lp[1:2, :], axis=-1, keepdims=True))
           - jnp.exp(jnp.sum(lp[2:3, :] * lp[3:4, :], axis=-1, keepdims=True)) + lambda_init)

    ys = []
    for hd in range(4):
        d = oa_ref[hd, 0] - lam * oa_ref[hd, 1]
        d = d * lax.rsqrt(_mean_sq(d, LANES) + EPS) * subln_ref[...]
        d = d * (1.0 - lambda_init)
        ys.append(d * sg_ref[:, hd * LANES:(hd + 1) * LANES].astype(F32))
    for g in range(2):
        for r in range(2):
            c = 4 + 2 * g + r
            ys.append(ob_ref[g, r] * sg_ref[:, c * LANES:(c + 1) * LANES].astype(F32))
    for hd in range(4):
        c = 8 + hd
        ys.append(oc_ref[hd, 0] * sg_ref[:, c * LANES:(c + 1) * LANES].astype(F32))

    row0 = i * tl
    def masked(x, first_row):
        r = first_row + lax.broadcasted_iota(jnp.int32, x.shape, 0)
        return jnp.where((r >= 0) & (r < seq_len), x, 0.0)
    ext_sc[0:POOL_HALO, :] = masked(dup_ref[...], row0 - POOL_HALO)
    ext_sc[POOL_HALO:POOL_HALO + tl, :] = masked(du_ref[...], row0)
    ext_sc[POOL_HALO + tl:2 * POOL_HALO + tl, :] = masked(dun_ref[...], row0 + tl)
    t = row0 + lax.broadcasted_iota(jnp.int32, (tl, LANES), 0)
    for g, w in enumerate(POOL_WINDOWS):
        cols = slice(g * LANES, (g + 1) * LANES)
        acc = ext_sc[POOL_HALO - w // 2:POOL_HALO - w // 2 + tl, cols]
        for j in range(-w // 2 + 1, w // 2):
            acc = acc + ext_sc[POOL_HALO + j:POOL_HALO + j + tl, cols]
        cnt = jnp.minimum(t + w // 2, seq_len) - jnp.maximum(t - w // 2, 0)
        cnt = jnp.maximum(cnt, 1).astype(F32)
        pooled = acc / cnt - ext_sc[POOL_HALO:POOL_HALO + tl, cols]
        mixed = jnp.dot(pooled.astype(BF16), wg_ref[g], preferred_element_type=F32)
        c = 12 + g
        ys.append(mixed * dsc_ref[:, cols] * sg_ref[:, c * LANES:(c + 1) * LANES].astype(F32))

    y = jnp.concatenate([v.astype(BF16) for v in ys], axis=1)
    out_ref[...] = h_ref[...] + jnp.dot(y, wout_ref[...], preferred_element_type=F32)


def _rope_lane_tables(n_tok, lp):
    l = N_META + n_tok
    rows = n_tok // GRID_W
    lane = jnp.arange(LANES, dtype=jnp.int32)

    def inv_freq(idx, dim, theta):
        return theta ** (-(2 * idx).astype(F32) / dim)

    def padded(v):
        return jnp.pad(v.astype(F32), (0, lp - l))[:, None]

    pos = padded(jnp.arange(l, dtype=jnp.int32))
    row_pos = padded(jnp.concatenate([jnp.full((N_META,), -1, jnp.int32),
                                      jnp.repeat(jnp.arange(rows, dtype=jnp.int32), GRID_W)]))
    col_pos = padded(jnp.concatenate([jnp.arange(N_META, dtype=jnp.int32),
                                      jnp.tile(jnp.arange(GRID_W, dtype=jnp.int32), rows)]))

    ja = lane % A_QK
    inv_a = jnp.where(ja < A_ROT, inv_freq(ja % (A_ROT // 2), A_ROT, ROPE_THETA), 0.0)
    sign_a = jnp.where(ja < A_ROT // 2, -1.0, 1.0).astype(F32)
    ang_a = pos * inv_a[None, :]
    jb = lane % (B_HD // 2)
    inv_b = inv_freq(jb % (B_HD // 4), B_HD // 2, AXIAL_THETA)
    inv_r = jnp.where(lane < B_HD // 2, inv_b, 0.0)
    inv_c = jnp.where(lane < B_HD // 2, 0.0, inv_b)
    sign_b = jnp.where(jb < B_HD // 4, -1.0, 1.0).astype(F32)
    ang_b = row_pos * inv_r[None, :] + col_pos * inv_c[None, :]
    inv_m = jnp.where(lane < C_ROPE, inv_freq(lane % (C_ROPE // 2), C_ROPE, ROPE_THETA), 0.0)
    sign_m = jnp.where(lane < C_ROPE // 2, -1.0, 1.0).astype(F32)
    ang_m = pos * inv_m[None, :]
    return (jnp.cos(ang_a), jnp.sin(ang_a) * sign_a[None, :],
            jnp.cos(ang_b), jnp.sin(ang_b) * sign_b[None, :],
            jnp.cos(ang_m), jnp.sin(ang_m) * sign_m[None, :])


def _row_tile_kernel(tile_fn, n_out, n_scratch, tl, seq_len):
    def kern(*refs):
        outs = refs[len(refs) - n_scratch - n_out:len(refs) - n_scratch]
        has_rows = pl.program_id(0) * tl < seq_len

        @pl.when(has_rows)
        def _():
            tile_fn(*refs)

        @pl.when(jnp.logical_not(has_rows))
        def _():
            for o in outs:
                o[...] = jnp.zeros(o.shape, o.dtype)
    return kern


class _TokenRows:
    def __init__(self, meta_ref, prev_ref, body_refs, seq_len, tl):
        self._parts = (meta_ref, prev_ref, body_refs)
        self._seq_len, self._tl = seq_len, tl

    def __getitem__(self, idx):
        meta_ref, prev_ref, body_refs = self._parts
        i = pl.program_id(0)
        head = jnp.where(i == 0, meta_ref[...], prev_ref[...])
        body = [r[...] for r in body_refs]
        body[-1] = body[-1][:body[-1].shape[0] - N_META]
        rows = jnp.concatenate([head] + body, axis=0)
        r_nat = i * self._tl + lax.broadcasted_iota(jnp.int32, rows.shape, 0)
        return jnp.where(r_nat < self._seq_len, rows, 0.0)


def _with_token_rows(tile_fn, seq_len, tl):
    def fn(meta_ref, prev_ref, b0_ref, b1_ref, b2_ref, *rest):
        tile_fn(_TokenRows(meta_ref, prev_ref, (b0_ref, b1_ref, b2_ref), seq_len, tl), *rest)
    return fn


def _token_row_specs(tl, n_tok):
    sub = tl // 3
    n_sub = n_tok // sub
    return [_const_spec((N_META, D_MODEL)),
            pl.BlockSpec((N_META, D_MODEL),
                         lambda i: (jnp.clip(i * (tl // N_META) - 1, 0, n_tok // N_META - 1), 0)),
            pl.BlockSpec((sub, D_MODEL), lambda i: (jnp.minimum(3 * i, n_sub - 1), 0)),
            pl.BlockSpec((sub, D_MODEL), lambda i: (jnp.minimum(3 * i + 1, n_sub - 1), 0)),
            pl.BlockSpec((sub, D_MODEL), lambda i: (jnp.minimum(3 * i + 2, n_sub - 1), 0))]


def _row_spec(tl, width):
    return pl.BlockSpec((tl, width), lambda i: (i, 0))


def _const_spec(shape):
    nd = len(shape)
    return pl.BlockSpec(shape, lambda i: (0,) * nd, pipeline_mode=pl.Buffered(1))


def _proj_call(h, lw, tables, lp, tl, seq_len):
    n_tiles = lp // tl
    small = [lw["aqf"], lw["akf"], lw["aqn"], lw["akn"], lw["bqn"], lw["bkn"], lw["cql"], lw["ckvl"],
             lw["cqnn"], lw["cqnr"], lw["cknn"], lw["cknr"]]
    first = isinstance(h, tuple)
    h_specs = _token_row_specs(tl, h[1].shape[0]) if first else [_row_spec(tl, D_MODEL)]
    h_args = (h[0], h[1], h[1], h[1], h[1]) if first else (h,)
    tile_fn = _with_token_rows(_proj_tile, seq_len, tl) if first else _proj_tile
    in_specs = (h_specs + [_const_spec((1, D_MODEL)),
                 pl.BlockSpec((None, D_MODEL, P_HEAD_END), lambda i: (lw["layer"], 0, 0),
                              pipeline_mode=pl.Buffered(1)),
                 _const_spec((D_MODEL, IN_COLS - P_TAIL)),
                 _const_spec(lw["wuq"].shape),
                 _const_spec(lw["wukv"].shape)]
                + [_const_spec(a.shape) for a in small]
                + [_row_spec(tl, LANES)] * 6)
    out_widths = [512, 2048, 512, 512, 256, 256, 1024, 1024, 512, 2048, 512]
    out_dtypes = [BF16, F8] + [BF16] * 8 + [F32]
    out_shape = [jax.ShapeDtypeStruct((lp, w), dt) for w, dt in zip(out_widths, out_dtypes)]
    out_specs = [_row_spec(tl, w) for w in out_widths]
    return pl.pallas_call(
        _row_tile_kernel(tile_fn, len(out_widths), 0, tl, seq_len),
        grid=(n_tiles,),
        in_specs=in_specs,
        out_specs=out_specs,
        out_shape=out_shape,
        compiler_params=pltpu.CompilerParams(
            dimension_semantics=("arbitrary",), vmem_limit_bytes=VMEM_LIMIT),
        name="proj",
    )(*h_args, lw["nw"], lw["win"], lw["wtail"], lw["wuq"], lw["wukv"], *small, *tables)


def _attn_call(q, k, v, mode, lp, n_tok, tq, ck):
    n_heads = {"A": 4, "B": 2, "C": 4}[mode]
    n_grp = {"A": 2, "B": 2, "C": 1}[mode]
    qw = {"A": LANES, "B": 2 * LANES, "C": 2 * LANES}[mode]
    kw = {"A": 4 * LANES, "B": LANES, "C": 2 * LANES}[mode]
    n_main = n_tok // ck
    tail_valid = N_META + n_tok - n_main * ck
    kern = functools.partial(_attn_kernel, mode=mode, tq=tq, ck=ck, n_main=n_main,
                             tail_valid=tail_valid, q_valid=N_META + n_tok)
    return pl.pallas_call(
        kern,
        grid=(n_heads, lp // tq),
        in_specs=[pl.BlockSpec((tq, qw), lambda hd, i: (i, hd)),
                  pl.BlockSpec((lp, kw), lambda hd, i: (0, hd), pipeline_mode=pl.Buffered(1)),
                  pl.BlockSpec((lp, LANES), lambda hd, i: (0, hd), pipeline_mode=pl.Buffered(1))],
        out_specs=pl.BlockSpec((None, n_grp, tq, LANES), lambda hd, i: (hd, 0, i, 0)),
        out_shape=jax.ShapeDtypeStruct((n_heads, n_grp, lp, LANES), F32),
        scratch_shapes=[pltpu.VMEM((lp, 2 * LANES), BF16),
                        pltpu.VMEM((6, n_grp * tq, ck), F32),
                        pltpu.VMEM((6, n_grp * tq, LANES), F32),
                        pltpu.VMEM((3, n_grp * tq, LANES), F32),
                        pltpu.VMEM((n_grp * tq, LANES), F32),
                        pltpu.VMEM((n_grp * tq, 2 * LANES), F32)],
        compiler_params=pltpu.CompilerParams(
            dimension_semantics=("arbitrary", "arbitrary"), vmem_limit_bytes=VMEM_LIMIT),
        name="attn_" + mode,
    )(q, k, v)


def _final_call(h, oa, ob, oc, du, sg, lw, lambda_init, lp, n_tok, tl):
    n_tiles = lp // tl
    hb = tl // POOL_HALO
    n_hblk = lp // POOL_HALO
    seq_len = N_META + n_tok
    tile_fn = functools.partial(_final_tile, lambda_init=lambda_init, seq_len=seq_len, tl=tl)
    first = isinstance(h, tuple)
    if first:
        tile_fn = _with_token_rows(tile_fn, seq_len, tl)
    h_specs = _token_row_specs(tl, h[1].shape[0]) if first else [_row_spec(tl, D_MODEL)]
    h_args = (h[0], h[1], h[1], h[1], h[1]) if first else (h,)
    kern = _row_tile_kernel(tile_fn, 1, 1, tl, seq_len)
    in_specs = h_specs + [
        pl.BlockSpec((4, 2, tl, LANES), lambda i: (0, 0, i, 0)),
        pl.BlockSpec((2, 2, tl, LANES), lambda i: (0, 0, i, 0)),
        pl.BlockSpec((4, 1, tl, LANES), lambda i: (0, 0, i, 0)),
        _row_spec(tl, BRANCH_W),
        pl.BlockSpec((POOL_HALO, BRANCH_W), lambda i: (jnp.maximum(i * hb - 1, 0), 0)),
        pl.BlockSpec((POOL_HALO, BRANCH_W), lambda i: (jnp.minimum((i + 1) * hb, n_hblk - 1), 0)),
        _row_spec(tl, D_MODEL),
        pl.BlockSpec((None, D_MODEL, D_MODEL), lambda i: (lw["layer"], 0, 0),
                     pipeline_mode=pl.Buffered(1)),
        _const_spec((4, A_QK)),
        _const_spec((1, LANES)),
        _const_spec((4, LANES, LANES)),
        _const_spec((1, BRANCH_W)),
    ]
    return pl.pallas_call(
        kern,
        grid=(n_tiles,),
        in_specs=in_specs,
        out_specs=_row_spec(tl, D_MODEL),
        out_shape=jax.ShapeDtypeStruct((lp, D_MODEL), F32),
        scratch_shapes=[pltpu.VMEM((tl + 2 * POOL_HALO, BRANCH_W), F32)],
        compiler_params=pltpu.CompilerParams(
            dimension_semantics=("arbitrary",), vmem_limit_bytes=VMEM_LIMIT),
        name="final",
    )(*h_args, oa, ob, oc, du, du, du, sg, lw["wout"], lw["lam"], lw["subln"], lw["wg"], lw["dsc"])


def _layer_weights(i, norm_w, w_in, w_in_bf16, w_out, a_q_norm, a_k_norm, a_lambda, a_subln, b_q_norm,
                   b_k_norm, c_q_lat_norm, c_kv_lat_norm, c_w_uq, c_w_ukv, c_q_norm, c_k_norm,
                   d_w_group, d_scale):
    wtail = w_in[i, :, P_TAIL:].astype(BF16)
    uq = c_w_uq[i].reshape(C_Q_LORA, 4, C_NOPE + C_ROPE)
    uq = jnp.pad(uq, ((0, 0), (0, 0), (0, 2 * LANES - C_NOPE - C_ROPE)))
    wuq = uq.reshape(C_Q_LORA, 4 * 2 * LANES).astype(BF16)
    row = lambda v: v.reshape(1, -1).astype(F32)
    a_scale = (A_QK ** -0.5) * LOG2E
    tiny = jnp.finfo(F32).tiny
    q_mag = jnp.maximum(jnp.max(jnp.abs(a_q_norm[i])).astype(F32) * a_scale, tiny)
    k_mag = jnp.maximum(jnp.max(jnp.abs(a_k_norm[i])).astype(F32), tiny)
    shift = jnp.clip(jnp.round(0.5 * (jnp.log2(k_mag) - jnp.log2(q_mag))), -60.0, 60.0)
    aqf = jnp.full((1, LANES), a_scale, F32) * jnp.exp2(shift)
    akf = jnp.full((1, LANES), 1.0, F32) * jnp.exp2(-shift)
    pad_row = lambda v: jnp.pad(v.astype(F32), (0, LANES - v.shape[0])).reshape(1, LANES)
    return dict(
        layer=i, nw=row(norm_w[i]), win=w_in_bf16, wtail=wtail, wuq=wuq, aqf=aqf, akf=akf, wukv=c_w_ukv[i].astype(BF16),
        aqn=row(jnp.tile(a_q_norm[i], 2)), akn=row(jnp.tile(a_k_norm[i], 2)),
        bqn=row(b_q_norm[i]), bkn=row(b_k_norm[i]),
        cql=row(c_q_lat_norm[i]), ckvl=row(c_kv_lat_norm[i]),
        cqnn=row(c_q_norm[i][:C_NOPE]), cqnr=pad_row(c_q_norm[i][C_NOPE:]),
        cknn=row(c_k_norm[i][:C_NOPE]), cknr=pad_row(c_k_norm[i][C_NOPE:]),
        wout=w_out.astype(BF16), lam=a_lambda[i].astype(F32), subln=row(a_subln[i]),
        wg=d_w_group[i].astype(BF16), dsc=row(d_scale[i]),
    )


def kernel(x, meta_tokens, norm_w, w_in, w_out, a_q_norm, a_k_norm, a_lambda, a_subln, b_q_norm,
           b_k_norm, c_q_lat_norm, c_kv_lat_norm, c_w_uq, c_w_ukv, c_q_norm, c_k_norm,
           d_w_group, d_scale):
    b, n_tok, d = x.shape
    assert b == 1 and d == D_MODEL
    lp = n_tok + SEQ_PAD
    tl = ROW_TILE
    ck = KV_CHUNK
    assert lp % tl == 0 and lp % ATTN_ROWS == 0 and n_tok % GRID_W == 0
    assert n_tok % (2 * ck) == 0 and (n_tok // (2 * ck) - 2) % 3 == 0
    depth = norm_w.shape[0]

    assert tl % 3 == 0 and n_tok % (tl // 3) == 0 and tl % N_META == 0
    h = (meta_tokens.astype(x.dtype), x[0])
    tables = _rope_lane_tables(n_tok, lp)

    w_in_bf16 = w_in.astype(BF16)
    for i in range(depth):
        lambda_init = 0.8 - 0.6 * math.exp(-0.3 * i)
        lw = _layer_weights(i, norm_w, w_in, w_in_bf16, w_out, a_q_norm, a_k_norm, a_lambda, a_subln,
                            b_q_norm, b_k_norm, c_q_lat_norm, c_kv_lat_norm, c_w_uq, c_w_ukv,
                            c_q_norm, c_k_norm, d_w_group, d_scale)
        qa, ka, va, qb, kb, vb, qc, kc, vc, sg, du = _proj_call(h, lw, tables, lp, tl, N_META + n_tok)
        oa = _attn_call(qa, ka, va, "A", lp, n_tok, ATTN_ROWS // 2, ck)
        ob = _attn_call(qb, kb, vb, "B", lp, n_tok, ATTN_ROWS // 2, ck)
        oc = _attn_call(qc, kc, vc, "C", lp, n_tok, ATTN_ROWS, ck)
        h = _final_call(h, oa, ob, oc, du, sg, lw, lambda_init, lp, n_tok, tl)
    return h[N_META:N_META + n_tok][None]
```

```python
import functools
import math

import jax
import jax.numpy as jnp
from jax import lax
from jax.experimental import pallas as pl
from jax.experimental.pallas import tpu as pltpu

F32 = jnp.float32
BF16 = jnp.bfloat16
F8 = jnp.float8_e4m3fn

D_MODEL = 2048
N_META = 16
GRID_W = 64
EPS = 1e-6
ROPE_THETA = 500000.0
AXIAL_THETA = 10000.0
BRANCH_W = 512
LANES = 128
SEQ_PAD = 512
ROW_TILE = 384
ATTN_ROWS = 768
KV_CHUNK = 1024
PV_ROW_BLOCKS = 2
A_QK = 64
A_ROT = 16
B_HD = 128
C_NOPE = 128
C_ROPE = 64
C_Q_LORA = 384
C_KV_LORA = 256
POOL_WINDOWS = (2, 4, 8, 16)
POOL_HALO = 8
LOG2E = 1.4426950408889634
NEG_BIG = -1e30
VMEM_LIMIT = 56 * 1024 * 1024

P_AQ, P_AV, P_AG = 0, 1024, 1536
P_BQ, P_BV, P_BG = 2048, 2816, 3072
P_CQ = 3584
P_TAIL = 4288
IN_COLS = 5824
P_HEAD_END = P_CQ + C_Q_LORA + C_KV_LORA + LANES


def _mean_sq(x, n):
    return jnp.sum(x * x, axis=-1, keepdims=True) * (1.0 / n)


def _half_mean_sq(x):
    xx = x * x
    lower = lax.broadcasted_iota(jnp.int32, x.shape, 1) < (LANES // 2)
    lo = jnp.sum(jnp.where(lower, xx, 0.0), axis=-1, keepdims=True)
    hi = jnp.sum(jnp.where(lower, 0.0, xx), axis=-1, keepdims=True)
    return jnp.where(lower, lo, hi) * (2.0 / LANES)


def _rope(x, c, s, half):
    lane = lax.broadcasted_iota(jnp.int32, x.shape, 1)
    fwd = pltpu.roll(x, LANES - half, axis=1)
    bwd = pltpu.roll(x, half, axis=1)
    partner = jnp.where((lane % (2 * half)) < half, fwd, bwd)
    return x * c + partner * s


def _proj_tile(h_ref, nw_ref, win_ref, wtail_ref, wuq_ref, wukv_ref,
               aqf_ref, akf_ref, aqn_ref, akn_ref, bqn_ref, bkn_ref, cql_ref, ckvl_ref,
               cqnn_ref, cqnr_ref, cknn_ref, cknr_ref,
               ca_ref, sa_ref, cb_ref, sb_ref, cc_ref, sc_ref,
               qa_ref, ka_ref, va_ref, qb_ref, kb_ref, vb_ref,
               qc_ref, kc_ref, vc_ref, sg_ref, du_ref):
    h = h_ref[...]
    ms = jnp.mean(h * h, axis=-1, keepdims=True)
    hb = (h * lax.rsqrt(ms + EPS) * nw_ref[...]).astype(BF16)

    def mm(lo, hi):
        return jnp.dot(hb, win_ref[:, lo:hi], preferred_element_type=F32)

    def mm_tail(blk):
        return jnp.dot(hb, wtail_ref[:, blk * BRANCH_W:(blk + 1) * BRANCH_W],
                       preferred_element_type=F32)

    ca, sa = ca_ref[...], sa_ref[...]
    cb, sb = cb_ref[...], sb_ref[...]
    cc, sc = cc_ref[...], sc_ref[...]

    pa = mm(P_AQ, P_AV)
    for blk in range(8):
        x = pa[:, blk * LANES:(blk + 1) * LANES]
        w = aqn_ref[...] if blk < 4 else akn_ref[...]
        x = x * lax.rsqrt(_half_mean_sq(x) + EPS) * w
        x = _rope(x, ca, sa, A_ROT // 2)
        if blk < 4:
            qa_ref[:, blk * LANES:(blk + 1) * LANES] = (x * aqf_ref[...]).astype(BF16)
        else:
            kh, kl = _f8_split((x * akf_ref[...]).astype(BF16).astype(F32))
            kl = kl * (1.0 / 16.0)
            khr = pltpu.roll(kh, A_QK, axis=1)
            klr = pltpu.roll(kl, A_QK, axis=1)
            lower = lax.broadcasted_iota(jnp.int32, kh.shape, 1) < A_QK
            parts = [jnp.where(lower, kh, khr * 0.0625), jnp.where(lower, kl, klr * 0.0625),
                     jnp.where(lower, khr, kh * 0.0625), jnp.where(lower, klr, kl * 0.0625)]
            for b, part in enumerate(parts):
                c0 = ((blk - 4) * 4 + b) * LANES
                ka_ref[:, c0:c0 + LANES] = part.astype(F8)
    va_ref[...] = mm(P_AV, P_AG).astype(BF16)

    pb = mm(P_BQ, P_BV)
    b_scale = (B_HD ** -0.5) * LOG2E
    for blk in range(6):
        x = pb[:, blk * LANES:(blk + 1) * LANES]
        w = bqn_ref[...] if blk < 4 else bkn_ref[...]
        x = x * lax.rsqrt(_mean_sq(x, B_HD) + EPS) * w
        x = _rope(x, cb, sb, B_HD // 4)
        if blk < 4:
            qb_ref[:, blk * LANES:(blk + 1) * LANES] = (x * b_scale).astype(BF16)
        else:
            kb_ref[:, (blk - 4) * LANES:(blk - 3) * LANES] = x.astype(BF16)
    vb_ref[...] = mm(P_BV, P_BG).astype(BF16)

    pc = mm(P_CQ, P_HEAD_END)
    cq = pc[:, 0:C_Q_LORA]
    cq = cq * lax.rsqrt(jnp.mean(cq * cq, axis=-1, keepdims=True) + EPS) * cql_ref[...]
    qfull = jnp.dot(cq.astype(BF16), wuq_ref[...], preferred_element_type=F32)
    ckv = pc[:, C_Q_LORA:C_Q_LORA + C_KV_LORA]
    ckv = ckv * lax.rsqrt(jnp.mean(ckv * ckv, axis=-1, keepdims=True) + EPS) * ckvl_ref[...]
    kvfull = jnp.dot(ckv.astype(BF16), wukv_ref[...], preferred_element_type=F32)
    kr = pc[:, C_Q_LORA + C_KV_LORA:]
    kr = jnp.where(lax.broadcasted_iota(jnp.int32, kr.shape, 1) < C_ROPE, kr, 0.0)
    kr = kr * lax.rsqrt(_mean_sq(kr, C_ROPE) + EPS) * cknr_ref[...]
    kr = _rope(kr, cc, sc, C_ROPE // 2).astype(BF16)
    c_scale = ((C_NOPE + C_ROPE) ** -0.5) * LOG2E
    for hd in range(4):
        base = hd * 2 * LANES
        qn = qfull[:, base:base + LANES]
        qn = qn * lax.rsqrt(_mean_sq(qn, C_NOPE) + EPS) * cqnn_ref[...]
        qr = qfull[:, base + LANES:base + 2 * LANES]
        qr = qr * lax.rsqrt(_mean_sq(qr, C_ROPE) + EPS) * cqnr_ref[...]
        qr = _rope(qr, cc, sc, C_ROPE // 2)
        qc_ref[:, base:base + LANES] = (qn * c_scale).astype(BF16)
        qc_ref[:, base + LANES:base + 2 * LANES] = (qr * c_scale).astype(BF16)
        kn = kvfull[:, base:base + LANES]
        kn = kn * lax.rsqrt(_mean_sq(kn, C_NOPE) + EPS) * cknn_ref[...]
        kc_ref[:, base:base + LANES] = kn.astype(BF16)
        kc_ref[:, base + LANES:base + 2 * LANES] = kr
        vc_ref[:, hd * LANES:(hd + 1) * LANES] = kvfull[:, base + LANES:base + 2 * LANES].astype(BF16)

    gates = [mm(P_AG, P_BQ), mm(P_BG, P_CQ), mm_tail(0), mm_tail(2)]
    for blk, g in enumerate(gates):
        sg_ref[:, blk * BRANCH_W:(blk + 1) * BRANCH_W] = jax.nn.silu(g).astype(BF16)
    du_ref[...] = mm_tail(1)


def _f8_split(x):
    hi = x.astype(F8).astype(F32)
    return hi, (x - hi) * 16.0


def _lane_repeat(x, n):
    return jnp.concatenate([x] * n, axis=1)


def _attn_tile(q_ref, k_ref, o_ref, vx_sc, s_sc, mc_sc, mr_sc, m_sc, acc_sc, *,
               mode, tq, ck, n_main, tail_valid):
    if mode == "A":
        qh, ql = _f8_split(q_ref[...].astype(F32))
        lower = lax.broadcasted_iota(jnp.int32, qh.shape, 1) < A_QK
        b0 = jnp.where(lower, qh, pltpu.roll(ql, A_QK, axis=1)).astype(F8)
        b1 = jnp.where(lower, pltpu.roll(qh, A_QK, axis=1), ql).astype(F8)
        q8 = [jnp.concatenate([b, b], axis=1) for b in (b0, b1)]
        rows = 2 * tq
    elif mode == "B":
        qs = jnp.concatenate([q_ref[:, 0:LANES], q_ref[:, LANES:2 * LANES]], axis=0)
        rows = qs.shape[0]
    else:
        qs = q_ref[...]
        rows = qs.shape[0]
    n_grp = rows // tq

    m_sc[...] = jnp.full(m_sc.shape, NEG_BIG, F32)
    acc_sc[...] = jnp.zeros(acc_sc.shape, F32)

    contract_last = (((1,), (1,)), ((), ()))

    def scores(start, width):
        if mode == "A":
            parts = [lax.dot_general(q8[m], k_ref[pl.ds(start, width), m * 2 * LANES:(m + 1) * 2 * LANES],
                                     contract_last, preferred_element_type=F32) for m in range(2)]
            return jnp.concatenate(parts, axis=0)
        return lax.dot_general(qs, k_ref[pl.ds(start, width), :], contract_last,
                               preferred_element_type=F32)

    def row_max(s):
        return jnp.broadcast_to(jnp.max(s, axis=1, keepdims=True), (rows, LANES))

    def lane_max(s):
        parts = [s[:, t * LANES:(t + 1) * LANES] for t in range(s.shape[1] // LANES)]
        return functools.reduce(jnp.maximum, parts)

    def qk_stage(j, slot):
        s = scores(pl.multiple_of(j * ck, ck), ck)
        s_sc[slot] = s
        mc_sc[slot] = lane_max(s)

    def softmax_pv(s_parts, m_cur, vx, after_block=()):
        m_prev = m_sc[...]
        m_new = jnp.maximum(m_prev, m_cur)
        alpha = jnp.exp2(m_prev - m_new)
        m_sc[...] = m_new
        assert len(after_block) <= PV_ROW_BLOCKS
        blk = rows // PV_ROW_BLOCKS
        for r0 in range(0, rows, blk):
            mr = m_new[r0:r0 + blk]
            p = jnp.concatenate(
                [jnp.exp2(s[r0:r0 + blk] - _lane_repeat(mr, s.shape[1] // LANES)).astype(BF16)
                 for s in s_parts], axis=1)
            pv = jnp.dot(p, vx, preferred_element_type=F32)
            acc_sc[r0:r0 + blk, :] = _lane_repeat(alpha[r0:r0 + blk], 2) * acc_sc[r0:r0 + blk, :] + pv
            if r0 // blk < len(after_block):
                after_block[r0 // blk]()

    def qk_pair(p, r):
        qk_stage(2 * p, 2 * r)
        qk_stage(2 * p + 1, 2 * r + 1)

    def reduce_pair(r):
        mr_sc[r] = row_max(jnp.maximum(mc_sc[2 * r], mc_sc[2 * r + 1]))

    def pv_pair(p, r, after_block=()):
        vx = vx_sc[pl.ds(pl.multiple_of(p * 2 * ck, 2 * ck), 2 * ck), :]
        softmax_pv([s_sc[2 * r], s_sc[2 * r + 1]], mr_sc[r], vx, after_block)

    n_pairs = n_main // 2
    qk_pair(0, 0)
    qk_pair(1, 1)
    reduce_pair(0)

    def body(i, carry):
        for r in range(3):
            p = 3 * i + r
            reduce_pair((r + 1) % 3)
            nxt = (r + 2) % 3
            if mode == "A":
                pv_pair(p, r, [functools.partial(qk_stage, 2 * (p + 2), 2 * nxt),
                               functools.partial(qk_stage, 2 * (p + 2) + 1, 2 * nxt + 1)])
            else:
                pv_pair(p, r)
                qk_pair(p + 2, nxt)
        return carry

    lax.fori_loop(0, (n_pairs - 2) // 3, body, 0)
    r = (n_pairs - 2) % 3
    pv_pair(n_pairs - 2, r)
    reduce_pair((r + 1) % 3)
    t0 = n_main * ck
    s_tail = scores(t0, LANES)
    col = lax.broadcasted_iota(jnp.int32, s_tail.shape, 1)
    s_tail = jnp.where(col < tail_valid, s_tail, NEG_BIG)
    pv_pair(n_pairs - 1, (r + 1) % 3)
    softmax_pv([s_tail], row_max(s_tail), vx_sc[t0:t0 + LANES, :])

    o = acc_sc[:, 0:LANES] / acc_sc[:, LANES:2 * LANES]
    for g in range(n_grp):
        o_ref[g] = o[g * tq:(g + 1) * tq]


def _attn_kernel(q_ref, k_ref, v_ref, o_ref, vx_sc, *scratch, tq, q_valid, **tile_args):
    lp = v_ref.shape[0]

    @pl.when(pl.program_id(1) == 0)
    def _():
        def fill(c, carry):
            r = pl.multiple_of(c * tq, tq)
            vx_sc[pl.ds(r, tq), 0:LANES] = v_ref[pl.ds(r, tq), :]
            vx_sc[pl.ds(r, tq), LANES:2 * LANES] = jnp.ones((tq, LANES), BF16)
            return carry
        lax.fori_loop(0, lp // tq, fill, 0)

    has_rows = pl.program_id(1) * tq < q_valid

    @pl.when(has_rows)
    def _():
        _attn_tile(q_ref, k_ref, o_ref, vx_sc, *scratch, tq=tq, **tile_args)

    @pl.when(jnp.logical_not(has_rows))
    def _():
        o_ref[...] = jnp.zeros(o_ref.shape, F32)


def _final_tile(h_ref, oa_ref, ob_ref, oc_ref, du_ref, dup_ref, dun_ref, sg_ref,
                wout_ref, lam_ref, subln_ref, wg_ref, dsc_ref, out_ref, ext_sc, *,
                lambda_init, seq_len, tl):
    i = pl.program_id(0)

    lp = lam_ref[...]
    lam = (jnp.exp(jnp.sum(lp[0:1, :] * lp[1:2, :], axis=-1, keepdims=True))
           - jnp.exp(jnp.sum(lp[2:3, :] * lp[3:4, :], axis=-1, keepdims=True)) + lambda_init)

    ys = []
    for hd in range(4):
        d = oa_ref[hd, 0] - lam * oa_ref[hd, 1]
        d = d * lax.rsqrt(_mean_sq(d, LANES) + EPS) * subln_ref[...]
        d = d * (1.0 - lambda_init)
        ys.append(d * sg_ref[:, hd * LANES:(hd + 1) * LANES].astype(F32))
    for g in range(2):
        for r in range(2):
            c = 4 + 2 * g + r
            ys.append(ob_ref[g, r] * sg_ref[:, c * LANES:(c + 1) * LANES].astype(F32))
    for hd in range(4):
        c = 8 + hd
        ys.append(oc_ref[hd, 0] * sg_ref[:, c * LANES:(c + 1) * LANES].astype(F32))

    row0 = i * tl
    def masked(x, first_row):
        r = first_row + lax.broadcasted_iota(jnp.int32, x.shape, 0)
        return jnp.where((r >= 0) & (r < seq_len), x, 0.0)
    ext_sc[0:POOL_HALO, :] = masked(dup_ref[...], row0 - POOL_HALO)
    ext_sc[POOL_HALO:POOL_HALO + tl, :] = masked(du_ref[...], row0)
    ext_sc[POOL_HALO + tl:2 * POOL_HALO + tl, :] = masked(dun_ref[...], row0 + tl)
    t = row0 + lax.broadcasted_iota(jnp.int32, (tl, LANES), 0)
    for g, w in enumerate(POOL_WINDOWS):
        cols = slice(g * LANES, (g + 1) * LANES)
        acc = ext_sc[POOL_HALO - w // 2:POOL_HALO - w // 2 + tl, cols]
        for j in range(-w // 2 + 1, w // 2):
            acc = acc + ext_sc[POOL_HALO + j:POOL_HALO + j + tl, cols]
        cnt = jnp.minimum(t + w // 2, seq_len) - jnp.maximum(t - w // 2, 0)
        cnt = jnp.maximum(cnt, 1).astype(F32)
        pooled = acc / cnt - ext_sc[POOL_HALO:POOL_HALO + tl, cols]
        mixed = jnp.dot(pooled.astype(BF16), wg_ref[g], preferred_element_type=F32)
        c = 12 + g
        ys.append(mixed * dsc_ref[:, cols] * sg_ref[:, c * LANES:(c + 1) * LANES].astype(F32))

    y = jnp.concatenate([v.astype(BF16) for v in ys], axis=1)
    out_ref[...] = h_ref[...] + jnp.dot(y, wout_ref[...], preferred_element_type=F32)


def _rope_lane_tables(n_tok, lp):
    l = N_META + n_tok
    rows = n_tok // GRID_W
    lane = jnp.arange(LANES, dtype=jnp.int32)

    def inv_freq(idx, dim, theta):
        return theta ** (-(2 * idx).astype(F32) / dim)

    def padded(v):
        return jnp.pad(v.astype(F32), (0, lp - l))[:, None]

    pos = padded(jnp.arange(l, dtype=jnp.int32))
    row_pos = padded(jnp.concatenate([jnp.full((N_META,), -1, jnp.int32),
                                      jnp.repeat(jnp.arange(rows, dtype=jnp.int32), GRID_W)]))
    col_pos = padded(jnp.concatenate([jnp.arange(N_META, dtype=jnp.int32),
                                      jnp.tile(jnp.arange(GRID_W, dtype=jnp.int32), rows)]))

    ja = lane % A_QK
    inv_a = jnp.where(ja < A_ROT, inv_freq(ja % (A_ROT // 2), A_ROT, ROPE_THETA), 0.0)
    sign_a = jnp.where(ja < A_ROT // 2, -1.0, 1.0).astype(F32)
    ang_a = pos * inv_a[None, :]
    jb = lane % (B_HD // 2)
    inv_b = inv_freq(jb % (B_HD // 4), B_HD // 2, AXIAL_THETA)
    inv_r = jnp.where(lane < B_HD // 2, inv_b, 0.0)
    inv_c = jnp.where(lane < B_HD // 2, 0.0, inv_b)
    sign_b = jnp.where(jb < B_HD // 4, -1.0, 1.0).astype(F32)
    ang_b = row_pos * inv_r[None, :] + col_pos * inv_c[None, :]
    inv_m = jnp.where(lane < C_ROPE, inv_freq(lane % (C_ROPE // 2), C_ROPE, ROPE_THETA), 0.0)
    sign_m = jnp.where(lane < C_ROPE // 2, -1.0, 1.0).astype(F32)
    ang_m = pos * inv_m[None, :]
    return (jnp.cos(ang_a), jnp.sin(ang_a) * sign_a[None, :],
            jnp.cos(ang_b), jnp.sin(ang_b) * sign_b[None, :],
            jnp.cos(ang_m), jnp.sin(ang_m) * sign_m[None, :])


def _row_tile_kernel(tile_fn, n_out, n_scratch, tl, seq_len):
    def kern(*refs):
        outs = refs[len(refs) - n_scratch - n_out:len(refs) - n_scratch]
        has_rows = pl.program_id(0) * tl < seq_len

        @pl.when(has_rows)
        def _():
            tile_fn(*refs)

        @pl.when(jnp.logical_not(has_rows))
        def _():
            for o in outs:
                o[...] = jnp.zeros(o.shape, o.dtype)
    return kern


class _TokenRows:
    def __init__(self, meta_ref, prev_ref, body_refs, seq_len, tl):
        self._parts = (meta_ref, prev_ref, body_refs)
        self._seq_len, self._tl = seq_len, tl

    def __getitem__(self, idx):
        meta_ref, prev_ref, body_refs = self._parts
        i = pl.program_id(0)
        head = jnp.where(i == 0, meta_ref[...], prev_ref[...])
        body = [r[...] for r in body_refs]
        body[-1] = body[-1][:body[-1].shape[0] - N_META]
        rows = jnp.concatenate([head] + body, axis=0)
        r_nat = i * self._tl + lax.broadcasted_iota(jnp.int32, rows.shape, 0)
        return jnp.where(r_nat < self._seq_len, rows, 0.0)


def _with_token_rows(tile_fn, seq_len, tl):
    def fn(meta_ref, prev_ref, b0_ref, b1_ref, b2_ref, *rest):
        tile_fn(_TokenRows(meta_ref, prev_ref, (b0_ref, b1_ref, b2_ref), seq_len, tl), *rest)
    return fn


def _token_row_specs(tl, n_tok):
    sub = tl // 3
    n_sub = n_tok // sub
    return [_const_spec((N_META, D_MODEL)),
            pl.BlockSpec((N_META, D_MODEL),
                         lambda i: (jnp.clip(i * (tl // N_META) - 1, 0, n_tok // N_META - 1), 0)),
            pl.BlockSpec((sub, D_MODEL), lambda i: (jnp.minimum(3 * i, n_sub - 1), 0)),
            pl.BlockSpec((sub, D_MODEL), lambda i: (jnp.minimum(3 * i + 1, n_sub - 1), 0)),
            pl.BlockSpec((sub, D_MODEL), lambda i: (jnp.minimum(3 * i + 2, n_sub - 1), 0))]


def _row_spec(tl, width):
    return pl.BlockSpec((tl, width), lambda i: (i, 0))


def _const_spec(shape):
    nd = len(shape)
    return pl.BlockSpec(shape, lambda i: (0,) * nd, pipeline_mode=pl.Buffered(1))


def _proj_call(h, lw, tables, lp, tl, seq_len):
    n_tiles = lp // tl
    small = [lw["aqf"], lw["akf"], lw["aqn"], lw["akn"], lw["bqn"], lw["bkn"], lw["cql"], lw["ckvl"],
             lw["cqnn"], lw["cqnr"], lw["cknn"], lw["cknr"]]
    first = isinstance(h, tuple)
    h_specs = _token_row_specs(tl, h[1].shape[0]) if first else [_row_spec(tl, D_MODEL)]
    h_args = (h[0], h[1], h[1], h[1], h[1]) if first else (h,)
    tile_fn = _with_token_rows(_proj_tile, seq_len, tl) if first else _proj_tile
    in_specs = (h_specs + [_const_spec((1, D_MODEL)),
                 pl.BlockSpec((None, D_MODEL, P_HEAD_END), lambda i: (lw["layer"], 0, 0),
                              pipeline_mode=pl.Buffered(1)),
                 _const_spec((D_MODEL, IN_COLS - P_TAIL)),
                 _const_spec(lw["wuq"].shape),
                 _const_spec(lw["wukv"].shape)]
                + [_const_spec(a.shape) for a in small]
                + [_row_spec(tl, LANES)] * 6)
    out_widths = [512, 2048, 512, 512, 256, 256, 1024, 1024, 512, 2048, 512]
    out_dtypes = [BF16, F8] + [BF16] * 8 + [F32]
    out_shape = [jax.ShapeDtypeStruct((lp, w), dt) for w, dt in zip(out_widths, out_dtypes)]
    out_specs = [_row_spec(tl, w) for w in out_widths]
    return pl.pallas_call(
        _row_tile_kernel(tile_fn, len(out_widths), 0, tl, seq_len),
        grid=(n_tiles,),
        in_specs=in_specs,
        out_specs=out_specs,
        out_shape=out_shape,
        compiler_params=pltpu.CompilerParams(
            dimension_semantics=("arbitrary",), vmem_limit_bytes=VMEM_LIMIT),
        name="proj",
    )(*h_args, lw["nw"], lw["win"], lw["wtail"], lw["wuq"], lw["wukv"], *small, *tables)


def _attn_call(q, k, v, mode, lp, n_tok, tq, ck):
    n_heads = {"A": 4, "B": 2, "C": 4}[mode]
    n_grp = {"A": 2, "B": 2, "C": 1}[mode]
    qw = {"A": LANES, "B": 2 * LANES, "C": 2 * LANES}[mode]
    kw = {"A": 4 * LANES, "B": LANES, "C": 2 * LANES}[mode]
    n_main = n_tok // ck
    tail_valid = N_META + n_tok - n_main * ck
    kern = functools.partial(_attn_kernel, mode=mode, tq=tq, ck=ck, n_main=n_main,
                             tail_valid=tail_valid, q_valid=N_META + n_tok)
    return pl.pallas_call(
        kern,
        grid=(n_heads, lp // tq),
        in_specs=[pl.BlockSpec((tq, qw), lambda hd, i: (i, hd)),
                  pl.BlockSpec((lp, kw), lambda hd, i: (0, hd),
                               pipeline_mode=pl.Buffered(2 if mode == "B" else 1)),
                  pl.BlockSpec((lp, LANES), lambda hd, i: (0, hd))],
        out_specs=pl.BlockSpec((None, n_grp, tq, LANES), lambda hd, i: (hd, 0, i, 0)),
        out_shape=jax.ShapeDtypeStruct((n_heads, n_grp, lp, LANES), F32),
        scratch_shapes=[pltpu.VMEM((lp, 2 * LANES), BF16),
                        pltpu.VMEM((6, n_grp * tq, ck), F32),
                        pltpu.VMEM((6, n_grp * tq, LANES), F32),
                        pltpu.VMEM((3, n_grp * tq, LANES), F32),
                        pltpu.VMEM((n_grp * tq, LANES), F32),
                        pltpu.VMEM((n_grp * tq, 2 * LANES), F32)],
        compiler_params=pltpu.CompilerParams(
            dimension_semantics=("arbitrary", "arbitrary"), vmem_limit_bytes=VMEM_LIMIT),
        name="attn_" + mode,
    )(q, k, v)


def _final_call(h, oa, ob, oc, du, sg, lw, lambda_init, lp, n_tok, tl):
    n_tiles = lp // tl
    hb = tl // POOL_HALO
    n_hblk = lp // POOL_HALO
    seq_len = N_META + n_tok
    tile_fn = functools.partial(_final_tile, lambda_init=lambda_init, seq_len=seq_len, tl=tl)
    first = isinstance(h, tuple)
    if first:
        tile_fn = _with_token_rows(tile_fn, seq_len, tl)
    h_specs = _token_row_specs(tl, h[1].shape[0]) if first else [_row_spec(tl, D_MODEL)]
    h_args = (h[0], h[1], h[1], h[1], h[1]) if first else (h,)
    kern = _row_tile_kernel(tile_fn, 1, 1, tl, seq_len)
    in_specs = h_specs + [
        pl.BlockSpec((4, 2, tl, LANES), lambda i: (0, 0, i, 0)),
        pl.BlockSpec((2, 2, tl, LANES), lambda i: (0, 0, i, 0)),
        pl.BlockSpec((4, 1, tl, LANES), lambda i: (0, 0, i, 0)),
        _row_spec(tl, BRANCH_W),
        pl.BlockSpec((POOL_HALO, BRANCH_W), lambda i: (jnp.maximum(i * hb - 1, 0), 0)),
        pl.BlockSpec((POOL_HALO, BRANCH_W), lambda i: (jnp.minimum((i + 1) * hb, n_hblk - 1), 0)),
        _row_spec(tl, D_MODEL),
        _const_spec((D_MODEL, D_MODEL)),
        _const_spec((4, A_QK)),
        _const_spec((1, LANES)),
        _const_spec((4, LANES, LANES)),
        _const_spec((1, BRANCH_W)),
    ]
    return pl.pallas_call(
        kern,
        grid=(n_tiles,),
        in_specs=in_specs,
        out_specs=_row_spec(tl, D_MODEL),
        out_shape=jax.ShapeDtypeStruct((lp, D_MODEL), F32),
        scratch_shapes=[pltpu.VMEM((tl + 2 * POOL_HALO, BRANCH_W), F32)],
        compiler_params=pltpu.CompilerParams(
            dimension_semantics=("arbitrary",), vmem_limit_bytes=VMEM_LIMIT),
        name="final",
    )(*h_args, oa, ob, oc, du, du, du, sg, lw["wout"], lw["lam"], lw["subln"], lw["wg"], lw["dsc"])


def _layer_weights(i, norm_w, w_in, w_in_bf16, w_out, a_q_norm, a_k_norm, a_lambda, a_subln, b_q_norm,
                   b_k_norm, c_q_lat_norm, c_kv_lat_norm, c_w_uq, c_w_ukv, c_q_norm, c_k_norm,
                   d_w_group, d_scale):
    wtail = w_in[i, :, P_TAIL:].astype(BF16)
    uq = c_w_uq[i].reshape(C_Q_LORA, 4, C_NOPE + C_ROPE)
    uq = jnp.pad(uq, ((0, 0), (0, 0), (0, 2 * LANES - C_NOPE - C_ROPE)))
    wuq = uq.reshape(C_Q_LORA, 4 * 2 * LANES).astype(BF16)
    row = lambda v: v.reshape(1, -1).astype(F32)
    a_scale = (A_QK ** -0.5) * LOG2E
    tiny = jnp.finfo(F32).tiny
    q_mag = jnp.maximum(jnp.max(jnp.abs(a_q_norm[i])).astype(F32) * a_scale, tiny)
    k_mag = jnp.maximum(jnp.max(jnp.abs(a_k_norm[i])).astype(F32), tiny)
    shift = jnp.clip(jnp.round(0.5 * (jnp.log2(k_mag) - jnp.log2(q_mag))), -60.0, 60.0)
    aqf = jnp.full((1, LANES), a_scale, F32) * jnp.exp2(shift)
    akf = jnp.full((1, LANES), 1.0, F32) * jnp.exp2(-shift)
    pad_row = lambda v: jnp.pad(v.astype(F32), (0, LANES - v.shape[0])).reshape(1, LANES)
    return dict(
        layer=i, nw=row(norm_w[i]), win=w_in_bf16, wtail=wtail, wuq=wuq, aqf=aqf, akf=akf, wukv=c_w_ukv[i].astype(BF16),
        aqn=row(jnp.tile(a_q_norm[i], 2)), akn=row(jnp.tile(a_k_norm[i], 2)),
        bqn=row(b_q_norm[i]), bkn=row(b_k_norm[i]),
        cql=row(c_q_lat_norm[i]), ckvl=row(c_kv_lat_norm[i]),
        cqnn=row(c_q_norm[i][:C_NOPE]), cqnr=pad_row(c_q_norm[i][C_NOPE:]),
        cknn=row(c_k_norm[i][:C_NOPE]), cknr=pad_row(c_k_norm[i][C_NOPE:]),
        wout=w_out[i].astype(BF16), lam=a_lambda[i].astype(F32), subln=row(a_subln[i]),
        wg=d_w_group[i].astype(BF16), dsc=row(d_scale[i]),
    )


def kernel(x, meta_tokens, norm_w, w_in, w_out, a_q_norm, a_k_norm, a_lambda, a_subln, b_q_norm,
           b_k_norm, c_q_lat_norm, c_kv_lat_norm, c_w_uq, c_w_ukv, c_q_norm, c_k_norm,
           d_w_group, d_scale):
    b, n_tok, d = x.shape
    assert b == 1 and d == D_MODEL
    lp = n_tok + SEQ_PAD
    tl = ROW_TILE
    ck = KV_CHUNK
    assert lp % tl == 0 and lp % ATTN_ROWS == 0 and n_tok % GRID_W == 0
    assert n_tok % (2 * ck) == 0 and (n_tok // (2 * ck) - 2) % 3 == 0
    depth = norm_w.shape[0]

    assert tl % 3 == 0 and n_tok % (tl // 3) == 0 and tl % N_META == 0
    h = (meta_tokens.astype(x.dtype), x[0])
    tables = _rope_lane_tables(n_tok, lp)

    w_in_bf16 = w_in.astype(BF16)
    for i in range(depth):
        lambda_init = 0.8 - 0.6 * math.exp(-0.3 * i)
        lw = _layer_weights(i, norm_w, w_in, w_in_bf16, w_out, a_q_norm, a_k_norm, a_lambda, a_subln,
                            b_q_norm, b_k_norm, c_q_lat_norm, c_kv_lat_norm, c_w_uq, c_w_ukv,
                            c_q_norm, c_k_norm, d_w_group, d_scale)
        qa, ka, va, qb, kb, vb, qc, kc, vc, sg, du = _proj_call(h, lw, tables, lp, tl, N_META + n_tok)
        oa = _attn_call(qa, ka, va, "A", lp, n_tok, ATTN_ROWS // 2, ck)
        ob = _attn_call(qb, kb, vb, "B", lp, n_tok, ATTN_ROWS // 2, ck)
        oc = _attn_call(qc, kc, vc, "C", lp, n_tok, ATTN_ROWS, ck)
        h = _final_call(h, oa, ob, oc, du, sg, lw, lambda_init, lp, n_tok, tl)
    return h[N_META:N_META + n_tok][None]
```
